```python
import math
import jax, jax.numpy as jnp
from jax import lax
import numpy as np

D_MODEL = 2048
BATCH = 8
SEQ = 4096
DEPTH = 1

D_MIX = D_MODEL
HEAD_DIM = 128
DN_WIDTH = D_MIX // 2
DN_HEADS = DN_WIDTH // HEAD_DIM
FOX_WIDTH = D_MIX - DN_WIDTH
FOX_HEADS = FOX_WIDTH // HEAD_DIM
CONV_WIDTH = 4
DN_CHUNK = 64
Q_BLOCK = 128
MIX_SPLITS = (3 * DN_WIDTH, 4 * DN_WIDTH, 4 * DN_WIDTH + DN_HEADS, 4 * DN_WIDTH + 2 * DN_HEADS,
              4 * DN_WIDTH + 2 * DN_HEADS + 3 * FOX_WIDTH)
D_IN = 4 * DN_WIDTH + 2 * DN_HEADS + 3 * FOX_WIDTH + FOX_HEADS
N_GROUPS = 4
EXPERTS_PER_GROUP = 8
N_EXPERTS = N_GROUPS * EXPERTS_PER_GROUP
TOP_K = 2
D_EXPERT = D_MODEL // 4
MOE_BLOCK = 256
LN_EPS = 1e-5
NORM_EPS = 1e-6

kernel_name = 'hybrid_deltanet_fox_hmoe_deepnorm_adaln'


def layer_norm(x, g, b):
    xf = x.astype(jnp.float32)
    mu = jnp.mean(xf, axis=-1, keepdims=True)
    var = jnp.mean(jnp.square(xf - mu), axis=-1, keepdims=True)
    return ((xf - mu) * lax.rsqrt(var + LN_EPS) * g.astype(jnp.float32) + b.astype(jnp.float32)).astype(x.dtype)


def l2_normalize(x):
    return x * lax.rsqrt(jnp.sum(jnp.square(x), axis=-1, keepdims=True) + NORM_EPS)


def causal_short_conv(x, w):
    S = x.shape[1]
    xp = jnp.pad(x, ((0, 0), (CONV_WIDTH - 1, 0), (0, 0)))
    return sum(xp[:, i:i + S] * w[i] for i in range(CONV_WIDTH))


def gated_delta_rule(q, k, v, g, beta):
    B, H, S, Dk = q.shape
    Dv = v.shape[-1]
    C = DN_CHUNK
    nc = S // C
    chunk = lambda t: t.reshape(B, H, nc, C, *t.shape[3:])
    q = chunk(q * Dk ** -0.5)
    k = chunk(k)
    v = chunk(v)
    beta = chunk(beta)
    G = jnp.cumsum(chunk(g), axis=-1)
    incl = jnp.tril(jnp.ones((C, C), bool))
    strict = jnp.tril(jnp.ones((C, C), bool), -1)
    diff = G[..., :, None] - G[..., None, :]
    decay = jnp.where(incl, jnp.exp(jnp.where(incl, diff, 0.0)), 0.0)
    kb = k * beta[..., None]
    A = jnp.where(strict, jnp.einsum('bhncd,bhnsd->bhncs', kb, k) * decay, 0.0)
    eye = jnp.eye(C, dtype=jnp.float32)
    T = lax.linalg.triangular_solve(eye + A, jnp.broadcast_to(eye, A.shape),
                                    left_side=True, lower=True, unit_diagonal=True)
    u = T @ (v * beta[..., None])
    w = T @ (kb * jnp.exp(G)[..., None])
    qk = jnp.where(incl, jnp.einsum('bhncd,bhnsd->bhncs', q, k) * decay, 0.0)

    def step(state, inp):
        qc, kc, uc, wc, qkc, Gc = inp
        v_new = uc - wc @ state
        o = (qc * jnp.exp(Gc)[..., None]) @ state + qkc @ v_new
        G_last = Gc[..., -1:]
        state = state * jnp.exp(G_last)[..., None] + jnp.einsum(
            'bhcd,bhce->bhde', kc * jnp.exp(G_last - Gc)[..., None], v_new)
        return state, o

    state0 = jnp.zeros((B, H, Dk, Dv), jnp.float32)
    xs = tuple(jnp.moveaxis(t, 2, 0) for t in (q, k, u, w, qk, G))
    _, o = lax.scan(step, state0, xs)
    return jnp.moveaxis(o, 0, 2).reshape(B, H, S, Dv)


def forgetting_attention(q, k, v, log_f):
    B, H, S, Dh = q.shape
    nb = S // Q_BLOCK
    F = jnp.cumsum(log_f, axis=-1)
    qb = jnp.moveaxis(q.reshape(B, H, nb, Q_BLOCK, Dh), 2, 0)
    Fb = jnp.moveaxis(F.reshape(B, H, nb, Q_BLOCK), 2, 0)
    k_pos = jnp.arange(S)

    def block(inp):
        qi, Fi, i = inp
        logits = jnp.einsum('bhqd,bhkd->bhqk', qi, k).astype(jnp.float32) * Dh ** -0.5
        logits = logits + Fi[..., :, None] - F[..., None, :]
        q_pos = i * Q_BLOCK + jnp.arange(Q_BLOCK)
        logits = jnp.where(k_pos[None, :] <= q_pos[:, None], logits, -jnp.inf)
        p = jax.nn.softmax(logits, axis=-1)
        return jnp.einsum('bhqk,bhkd->bhqd', p.astype(v.dtype), v)

    o = lax.map(block, (qb, Fb, jnp.arange(nb)))
    return jnp.moveaxis(o, 0, 2).reshape(B, H, S, Dh)


def hybrid_mixer(h, w_in, conv_w, a_log, dt_bias, norm_w, f_bias, w_out):
    B, S, _ = h.shape
    proj = h @ w_in
    dn_qkv, dn_z, dn_b, dn_a, fox_qkv, fox_f = jnp.split(proj, MIX_SPLITS, axis=-1)
    heads = lambda t, nh: t.reshape(B, S, nh, -1).transpose(0, 2, 1, 3)
    f32 = jnp.float32
    dn_qkv = jax.nn.silu(causal_short_conv(dn_qkv, conv_w))
    dq, dk, dv = jnp.split(dn_qkv.astype(f32), 3, axis=-1)
    dq = l2_normalize(heads(dq, DN_HEADS))
    dk = l2_normalize(heads(dk, DN_HEADS))
    dv = heads(dv, DN_HEADS)
    beta = jax.nn.sigmoid(dn_b.astype(f32)).transpose(0, 2, 1)
    g = (-jnp.exp(a_log.astype(f32)) * jax.nn.softplus(dn_a.astype(f32) + dt_bias.astype(f32))).transpose(0, 2, 1)
    o_dn = gated_delta_rule(dq, dk, dv, g, beta).transpose(0, 2, 1, 3)
    z = dn_z.astype(f32).reshape(B, S, DN_HEADS, HEAD_DIM)
    o_dn = o_dn * lax.rsqrt(jnp.mean(jnp.square(o_dn), axis=-1, keepdims=True) + NORM_EPS) \
        * norm_w.astype(f32) * jax.nn.silu(z)
    o_dn = o_dn.reshape(B, S, DN_WIDTH).astype(h.dtype)
    fq, fk, fv = jnp.split(fox_qkv, 3, axis=-1)
    log_f = jax.nn.log_sigmoid(fox_f.astype(f32) + f_bias.astype(f32)).transpose(0, 2, 1)
    o_fox = forgetting_attention(heads(fq, FOX_HEADS), heads(fk, FOX_HEADS), heads(fv, FOX_HEADS), log_f)
    o_fox = o_fox.transpose(0, 2, 1, 3).reshape(B, S, FOX_WIDTH)
    return jnp.concatenate([o_dn, o_fox], axis=-1) @ w_out


def hier_moe(h, w_rg, b_rg, w_re, b_re, w_gate, w_up, w_down):
    B, S, D = h.shape
    hf = h.reshape(-1, D)
    N = hf.shape[0]
    f32 = jnp.float32
    g_logits = (hf @ w_rg + b_rg).astype(f32)
    g_prob = jax.nn.softmax(g_logits, axis=-1)
    g_idx = jnp.argmax(g_logits, axis=-1)
    g_w = jnp.take_along_axis(g_prob, g_idx[:, None], axis=-1)
    e_logits = jnp.einsum('nd,gde->nge', hf, w_re) + b_re
    e_logits = jnp.take_along_axis(e_logits, g_idx[:, None, None], axis=1)[:, 0].astype(f32)
    e_prob = jax.nn.softmax(e_logits, axis=-1)
    top_p, top_e = lax.top_k(e_prob, TOP_K)
    weights = g_w * top_p / jnp.sum(top_p, axis=-1, keepdims=True)
    expert = g_idx[:, None].astype(jnp.int32) * EXPERTS_PER_GROUP + top_e.astype(jnp.int32)
    M = N * TOP_K
    P = (-(-M // MOE_BLOCK) + N_EXPERTS) * MOE_BLOCK
    n_blocks = P // MOE_BLOCK
    e_flat = expert.reshape(-1)
    tok_flat = jnp.repeat(jnp.arange(N, dtype=jnp.int32), TOP_K)
    w_flat = weights.reshape(-1)
    order = jnp.argsort(e_flat)
    e_s, tok_s, w_s = e_flat[order], tok_flat[order], w_flat[order]
    sizes = jnp.zeros((N_EXPERTS,), jnp.int32).at[e_flat].add(1)
    padded = (sizes + MOE_BLOCK - 1) // MOE_BLOCK * MOE_BLOCK
    p_end = jnp.cumsum(padded)
    p_start = p_end - padded
    u_start = jnp.cumsum(sizes) - sizes
    dest = p_start[e_s] + jnp.arange(M, dtype=jnp.int32) - u_start[e_s]
    tok_buf = jnp.zeros((P,), jnp.int32).at[dest].set(tok_s)
    w_buf = jnp.zeros((P,), hf.dtype).at[dest].set(w_s.astype(hf.dtype))
    blk_expert = jnp.minimum(jnp.searchsorted(p_end, jnp.arange(n_blocks, dtype=jnp.int32) * MOE_BLOCK,
                                              side='right'), N_EXPERTS - 1)

    def expert_block(inp):
        tok, wt, e = inp
        xb = hf[tok]
        hid = jax.nn.silu(xb @ w_gate[e]) * (xb @ w_up[e])
        return (hid @ w_down[e]) * wt[:, None]

    y_buf = lax.map(expert_block, (tok_buf.reshape(n_blocks, MOE_BLOCK),
                                   w_buf.reshape(n_blocks, MOE_BLOCK), blk_expert))
    y = jax.ops.segment_sum(y_buf.reshape(P, D), tok_buf, num_segments=N)
    return y.reshape(B, S, D)


def setup_inputs(seed: int = 0) -> dict:
    key = jax.random.key(seed)
    ks = jax.random.split(key, 32)
    L = DEPTH
    f32 = jnp.float32
    nrm = lambda k, shape, scale: jax.random.normal(k, shape, f32) * scale
    beta_dn = (8 * DEPTH) ** -0.25
    s_in = D_MODEL ** -0.5
    x = nrm(ks[0], (BATCH, SEQ, D_MODEL), 1.0)
    c = nrm(ks[1], (BATCH, D_MODEL), 1.0)
    w_ada = nrm(ks[2], (L, D_MODEL, 6 * D_MODEL), 0.1 * s_in)
    b_ada = nrm(ks[3], (L, 6 * D_MODEL), 0.01)
    w_in = jnp.concatenate([
        nrm(ks[4], (L, D_MODEL, 2 * DN_WIDTH), s_in),
        nrm(ks[5], (L, D_MODEL, DN_WIDTH), s_in * beta_dn),
        nrm(ks[6], (L, D_MODEL, DN_WIDTH + 2 * DN_HEADS), s_in),
        nrm(ks[7], (L, D_MODEL, 2 * FOX_WIDTH), s_in),
        nrm(ks[8], (L, D_MODEL, FOX_WIDTH), s_in * beta_dn),
        nrm(ks[9], (L, D_MODEL, FOX_HEADS), s_in)], axis=-1)
    dn_conv_w = nrm(ks[10], (L, CONV_WIDTH, 3 * DN_WIDTH), CONV_WIDTH ** -0.5)
    dn_a_log = jnp.log(jax.random.uniform(ks[11], (L, DN_HEADS), f32, 1.0, 16.0))
    dt = jnp.exp(jax.random.uniform(ks[12], (L, DN_HEADS), f32, math.log(1e-3), math.log(1e-1)))
    dn_dt_bias = dt + jnp.log(-jnp.expm1(-dt))
    dn_norm_w = 1.0 + nrm(ks[13], (L, HEAD_DIM), 0.02)
    fox_f_bias = jax.random.uniform(ks[14], (L, FOX_HEADS), f32, 1.0, 5.0)
    w_out = nrm(ks[15], (L, D_MIX, D_MODEL), D_MIX ** -0.5 * beta_dn)
    ln1_g = 1.0 + nrm(ks[16], (L, D_MODEL), 0.02)
    ln1_b = nrm(ks[17], (L, D_MODEL), 0.02)
    w_router_group = nrm(ks[18], (L, D_MODEL, N_GROUPS), s_in)
    b_router_group = nrm(ks[19], (L, N_GROUPS), 0.01)
    w_router_expert = nrm(ks[20], (L, N_GROUPS, D_MODEL, EXPERTS_PER_GROUP), s_in)
    b_router_expert = nrm(ks[21], (L, N_GROUPS, EXPERTS_PER_GROUP), 0.01)
    w_gate = nrm(ks[22], (L, N_EXPERTS, D_MODEL, D_EXPERT), s_in)
    w_up = nrm(ks[23], (L, N_EXPERTS, D_MODEL, D_EXPERT), s_in)
    w_down = nrm(ks[24], (L, N_EXPERTS, D_EXPERT, D_MODEL), D_EXPERT ** -0.5 * beta_dn)
    ln2_g = 1.0 + nrm(ks[25], (L, D_MODEL), 0.02)
    ln2_b = nrm(ks[26], (L, D_MODEL), 0.02)
    return {'x': x, 'c': c, 'w_ada': w_ada, 'b_ada': b_ada, 'w_in': w_in, 'dn_conv_w': dn_conv_w,
            'dn_a_log': dn_a_log, 'dn_dt_bias': dn_dt_bias, 'dn_norm_w': dn_norm_w,
            'fox_f_bias': fox_f_bias, 'w_out': w_out, 'ln1_g': ln1_g, 'ln1_b': ln1_b,
            'w_router_group': w_router_group, 'b_router_group': b_router_group,
            'w_router_expert': w_router_expert, 'b_router_expert': b_router_expert,
            'w_gate': w_gate, 'w_up': w_up, 'w_down': w_down, 'ln2_g': ln2_g, 'ln2_b': ln2_b}


def reference(x, c, w_ada, b_ada, w_in, dn_conv_w, dn_a_log, dn_dt_bias, dn_norm_w, fox_f_bias,
              w_out, ln1_g, ln1_b, w_router_group, b_router_group, w_router_expert, b_router_expert,
              w_gate, w_up, w_down, ln2_g, ln2_b):
    alpha = (2 * DEPTH) ** 0.25
    for l in range(DEPTH):
        mod = jax.nn.silu(c) @ w_ada[l] + b_ada[l]
        sh1, sc1, g1, sh2, sc2, g2 = jnp.split(mod[:, None, :], 6, axis=-1)
        h = x * (1.0 + sc1) + sh1
        y = hybrid_mixer(h, w_in[l], dn_conv_w[l], dn_a_log[l], dn_dt_bias[l], dn_norm_w[l],
                         fox_f_bias[l], w_out[l])
        x = layer_norm(alpha * x + (1.0 + g1) * y, ln1_g[l], ln1_b[l])
        h = x * (1.0 + sc2) + sh2
        y = hier_moe(h, w_router_group[l], b_router_group[l], w_router_expert[l], b_router_expert[l],
                     w_gate[l], w_up[l], w_down[l])
        x = layer_norm(alpha * x + (1.0 + g2) * y, ln2_g[l], ln2_b[l])
    return x
```

```python
import functools
import math

import jax
import jax.numpy as jnp
from jax import lax
from jax.experimental import pallas as pl
from jax.experimental.pallas import tpu as pltpu

F32 = jnp.float32
BF16 = jnp.bfloat16
U32 = jnp.uint32
HIGHEST = lax.Precision.HIGHEST

D_MODEL = 2048
HEAD_DIM = 128
DN_HEADS = 8
FOX_HEADS = 8
DN_WIDTH = DN_HEADS * HEAD_DIM
FOX_WIDTH = FOX_HEADS * HEAD_DIM
CONV_WIDTH = 4
N_GROUPS = 4
EXPERTS_PER_GROUP = 8
N_EXPERTS = N_GROUPS * EXPERTS_PER_GROUP
TOP_K = 2
D_EXPERT = 512
MOE_BLOCK = 256
LN_EPS = 1e-5
NORM_EPS = 1e-6
ALPHA = 2.0 ** 0.25

D_MAIN = 3 * DN_WIDTH + DN_WIDTH + 3 * FOX_WIDTH
LANES = 128
BF16_SUBLANES = 16
DN_CHUNK = 128
NEG_BIG = -1e30
HALF = D_MODEL // 2
VMEM_LIMIT = 56 * 1024 * 1024


def _sigmoid(x):
    return 1.0 / (1.0 + jnp.exp(-x))


def _softplus(x):
    return jnp.maximum(x, 0.0) + jnp.log(1.0 + jnp.exp(-jnp.abs(x)))


def _mm(a, b):
    return jnp.dot(a.astype(BF16), b.astype(BF16), preferred_element_type=F32)


def _mm_nt(a, b):
    return lax.dot_general(a.astype(BF16), b.astype(BF16), (((1,), (1,)), ((), ())),
                           preferred_element_type=F32)


def _mm_tn(a, b):
    return lax.dot_general(a.astype(BF16), b.astype(BF16), (((0,), (0,)), ((), ())),
                           preferred_element_type=F32)


def _pack_halves(x):
    c = x.shape[1] // 2
    lo = pltpu.bitcast(x[:, :c].astype(BF16).astype(F32), U32)
    hi = pltpu.bitcast(x[:, c:].astype(BF16).astype(F32), U32)
    return (lo >> 16) | (hi & jnp.uint32(0xFFFF0000))


def _unpack_halves(p):
    lo = pltpu.bitcast(p << 16, F32)
    hi = pltpu.bitcast(p & jnp.uint32(0xFFFF0000), F32)
    return lo, hi


def _layer_norm(r, g, b):
    mu = jnp.mean(r, axis=-1, keepdims=True)
    d = r - mu
    var = jnp.mean(d * d, axis=-1, keepdims=True)
    return d * lax.rsqrt(var + LN_EPS) * g + b


def _ada_kernel(c_ref, w_ref, b_ref, o_ref):
    c = c_ref[...]
    a = c * _sigmoid(c)
    o_ref[...] = jnp.dot(a, w_ref[...], preferred_element_type=F32, precision=HIGHEST) + b_ref[...]


def _ada(c, w_ada, b_ada):
    bsz = c.shape[0]
    n_out = w_ada.shape[1]
    tn = 1024
    return pl.pallas_call(
        _ada_kernel,
        grid=(n_out // tn,),
        in_specs=[pl.BlockSpec((bsz, D_MODEL), lambda j: (0, 0)),
                  pl.BlockSpec((D_MODEL, tn), lambda j: (0, j)),
                  pl.BlockSpec((1, tn), lambda j: (0, j))],
        out_specs=pl.BlockSpec((bsz, tn), lambda j: (0, j)),
        out_shape=jax.ShapeDtypeStruct((bsz, n_out), F32),
        compiler_params=pltpu.CompilerParams(dimension_semantics=("arbitrary",),
                                             vmem_limit_bytes=VMEM_LIMIT),
    )(c, w_ada, b_ada.reshape(1, n_out))


def _inproj_kernel(x_ref, sc_ref, sh_ref, w_ref, ws_ref, par_ref, o_ref, g_ref, h_scr):
    @pl.when(pl.program_id(1) == 0)
    def _():
        h = (x_ref[...] * (1.0 + sc_ref[0]) + sh_ref[0]).astype(BF16)
        h_scr[...] = h
        s = jnp.dot(h, ws_ref[...], preferred_element_type=F32)
        lane = lax.broadcasted_iota(jnp.int32, s.shape, 1)
        sb = s + par_ref[0:1, :]
        neg_a = -jnp.exp(par_ref[1:2, :])
        beta = _sigmoid(s)
        g = neg_a * _softplus(sb)
        log_f = -_softplus(-sb)
        g_ref[...] = jnp.where(lane < DN_HEADS, beta,
                               jnp.where(lane < 2 * DN_HEADS, g,
                                         jnp.where(lane < 2 * DN_HEADS + FOX_HEADS, log_f, 0.0)))

    o_ref[...] = jnp.dot(h_scr[...], w_ref[...], preferred_element_type=F32).astype(BF16)


def _inproj(x2, sc1, sh1, w_main, w_small, gate_par, seq):
    n = x2.shape[0]
    tm = min(1024, seq)
    tn = 1024
    per_batch = seq // tm
    return pl.pallas_call(
        _inproj_kernel,
        grid=(n // tm, D_MAIN // tn),
        in_specs=[pl.BlockSpec((tm, D_MODEL), lambda i, j: (i, 0)),
                  pl.BlockSpec((1, 1, D_MODEL), lambda i, j: (i // per_batch, 0, 0)),
                  pl.BlockSpec((1, 1, D_MODEL), lambda i, j: (i // per_batch, 0, 0)),
                  pl.BlockSpec((D_MODEL, tn), lambda i, j: (0, j)),
                  pl.BlockSpec((D_MODEL, LANES), lambda i, j: (0, 0)),
                  pl.BlockSpec((2, LANES), lambda i, j: (0, 0))],
        out_specs=[pl.BlockSpec((tm, tn), lambda i, j: (i, j)),
                   pl.BlockSpec((tm, LANES), lambda i, j: (i, 0))],
        out_shape=[jax.ShapeDtypeStruct((n, D_MAIN), BF16),
                   jax.ShapeDtypeStruct((n, LANES), F32)],
        scratch_shapes=[pltpu.VMEM((tm, D_MODEL), BF16)],
        compiler_params=pltpu.CompilerParams(dimension_semantics=("arbitrary", "arbitrary"),
                                             vmem_limit_bytes=VMEM_LIMIT),
    )(x2, sc1, sh1, w_main, w_small, gate_par)


def _cumsum_kernel(g_ref, f_ref, gc_ref, carry):
    @pl.when(pl.program_id(1) == 0)
    def _():
        carry[...] = jnp.zeros_like(carry)

    x = g_ref[0]
    t = x.shape[0]
    row = lax.broadcasted_iota(jnp.int32, (t, t), 0)
    col = lax.broadcasted_iota(jnp.int32, (t, t), 1)
    tri = (col <= row).astype(F32)
    shift = int(math.log2(DN_CHUNK))
    tri_chunk = jnp.where((row >> shift) == (col >> shift), tri, 0.0)
    cs = jnp.dot(tri, x, preferred_element_type=F32, precision=HIGHEST)
    f_ref[0] = cs + carry[...]
    gc_ref[0] = jnp.dot(tri_chunk, x, preferred_element_type=F32, precision=HIGHEST)
    carry[...] = carry[...] + cs[t - 1:t, :]


def _cumsum(gates3):
    bsz, seq, _ = gates3.shape
    t = min(512, seq)
    spec = pl.BlockSpec((1, t, LANES), lambda b, i: (b, i, 0))
    return pl.pallas_call(
        _cumsum_kernel,
        grid=(bsz, seq // t),
        in_specs=[spec],
        out_specs=[spec, spec],
        out_shape=[jax.ShapeDtypeStruct(gates3.shape, F32)] * 2,
        scratch_shapes=[pltpu.VMEM((1, LANES), F32)],
        compiler_params=pltpu.CompilerParams(dimension_semantics=("arbitrary", "arbitrary")),
    )(gates3)


def _unit_lower_inverse(a, row, col, eye):
    c = a.shape[0]
    n = jnp.where((row >> 3) == (col >> 3), -a, 0.0)
    x = eye + n
    n2 = _mm(n, n)
    x = x + _mm(n2, x)
    n4 = _mm(n2, n2)
    x = x + _mm(n4, x)
    s = 3
    while (1 << s) < c:
        low = jnp.where((row >> (s + 1)) == (col >> (s + 1)),
                        jnp.where((row >> s) == (col >> s), 0.0, a), 0.0)
        x = x - _mm(x, _mm(low, x))
        s += 1
    return x


def _dn_kernel(qkv_ref, halo_ref, z_ref, gate_ref, gc_ref, gct_ref, convw_ref, normw_ref,
               o_ref, state_ref):
    t = pl.program_id(1)

    @pl.when(t == 0)
    def _():
        state_ref[...] = jnp.zeros_like(state_ref)

    c = DN_CHUNK
    d = HEAD_DIM
    row = lax.broadcasted_iota(jnp.int32, (c, c), 0)
    col = lax.broadcasted_iota(jnp.int32, (c, c), 1)
    incl = col <= row
    strict = col < row
    eye = (row == col).astype(F32)
    halo_on = (t > 0).astype(F32)
    gates = gate_ref[0]
    gcs = gc_ref[0]
    gct = gct_ref[0]
    normw = normw_ref[...]

    def conv_silu(off):
        x = qkv_ref[0, :, off:off + d].astype(F32)
        hx = halo_ref[0, :, off:off + d].astype(F32) * halo_on
        xc = jnp.concatenate([hx, x], axis=0)
        w = convw_ref[:, off:off + d]
        acc = x * w[CONV_WIDTH - 1:CONV_WIDTH]
        for k in range(1, CONV_WIDTH):
            acc = acc + pltpu.roll(xc, k, axis=0)[BF16_SUBLANES:] * w[CONV_WIDTH - 1 - k:CONV_WIDTH - k]
        return acc * _sigmoid(acc)

    def l2n(x):
        return x * lax.rsqrt(jnp.sum(x * x, axis=-1, keepdims=True) + NORM_EPS)

    for h in range(DN_HEADS):
        q = l2n(conv_silu(h * d)) * (d ** -0.5)
        k = l2n(conv_silu(DN_WIDTH + h * d))
        v = conv_silu(2 * DN_WIDTH + h * d)
        beta = gates[:, h:h + 1]
        g_col = gcs[:, DN_HEADS + h:DN_HEADS + h + 1]
        g_row = gct[h:h + 1, :]
        diff = g_col - g_row
        decay = jnp.where(incl, jnp.exp(jnp.where(incl, diff, 0.0)), 0.0)
        e_g = jnp.exp(g_col)
        g_last = g_col[c - 1:c, :]
        kb = k * beta
        kk = _mm_nt(jnp.concatenate([kb, q], axis=0), k)
        a = jnp.where(strict, kk[:c] * decay, 0.0)
        qk = jnp.where(incl, kk[c:] * decay, 0.0)
        t_inv = _unit_lower_inverse(a, row, col, eye)
        uw = _mm(t_inv, jnp.concatenate([v * beta, kb * e_g], axis=1))
        u = uw[:, :d]
        w = uw[:, d:]
        s_old = state_ref[h]
        r = _mm(jnp.concatenate([w, q * e_g], axis=0), s_old)
        v_new = u - r[:c]
        o = r[c:] + _mm(qk, v_new)
        kd = k * jnp.exp(g_last - g_col)
        state_ref[h] = s_old * jnp.exp(g_last) + _mm_tn(kd, v_new)
        z = z_ref[0, :, h * d:(h + 1) * d].astype(F32)
        o = o * lax.rsqrt(jnp.mean(o * o, axis=-1, keepdims=True) + NORM_EPS) * normw * (z * _sigmoid(z))
        o_ref[0, :, h * d:(h + 1) * d] = o.astype(BF16)


def _deltanet(main3, gates3, gc3, gct, conv_w, norm_w):
    bsz, seq, _ = main3.shape
    c = DN_CHUNK
    halo_blocks = c // BF16_SUBLANES
    return pl.pallas_call(
        _dn_kernel,
        grid=(bsz, seq // c),
        in_specs=[pl.BlockSpec((1, c, 3 * DN_WIDTH), lambda b, t: (b, t, 0)),
                  pl.BlockSpec((1, BF16_SUBLANES, 3 * DN_WIDTH),
                               lambda b, t: (b, jnp.maximum(t * halo_blocks - 1, 0), 0)),
                  pl.BlockSpec((1, c, DN_WIDTH), lambda b, t: (b, t, 3)),
                  pl.BlockSpec((1, c, LANES), lambda b, t: (b, t, 0)),
                  pl.BlockSpec((1, c, LANES), lambda b, t: (b, t, 0)),
                  pl.BlockSpec((1, DN_HEADS, c), lambda b, t: (b, 0, t)),
                  pl.BlockSpec((CONV_WIDTH, 3 * DN_WIDTH), lambda b, t: (0, 0)),
                  pl.BlockSpec((1, HEAD_DIM), lambda b, t: (0, 0))],
        out_specs=pl.BlockSpec((1, c, DN_WIDTH), lambda b, t: (b, t, 0)),
        out_shape=jax.ShapeDtypeStruct((bsz, seq, DN_WIDTH), BF16),
        scratch_shapes=[pltpu.VMEM((DN_HEADS, HEAD_DIM, HEAD_DIM), F32)],
        compiler_params=pltpu.CompilerParams(dimension_semantics=("arbitrary", "arbitrary"),
                                             vmem_limit_bytes=VMEM_LIMIT),
    )(main3, main3, main3, gates3, gc3, gct, conv_w, norm_w.reshape(1, HEAD_DIM))


def _fox_kernel(q_ref, k_ref, v_ref, fq_ref, fk_ref, o_ref, m_scr, l_scr, acc_scr):
    i = pl.program_id(1)
    j = pl.program_id(2)
    d = HEAD_DIM
    tq = q_ref.shape[1]
    tk = k_ref.shape[1]

    @pl.when(j == 0)
    def _():
        m_scr[...] = jnp.full_like(m_scr, NEG_BIG)
        l_scr[...] = jnp.zeros_like(l_scr)
        acc_scr[...] = jnp.zeros_like(acc_scr)

    def update(diagonal):
        if diagonal:
            key_pos = lax.broadcasted_iota(jnp.int32, (tk, tq), 0)
            q_pos = lax.broadcasted_iota(jnp.int32, (tk, tq), 1)
            visible = key_pos <= q_pos
        for h in range(FOX_HEADS):
            q = (q_ref[0, :, h * d:(h + 1) * d].astype(F32) * (d ** -0.5)).astype(BF16)
            k = k_ref[0, :, h * d:(h + 1) * d]
            v = v_ref[0, :, h * d:(h + 1) * d]
            s = lax.dot_general(k, q, (((1,), (1,)), ((), ())), preferred_element_type=F32)
            fk = fk_ref[0, :, 2 * DN_HEADS + h:2 * DN_HEADS + h + 1]
            s = s + (fq_ref[0, h:h + 1, :] - fk)
            if diagonal:
                s = jnp.where(visible, s, NEG_BIG)
            m_old = m_scr[h]
            m_new = jnp.maximum(m_old, jnp.max(s, axis=0, keepdims=True))
            alpha = jnp.exp(m_old - m_new)
            p = jnp.exp(s - m_new)
            l_scr[h] = alpha * l_scr[h] + jnp.sum(p, axis=0, keepdims=True)
            acc_scr[h] = alpha * acc_scr[h] + _mm_tn(v, p)
            m_scr[h] = m_new

    @pl.when(j < i)
    def _():
        update(False)

    @pl.when(j == i)
    def _():
        update(True)
        for h in range(FOX_HEADS):
            o = acc_scr[h] / l_scr[h]
            o_ref[0, :, h * d:(h + 1) * d] = o.T.astype(BF16)


def _fox(main3, f_col, f_row):
    bsz, seq, _ = main3.shape
    tq = min(512, seq)
    n_blk = seq // tq
    w = FOX_WIDTH
    base = (4 * DN_WIDTH) // w
    return pl.pallas_call(
        _fox_kernel,
        grid=(bsz, n_blk, n_blk),
        in_specs=[pl.BlockSpec((1, tq, w), lambda b, i, j: (b, i, base)),
                  pl.BlockSpec((1, tq, w), lambda b, i, j: (b, jnp.minimum(i, j), base + 1)),
                  pl.BlockSpec((1, tq, w), lambda b, i, j: (b, jnp.minimum(i, j), base + 2)),
                  pl.BlockSpec((1, FOX_HEADS, tq), lambda b, i, j: (b, 0, i)),
                  pl.BlockSpec((1, tq, LANES), lambda b, i, j: (b, jnp.minimum(i, j), 0))],
        out_specs=pl.BlockSpec((1, tq, w), lambda b, i, j: (b, i, 0)),
        out_shape=jax.ShapeDtypeStruct((bsz, seq, w), BF16),
        scratch_shapes=[pltpu.VMEM((FOX_HEADS, 1, tq), F32),
                        pltpu.VMEM((FOX_HEADS, 1, tq), F32),
                        pltpu.VMEM((FOX_HEADS, HEAD_DIM, tq), F32)],
        compiler_params=pltpu.CompilerParams(
            dimension_semantics=("arbitrary", "arbitrary", "arbitrary"),
            vmem_limit_bytes=VMEM_LIMIT),
    )(main3, main3, main3, f_row, f_col)


ROUTE_ROWS = 8 + N_EXPERTS


def _outproj_kernel(odn_ref, ofox_ref, w1_ref, w2_ref, x_ref, g1_ref, sc2_ref, sh2_ref,
                    lng_ref, lnb_ref, wr_ref, br_ref, x1_ref, h2p_ref, route_ref):
    y = jnp.dot(odn_ref[...], w1_ref[...], preferred_element_type=F32)
    y = y + jnp.dot(ofox_ref[...], w2_ref[...], preferred_element_type=F32)
    r = ALPHA * x_ref[...] + (1.0 + g1_ref[0]) * y
    x1 = _layer_norm(r, lng_ref[...], lnb_ref[...])
    x1_ref[...] = x1
    h2 = x1 * (1.0 + sc2_ref[0]) + sh2_ref[0]
    h2p_ref[...] = _pack_halves(h2)

    lt = lax.dot_general(wr_ref[...], h2, (((1,), (1,)), ((), ())),
                         preferred_element_type=F32, precision=HIGHEST) + br_ref[...]
    tm = h2.shape[0]
    ridx = lax.broadcasted_iota(jnp.int32, (8, tm), 0)
    is_group = ridx < N_GROUPS
    gl = jnp.where(is_group, lt[0:8], NEG_BIG)
    gmax = jnp.max(gl, axis=0, keepdims=True)
    gidx = jnp.min(jnp.where(gl == gmax, ridx, 2 * N_EXPERTS), axis=0, keepdims=True)
    gsum = jnp.sum(jnp.where(is_group, jnp.exp(gl - gmax), 0.0), axis=0, keepdims=True)
    g_w = 1.0 / gsum
    el = jnp.zeros((8, tm), F32)
    for g in range(N_GROUPS):
        el = jnp.where(gidx == g, lt[8 + 8 * g:16 + 8 * g], el)
    emax = jnp.max(el, axis=0, keepdims=True)
    ee = jnp.exp(el - emax)
    prob = ee / jnp.sum(ee, axis=0, keepdims=True)
    p1 = jnp.max(prob, axis=0, keepdims=True)
    e1 = jnp.min(jnp.where(prob == p1, ridx, 2 * N_EXPERTS), axis=0, keepdims=True)
    prob2 = jnp.where(ridx == e1, -1.0, prob)
    p2 = jnp.max(prob2, axis=0, keepdims=True)
    e2 = jnp.min(jnp.where(prob2 == p2, ridx, 2 * N_EXPERTS), axis=0, keepdims=True)
    denom = p1 + p2
    ex1 = (gidx * EXPERTS_PER_GROUP + e1).astype(F32)
    ex2 = (gidx * EXPERTS_PER_GROUP + e2).astype(F32)
    route_ref[...] = jnp.where(ridx == 0, ex1,
                               jnp.where(ridx == 1, ex2,
                                         jnp.where(ridx == 2, g_w * p1 / denom,
                                                   jnp.where(ridx == 3, g_w * p2 / denom, 0.0))))


def _outproj(o_dn, o_fox, w1, w2, x2, g1, sc2, sh2, ln_g, ln_b, wr_t, br_t, seq):
    n = x2.shape[0]
    tm = min(512, seq)
    per_batch = seq // tm
    row = lambda i: (i, 0)
    fixed = lambda i: (0, 0)
    mod = pl.BlockSpec((1, 1, D_MODEL), lambda i: (i // per_batch, 0, 0))
    return pl.pallas_call(
        _outproj_kernel,
        grid=(n // tm,),
        in_specs=[pl.BlockSpec((tm, DN_WIDTH), row),
                  pl.BlockSpec((tm, FOX_WIDTH), row),
                  pl.BlockSpec((DN_WIDTH, D_MODEL), fixed),
                  pl.BlockSpec((FOX_WIDTH, D_MODEL), fixed),
                  pl.BlockSpec((tm, D_MODEL), row),
                  mod, mod, mod,
                  pl.BlockSpec((1, D_MODEL), fixed),
                  pl.BlockSpec((1, D_MODEL), fixed),
                  pl.BlockSpec((ROUTE_ROWS, D_MODEL), fixed),
                  pl.BlockSpec((ROUTE_ROWS, 1), fixed)],
        out_specs=[pl.BlockSpec((tm, D_MODEL), row),
                   pl.BlockSpec((tm, HALF), row),
                   pl.BlockSpec((8, tm), lambda i: (0, i))],
        out_shape=[jax.ShapeDtypeStruct((n, D_MODEL), F32),
                   jax.ShapeDtypeStruct((n, HALF), U32),
                   jax.ShapeDtypeStruct((8, n), F32)],
        compiler_params=pltpu.CompilerParams(dimension_semantics=("arbitrary",),
                                             vmem_limit_bytes=VMEM_LIMIT),
    )(o_dn, o_fox, w1, w2, x2, g1, sc2, sh2, ln_g.reshape(1, D_MODEL), ln_b.reshape(1, D_MODEL),
      wr_t, br_t)


def _row_gather_start(idx_ref, base, src_hbm, dst, sem, rows):
    def issue(r, carry):
        tok = idx_ref[base + r]
        pltpu.make_async_copy(src_hbm.at[pl.ds(tok, 1)], dst.at[pl.ds(r, 1)], sem).start()
        return carry
    lax.fori_loop(0, rows, issue, 0, unroll=8)


def _row_gather_wait(src_hbm, dst, sem, rows):
    pltpu.make_async_copy(src_hbm.at[pl.ds(0, rows)], dst, sem).wait()


def _moe_kernel(blk_expert_ref, n_active_ref, tok_ref, h_hbm, wt_ref, wg_ref, wu_ref, wd_ref,
                y_ref, xbuf, sem):
    b = pl.program_id(0)
    nb = pl.num_programs(0)
    slot = b % 2
    n_active = n_active_ref[0]

    @pl.when(b == 0)
    def _():
        _row_gather_start(tok_ref, 0, h_hbm, xbuf.at[0], sem.at[0], MOE_BLOCK)

    @pl.when(jnp.logical_and(b + 1 < nb, b + 1 < n_active))
    def _():
        _row_gather_start(tok_ref, (b + 1) * MOE_BLOCK, h_hbm, xbuf.at[1 - slot], sem.at[1 - slot],
                          MOE_BLOCK)

    @pl.when(jnp.logical_or(b < n_active, b == 0))
    def _():
        _row_gather_wait(h_hbm, xbuf.at[slot], sem.at[slot], MOE_BLOCK)
        lo, hi = _unpack_halves(xbuf[slot])
        lo = lo.astype(BF16)
        hi = hi.astype(BF16)
        gate = (jnp.dot(lo, wg_ref[0, :HALF, :], preferred_element_type=F32)
                + jnp.dot(hi, wg_ref[0, HALF:, :], preferred_element_type=F32))
        up = (jnp.dot(lo, wu_ref[0, :HALF, :], preferred_element_type=F32)
              + jnp.dot(hi, wu_ref[0, HALF:, :], preferred_element_type=F32))
        hid = (gate * _sigmoid(gate) * up).astype(BF16)
        y = jnp.dot(hid, wd_ref[0], preferred_element_type=F32) * wt_ref[...]
        y_ref[...] = _pack_halves(y)

    @pl.when(jnp.logical_and(b >= n_active, b > 0))
    def _():
        y_ref[...] = jnp.zeros_like(y_ref)


def _moe(blk_expert, n_active, tok_buf, h2p, w_buf, wg, wu, wd):
    n_blocks = blk_expert.shape[0]
    p = n_blocks * MOE_BLOCK
    grid_spec = pltpu.PrefetchScalarGridSpec(
        num_scalar_prefetch=3,
        grid=(n_blocks,),
        in_specs=[pl.BlockSpec(memory_space=pl.ANY),
                  pl.BlockSpec((MOE_BLOCK, 1), lambda b, be, na, tk: (b, 0)),
                  pl.BlockSpec((1, D_MODEL, D_EXPERT), lambda b, be, na, tk: (be[b], 0, 0)),
                  pl.BlockSpec((1, D_MODEL, D_EXPERT), lambda b, be, na, tk: (be[b], 0, 0)),
                  pl.BlockSpec((1, D_EXPERT, D_MODEL), lambda b, be, na, tk: (be[b], 0, 0))],
        out_specs=pl.BlockSpec((MOE_BLOCK, HALF), lambda b, be, na, tk: (b, 0)),
        scratch_shapes=[pltpu.VMEM((2, MOE_BLOCK, HALF), U32),
                        pltpu.SemaphoreType.DMA((2,))])
    return pl.pallas_call(
        _moe_kernel,
        grid_spec=grid_spec,
        out_shape=jax.ShapeDtypeStruct((p, HALF), U32),
        compiler_params=pltpu.CompilerParams(dimension_semantics=("arbitrary",),
                                             vmem_limit_bytes=VMEM_LIMIT),
    )(blk_expert, n_active, tok_buf, h2p, w_buf, wg, wu, wd)


def _combine_kernel(dest_ref, y_hbm, x1_ref, g2_ref, lng_ref, lnb_ref, o_ref, ybuf, sem):
    i = pl.program_id(0)
    nt = pl.num_programs(0)
    slot = i % 2
    rows = 2 * x1_ref.shape[0]

    @pl.when(i == 0)
    def _():
        _row_gather_start(dest_ref, 0, y_hbm, ybuf.at[0], sem.at[0], rows)

    @pl.when(i + 1 < nt)
    def _():
        _row_gather_start(dest_ref, (i + 1) * rows, y_hbm, ybuf.at[1 - slot], sem.at[1 - slot], rows)

    _row_gather_wait(y_hbm, ybuf.at[slot], sem.at[slot], rows)
    tm = x1_ref.shape[0]
    lo0, hi0 = _unpack_halves(ybuf[slot, :tm, :])
    lo1, hi1 = _unpack_halves(ybuf[slot, tm:, :])
    y = jnp.concatenate([lo0 + lo1, hi0 + hi1], axis=1)
    r = ALPHA * x1_ref[...] + (1.0 + g2_ref[0]) * y
    o_ref[...] = _layer_norm(r, lng_ref[...], lnb_ref[...])


def _combine(dest_tiled, y_buf, x1, g2, ln_g, ln_b, seq, tm):
    n = x1.shape[0]
    per_batch = seq // tm
    grid_spec = pltpu.PrefetchScalarGridSpec(
        num_scalar_prefetch=1,
        grid=(n // tm,),
        in_specs=[pl.BlockSpec(memory_space=pl.ANY),
                  pl.BlockSpec((tm, D_MODEL), lambda i, d: (i, 0)),
                  pl.BlockSpec((1, 1, D_MODEL), lambda i, d: (i // per_batch, 0, 0)),
                  pl.BlockSpec((1, D_MODEL), lambda i, d: (0, 0)),
                  pl.BlockSpec((1, D_MODEL), lambda i, d: (0, 0))],
        out_specs=pl.BlockSpec((tm, D_MODEL), lambda i, d: (i, 0)),
        scratch_shapes=[pltpu.VMEM((2, 2 * tm, HALF), U32),
                        pltpu.SemaphoreType.DMA((2,))])
    return pl.pallas_call(
        _combine_kernel,
        grid_spec=grid_spec,
        out_shape=jax.ShapeDtypeStruct((n, D_MODEL), F32),
        compiler_params=pltpu.CompilerParams(dimension_semantics=("arbitrary",),
                                             vmem_limit_bytes=VMEM_LIMIT),
    )(dest_tiled, y_buf, x1, g2, ln_g.reshape(1, D_MODEL), ln_b.reshape(1, D_MODEL))


def _dispatch_plan(route, n):
    m = n * TOP_K
    n_blocks = -(-m // MOE_BLOCK) + N_EXPERTS
    p = n_blocks * MOE_BLOCK
    expert = route[0:2].astype(jnp.int32).T.reshape(-1)
    weight = route[2:4].T.reshape(-1)
    onehot = (expert[:, None] == jnp.arange(N_EXPERTS, dtype=jnp.int32)[None, :]).astype(jnp.int32)
    csum = jnp.cumsum(onehot, axis=0)
    rank = jnp.take_along_axis(csum, expert[:, None], axis=1)[:, 0] - 1
    sizes = csum[-1]
    padded = (sizes + MOE_BLOCK - 1) // MOE_BLOCK * MOE_BLOCK
    p_end = jnp.cumsum(padded)
    p_start = p_end - padded
    dest = p_start[expert] + rank
    tok = jnp.arange(m, dtype=jnp.int32) // TOP_K
    tok_buf = jnp.zeros((p,), jnp.int32).at[dest].set(tok, unique_indices=True)
    w_buf = jnp.zeros((p,), F32).at[dest].set(weight, unique_indices=True)
    blk_expert = jnp.minimum(
        jnp.searchsorted(p_end, jnp.arange(n_blocks, dtype=jnp.int32) * MOE_BLOCK, side='right'),
        N_EXPERTS - 1).astype(jnp.int32)
    n_active = (p_end[-1] // MOE_BLOCK).astype(jnp.int32).reshape(1)
    return dest, tok_buf, w_buf.reshape(p, 1), blk_expert, n_active


def kernel(x, c, w_ada, b_ada, w_in, dn_conv_w, dn_a_log, dn_dt_bias, dn_norm_w, fox_f_bias, w_out,
           ln1_g, ln1_b, w_router_group, b_router_group, w_router_expert, b_router_expert,
           w_gate, w_up, w_down, ln2_g, ln2_b):
    bsz, seq, _ = x.shape
    n = bsz * seq
    x2 = x.reshape(n, D_MODEL)

    mod = _ada(c, w_ada[0], b_ada[0])
    sh1, sc1, g1, sh2, sc2, g2 = [m.reshape(bsz, 1, D_MODEL) for m in jnp.split(mod, 6, axis=-1)]

    wi = w_in[0]
    o_b = 4 * DN_WIDTH
    o_a = o_b + DN_HEADS
    o_fq = o_a + DN_HEADS
    o_f = o_fq + 3 * FOX_WIDTH
    w_main = jnp.concatenate([wi[:, :o_b], wi[:, o_fq:o_f]], axis=1).astype(BF16)
    w_small = jnp.concatenate([wi[:, o_b:o_fq], wi[:, o_f:],
                               jnp.zeros((D_MODEL, LANES - 2 * DN_HEADS - FOX_HEADS), F32)],
                              axis=1).astype(BF16)
    zeros8 = jnp.zeros((DN_HEADS,), F32)
    pad = jnp.zeros((LANES - 2 * DN_HEADS - FOX_HEADS,), F32)
    gate_par = jnp.stack([jnp.concatenate([zeros8, dn_dt_bias[0], fox_f_bias[0], pad]),
                          jnp.concatenate([zeros8, dn_a_log[0], zeros8, pad])])

    main, gates = _inproj(x2, sc1, sh1, w_main, w_small, gate_par, seq)
    main3 = main.reshape(bsz, seq, D_MAIN)
    gates3 = gates.reshape(bsz, seq, LANES)
    f_col, gc3 = _cumsum(gates3)
    gct = jnp.swapaxes(gc3[:, :, DN_HEADS:2 * DN_HEADS], 1, 2)
    f_row = jnp.swapaxes(f_col[:, :, 2 * DN_HEADS:2 * DN_HEADS + FOX_HEADS], 1, 2)

    o_dn = _deltanet(main3, gates3, gc3, gct, dn_conv_w[0], dn_norm_w[0])
    o_fox = _fox(main3, f_col, f_row)

    wo = w_out[0].astype(BF16)
    wr_t = jnp.concatenate([w_router_group[0].T, jnp.zeros((8 - N_GROUPS, D_MODEL), F32),
                            jnp.transpose(w_router_expert[0], (0, 2, 1)).reshape(N_EXPERTS, D_MODEL)],
                           axis=0)
    br_t = jnp.concatenate([b_router_group[0], jnp.zeros((8 - N_GROUPS,), F32),
                            b_router_expert[0].reshape(-1)]).reshape(ROUTE_ROWS, 1)
    x1, h2p, route = _outproj(o_dn.reshape(n, DN_WIDTH), o_fox.reshape(n, FOX_WIDTH),
                              wo[:DN_WIDTH], wo[DN_WIDTH:], x2, g1, sc2, sh2, ln1_g[0], ln1_b[0],
                              wr_t, br_t, seq)

    dest, tok_buf, w_buf, blk_expert, n_active = _dispatch_plan(route, n)
    y_buf = _moe(blk_expert, n_active, tok_buf, h2p, w_buf,
                 w_gate[0].astype(BF16), w_up[0].astype(BF16), w_down[0].astype(BF16))

    tm = min(256, seq)
    dest_tiled = dest.reshape(n // tm, tm, TOP_K).transpose(0, 2, 1).reshape(-1)
    out = _combine(dest_tiled, y_buf, x1, g2, ln2_g[0], ln2_b[0], seq, tm)
    return out.reshape(bsz, seq, D_MODEL)
```

```python
import functools
import math

import jax
import jax.numpy as jnp
from jax import lax
from jax.experimental import pallas as pl
from jax.experimental.pallas import tpu as pltpu

F32 = jnp.float32
BF16 = jnp.bfloat16
U32 = jnp.uint32
HIGHEST = lax.Precision.HIGHEST

D_MODEL = 2048
HEAD_DIM = 128
DN_HEADS = 8
FOX_HEADS = 8
DN_WIDTH = DN_HEADS * HEAD_DIM
FOX_WIDTH = FOX_HEADS * HEAD_DIM
CONV_WIDTH = 4
N_GROUPS = 4
EXPERTS_PER_GROUP = 8
N_EXPERTS = N_GROUPS * EXPERTS_PER_GROUP
TOP_K = 2
D_EXPERT = 512
MOE_BLOCK = 256
LN_EPS = 1e-5
NORM_EPS = 1e-6
ALPHA = 2.0 ** 0.25

D_MAIN = 3 * DN_WIDTH + DN_WIDTH + 3 * FOX_WIDTH
LANES = 128
BF16_SUBLANES = 16
DN_CHUNK = 128
NEG_BIG = -1e30
HALF = D_MODEL // 2
VMEM_LIMIT = 56 * 1024 * 1024


def _sigmoid(x):
    return 1.0 / (1.0 + jnp.exp(-x))


def _softplus(x):
    return jnp.maximum(x, 0.0) + jnp.log(1.0 + jnp.exp(-jnp.abs(x)))


def _mm(a, b):
    return jnp.dot(a.astype(BF16), b.astype(BF16), preferred_element_type=F32)


def _mm_nt(a, b):
    return lax.dot_general(a.astype(BF16), b.astype(BF16), (((1,), (1,)), ((), ())),
                           preferred_element_type=F32)


def _mm_tn(a, b):
    return lax.dot_general(a.astype(BF16), b.astype(BF16), (((0,), (0,)), ((), ())),
                           preferred_element_type=F32)


def _pack_halves(x):
    c = x.shape[1] // 2
    lo = pltpu.bitcast(x[:, :c].astype(BF16).astype(F32), U32)
    hi = pltpu.bitcast(x[:, c:].astype(BF16).astype(F32), U32)
    return (lo >> 16) | (hi & jnp.uint32(0xFFFF0000))


def _unpack_halves(p):
    lo = pltpu.bitcast(p << 16, F32)
    hi = pltpu.bitcast(p & jnp.uint32(0xFFFF0000), F32)
    return lo, hi


def _layer_norm(r, g, b):
    mu = jnp.mean(r, axis=-1, keepdims=True)
    d = r - mu
    var = jnp.mean(d * d, axis=-1, keepdims=True)
    return d * lax.rsqrt(var + LN_EPS) * g + b


def _ada_kernel(c_ref, w_ref, b_ref, o_ref):
    c = c_ref[...]
    a = c * _sigmoid(c)
    o_ref[...] = jnp.dot(a, w_ref[...], preferred_element_type=F32, precision=HIGHEST) + b_ref[...]


def _ada(c, w_ada, b_ada):
    bsz = c.shape[0]
    n_out = w_ada.shape[1]
    tn = 1024
    return pl.pallas_call(
        _ada_kernel,
        grid=(n_out // tn,),
        in_specs=[pl.BlockSpec((bsz, D_MODEL), lambda j: (0, 0)),
                  pl.BlockSpec((D_MODEL, tn), lambda j: (0, j)),
                  pl.BlockSpec((1, tn), lambda j: (0, j))],
        out_specs=pl.BlockSpec((bsz, tn), lambda j: (0, j)),
        out_shape=jax.ShapeDtypeStruct((bsz, n_out), F32),
        compiler_params=pltpu.CompilerParams(dimension_semantics=("arbitrary",),
                                             vmem_limit_bytes=VMEM_LIMIT),
    )(c, w_ada, b_ada.reshape(1, n_out))


def _inproj_kernel(x_ref, sc_ref, sh_ref, w_ref, ws_ref, par_ref, o_ref, g_ref, h_scr):
    @pl.when(pl.program_id(1) == 0)
    def _():
        h = (x_ref[...] * (1.0 + sc_ref[0]) + sh_ref[0]).astype(BF16)
        h_scr[...] = h
        s = jnp.dot(h, ws_ref[...], preferred_element_type=F32)
        lane = lax.broadcasted_iota(jnp.int32, s.shape, 1)
        sb = s + par_ref[0:1, :]
        neg_a = -jnp.exp(par_ref[1:2, :])
        beta = _sigmoid(s)
        g = neg_a * _softplus(sb)
        log_f = -_softplus(-sb)
        g_ref[...] = jnp.where(lane < DN_HEADS, beta,
                               jnp.where(lane < 2 * DN_HEADS, g,
                                         jnp.where(lane < 2 * DN_HEADS + FOX_HEADS, log_f, 0.0)))

    o_ref[...] = jnp.dot(h_scr[...], w_ref[...], preferred_element_type=F32).astype(BF16)


def _inproj(x2, sc1, sh1, w_main, w_small, gate_par, seq):
    n = x2.shape[0]
    tm = min(1024, seq)
    tn = 1024
    per_batch = seq // tm
    return pl.pallas_call(
        _inproj_kernel,
        grid=(n // tm, D_MAIN // tn),
        in_specs=[pl.BlockSpec((tm, D_MODEL), lambda i, j: (i, 0)),
                  pl.BlockSpec((1, 1, D_MODEL), lambda i, j: (i // per_batch, 0, 0)),
                  pl.BlockSpec((1, 1, D_MODEL), lambda i, j: (i // per_batch, 0, 0)),
                  pl.BlockSpec((D_MODEL, tn), lambda i, j: (0, j)),
                  pl.BlockSpec((D_MODEL, LANES), lambda i, j: (0, 0)),
                  pl.BlockSpec((2, LANES), lambda i, j: (0, 0))],
        out_specs=[pl.BlockSpec((tm, tn), lambda i, j: (i, j)),
                   pl.BlockSpec((tm, LANES), lambda i, j: (i, 0))],
        out_shape=[jax.ShapeDtypeStruct((n, D_MAIN), BF16),
                   jax.ShapeDtypeStruct((n, LANES), F32)],
        scratch_shapes=[pltpu.VMEM((tm, D_MODEL), BF16)],
        compiler_params=pltpu.CompilerParams(dimension_semantics=("arbitrary", "arbitrary"),
                                             vmem_limit_bytes=VMEM_LIMIT),
    )(x2, sc1, sh1, w_main, w_small, gate_par)


def _cumsum_kernel(g_ref, f_ref, gc_ref, carry):
    @pl.when(pl.program_id(1) == 0)
    def _():
        carry[...] = jnp.zeros_like(carry)

    x = g_ref[0]
    t = x.shape[0]
    row = lax.broadcasted_iota(jnp.int32, (t, t), 0)
    col = lax.broadcasted_iota(jnp.int32, (t, t), 1)
    tri = (col <= row).astype(F32)
    shift = int(math.log2(DN_CHUNK))
    tri_chunk = jnp.where((row >> shift) == (col >> shift), tri, 0.0)
    cs = jnp.dot(tri, x, preferred_element_type=F32, precision=HIGHEST)
    f_ref[0] = cs + carry[...]
    gc_ref[0] = jnp.dot(tri_chunk, x, preferred_element_type=F32, precision=HIGHEST)
    carry[...] = carry[...] + cs[t - 1:t, :]


def _cumsum(gates3):
    bsz, seq, _ = gates3.shape
    t = min(512, seq)
    spec = pl.BlockSpec((1, t, LANES), lambda b, i: (b, i, 0))
    return pl.pallas_call(
        _cumsum_kernel,
        grid=(bsz, seq // t),
        in_specs=[spec],
        out_specs=[spec, spec],
        out_shape=[jax.ShapeDtypeStruct(gates3.shape, F32)] * 2,
        scratch_shapes=[pltpu.VMEM((1, LANES), F32)],
        compiler_params=pltpu.CompilerParams(dimension_semantics=("arbitrary", "arbitrary")),
    )(gates3)


def _dn_kernel(qkv_ref, halo_ref, z_ref, gate_ref, gc_ref, gct_ref, convw_ref, normw_ref,
               o_ref, state_ref):
    t = pl.program_id(1)

    @pl.when(t == 0)
    def _():
        state_ref[...] = jnp.zeros_like(state_ref)

    c = DN_CHUNK
    d = HEAD_DIM
    heads = range(DN_HEADS)
    row = lax.broadcasted_iota(jnp.int32, (c, c), 0)
    col = lax.broadcasted_iota(jnp.int32, (c, c), 1)
    incl = col <= row
    strict_f = (col < row).astype(F32)
    incl_f = incl.astype(F32)
    eye = (row == col).astype(F32)
    diag8_f = ((row >> 3) == (col >> 3)).astype(F32)
    merge_levels = range(3, int(math.log2(c)))
    low_f = [jnp.where((row >> (s + 1)) == (col >> (s + 1)),
                       jnp.where((row >> s) == (col >> s), 0.0, 1.0), 0.0) for s in merge_levels]
    halo_on = (t > 0).astype(F32)
    gates = gate_ref[0]
    gcs = gc_ref[0]
    gct = gct_ref[0]
    normw = normw_ref[...]

    def conv_silu(off):
        x = qkv_ref[0, :, off:off + d].astype(F32)
        hx = halo_ref[0, :, off:off + d].astype(F32) * halo_on
        xc = jnp.concatenate([hx, x], axis=0)
        w = convw_ref[:, off:off + d]
        acc = x * w[CONV_WIDTH - 1:CONV_WIDTH]
        for k in range(1, CONV_WIDTH):
            acc = acc + pltpu.roll(xc, k, axis=0)[BF16_SUBLANES:] * w[CONV_WIDTH - 1 - k:CONV_WIDTH - k]
        return acc * _sigmoid(acc)

    def l2n(x):
        return x * lax.rsqrt(jnp.sum(x * x, axis=-1, keepdims=True) + NORM_EPS)

    def bdot(a, b):
        return jnp.dot(a, b, preferred_element_type=F32)

    q = [l2n(conv_silu(h * d)) * (d ** -0.5) for h in heads]
    k = [l2n(conv_silu(DN_WIDTH + h * d)) for h in heads]
    v = [conv_silu(2 * DN_WIDTH + h * d) for h in heads]
    beta = [gates[:, h:h + 1] for h in heads]
    g_col = [gcs[:, DN_HEADS + h:DN_HEADS + h + 1] for h in heads]
    e_g = [jnp.exp(g) for g in g_col]
    g_last = [g[c - 1:c, :] for g in g_col]
    decay = [jnp.exp(jnp.where(incl, g_col[h] - gct[h:h + 1, :], 0.0)) for h in heads]
    kb = [k[h] * beta[h] for h in heads]

    kk = [_mm_nt(jnp.concatenate([kb[h], q[h]], axis=0), k[h]) for h in heads]
    a = [kk[h][:c] * decay[h] * strict_f for h in heads]
    qk = [(kk[h][c:] * decay[h] * incl_f).astype(BF16) for h in heads]
    n1 = [-(a[h] * diag8_f) for h in heads]
    n1b = [n.astype(BF16) for n in n1]
    x = [eye + n for n in n1]
    xb = [xx.astype(BF16) for xx in x]
    n2b = [bdot(n, n).astype(BF16) for n in n1b]
    x = [x[h] + bdot(n2b[h], xb[h]) for h in heads]
    n4b = [bdot(n, n).astype(BF16) for n in n2b]
    xb = [xx.astype(BF16) for xx in x]
    x = [x[h] + bdot(n4b[h], xb[h]) for h in heads]
    for lf in low_f:
        xb = [xx.astype(BF16) for xx in x]
        lx = [bdot((a[h] * lf).astype(BF16), xb[h]).astype(BF16) for h in heads]
        x = [x[h] - bdot(xb[h], lx[h]) for h in heads]

    uw = [_mm(x[h], jnp.concatenate([v[h] * beta[h], kb[h] * e_g[h]], axis=1)) for h in heads]
    s_old = [state_ref[h] for h in heads]
    r = [_mm(jnp.concatenate([uw[h][:, d:], q[h] * e_g[h]], axis=0), s_old[h]) for h in heads]
    v_new = [(uw[h][:, :d] - r[h][:c]).astype(BF16) for h in heads]
    o = [r[h][c:] + bdot(qk[h], v_new[h]) for h in heads]
    for h in heads:
        kd = k[h] * jnp.exp(g_last[h] - g_col[h])
        state_ref[h] = s_old[h] * jnp.exp(g_last[h]) + _mm_tn(kd, v_new[h])
    for h in heads:
        z = z_ref[0, :, h * d:(h + 1) * d].astype(F32)
        rms = lax.rsqrt(jnp.mean(o[h] * o[h], axis=-1, keepdims=True) + NORM_EPS)
        o_ref[0, :, h * d:(h + 1) * d] = (o[h] * rms * normw * (z * _sigmoid(z))).astype(BF16)


def _deltanet(main3, gates3, gc3, gct, conv_w, norm_w):
    bsz, seq, _ = main3.shape
    c = DN_CHUNK
    halo_blocks = c // BF16_SUBLANES
    return pl.pallas_call(
        _dn_kernel,
        grid=(bsz, seq // c),
        in_specs=[pl.BlockSpec((1, c, 3 * DN_WIDTH), lambda b, t: (b, t, 0)),
                  pl.BlockSpec((1, BF16_SUBLANES, 3 * DN_WIDTH),
                               lambda b, t: (b, jnp.maximum(t * halo_blocks - 1, 0), 0)),
                  pl.BlockSpec((1, c, DN_WIDTH), lambda b, t: (b, t, 3)),
                  pl.BlockSpec((1, c, LANES), lambda b, t: (b, t, 0)),
                  pl.BlockSpec((1, c, LANES), lambda b, t: (b, t, 0)),
                  pl.BlockSpec((1, DN_HEADS, c), lambda b, t: (b, 0, t)),
                  pl.BlockSpec((CONV_WIDTH, 3 * DN_WIDTH), lambda b, t: (0, 0)),
                  pl.BlockSpec((1, HEAD_DIM), lambda b, t: (0, 0))],
        out_specs=pl.BlockSpec((1, c, DN_WIDTH), lambda b, t: (b, t, 0)),
        out_shape=jax.ShapeDtypeStruct((bsz, seq, DN_WIDTH), BF16),
        scratch_shapes=[pltpu.VMEM((DN_HEADS, HEAD_DIM, HEAD_DIM), F32)],
        compiler_params=pltpu.CompilerParams(dimension_semantics=("arbitrary", "arbitrary"),
                                             vmem_limit_bytes=VMEM_LIMIT),
    )(main3, main3, main3, gates3, gc3, gct, conv_w, norm_w.reshape(1, HEAD_DIM))


def _fox_kernel(q_ref, k_ref, v_ref, fq_ref, fk_ref, o_ref, m_scr, l_scr, acc_scr):
    i = pl.program_id(1)
    j = pl.program_id(2)
    d = HEAD_DIM
    tq = q_ref.shape[1]
    tk = k_ref.shape[1]

    @pl.when(j == 0)
    def _():
        m_scr[...] = jnp.full_like(m_scr, NEG_BIG)
        l_scr[...] = jnp.zeros_like(l_scr)
        acc_scr[...] = jnp.zeros_like(acc_scr)

    def update(diagonal):
        if diagonal:
            key_pos = lax.broadcasted_iota(jnp.int32, (tk, tq), 0)
            q_pos = lax.broadcasted_iota(jnp.int32, (tk, tq), 1)
            visible = key_pos <= q_pos
        for h in range(FOX_HEADS):
            q = (q_ref[0, :, h * d:(h + 1) * d].astype(F32) * (d ** -0.5)).astype(BF16)
            k = k_ref[0, :, h * d:(h + 1) * d]
            v = v_ref[0, :, h * d:(h + 1) * d]
            s = lax.dot_general(k, q, (((1,), (1,)), ((), ())), preferred_element_type=F32)
            fk = fk_ref[0, :, 2 * DN_HEADS + h:2 * DN_HEADS + h + 1]
            s = s + (fq_ref[0, h:h + 1, :] - fk)
            if diagonal:
                s = jnp.where(visible, s, NEG_BIG)
            m_old = m_scr[h]
            m_new = jnp.maximum(m_old, jnp.max(s, axis=0, keepdims=True))
            alpha = jnp.exp(m_old - m_new)
            p = jnp.exp(s - m_new)
            l_scr[h] = alpha * l_scr[h] + jnp.sum(p, axis=0, keepdims=True)
            acc_scr[h] = alpha * acc_scr[h] + _mm_tn(v, p)
            m_scr[h] = m_new

    @pl.when(j < i)
    def _():
        update(False)

    @pl.when(j == i)
    def _():
        update(True)
        for h in range(FOX_HEADS):
            o = acc_scr[h] / l_scr[h]
            o_ref[0, :, h * d:(h + 1) * d] = o.T.astype(BF16)


def _fox(main3, f_col, f_row):
    bsz, seq, _ = main3.shape
    tq = min(512, seq)
    n_blk = seq // tq
    w = FOX_WIDTH
    base = (4 * DN_WIDTH) // w
    return pl.pallas_call(
        _fox_kernel,
        grid=(bsz, n_blk, n_blk),
        in_specs=[pl.BlockSpec((1, tq, w), lambda b, i, j: (b, i, base)),
                  pl.BlockSpec((1, tq, w), lambda b, i, j: (b, jnp.minimum(i, j), base + 1)),
                  pl.BlockSpec((1, tq, w), lambda b, i, j: (b, jnp.minimum(i, j), base + 2)),
                  pl.BlockSpec((1, FOX_HEADS, tq), lambda b, i, j: (b, 0, i)),
                  pl.BlockSpec((1, tq, LANES), lambda b, i, j: (b, jnp.minimum(i, j), 0))],
        out_specs=pl.BlockSpec((1, tq, w), lambda b, i, j: (b, i, 0)),
        out_shape=jax.ShapeDtypeStruct((bsz, seq, w), BF16),
        scratch_shapes=[pltpu.VMEM((FOX_HEADS, 1, tq), F32),
                        pltpu.VMEM((FOX_HEADS, 1, tq), F32),
                        pltpu.VMEM((FOX_HEADS, HEAD_DIM, tq), F32)],
        compiler_params=pltpu.CompilerParams(
            dimension_semantics=("arbitrary", "arbitrary", "arbitrary"),
            vmem_limit_bytes=VMEM_LIMIT),
    )(main3, main3, main3, f_row, f_col)


ROUTE_ROWS = 8 + N_EXPERTS


def _outproj_kernel(odn_ref, ofox_ref, w1_ref, w2_ref, x_ref, g1_ref, sc2_ref, sh2_ref,
                    lng_ref, lnb_ref, wr_ref, br_ref, x1_ref, h2p_ref, route_ref):
    y = jnp.dot(odn_ref[...], w1_ref[...], preferred_element_type=F32)
    y = y + jnp.dot(ofox_ref[...], w2_ref[...], preferred_element_type=F32)
    r = ALPHA * x_ref[...] + (1.0 + g1_ref[0]) * y
    x1 = _layer_norm(r, lng_ref[...], lnb_ref[...])
    x1_ref[...] = x1
    h2 = x1 * (1.0 + sc2_ref[0]) + sh2_ref[0]
    h2p_ref[...] = _pack_halves(h2)

    lt = lax.dot_general(wr_ref[...], h2, (((1,), (1,)), ((), ())),
                         preferred_element_type=F32, precision=HIGHEST) + br_ref[...]
    tm = h2.shape[0]
    ridx = lax.broadcasted_iota(jnp.int32, (8, tm), 0)
    is_group = ridx < N_GROUPS
    gl = jnp.where(is_group, lt[0:8], NEG_BIG)
    gmax = jnp.max(gl, axis=0, keepdims=True)
    gidx = jnp.min(jnp.where(gl == gmax, ridx, 2 * N_EXPERTS), axis=0, keepdims=True)
    gsum = jnp.sum(jnp.where(is_group, jnp.exp(gl - gmax), 0.0), axis=0, keepdims=True)
    g_w = 1.0 / gsum
    el = jnp.zeros((8, tm), F32)
    for g in range(N_GROUPS):
        el = jnp.where(gidx == g, lt[8 + 8 * g:16 + 8 * g], el)
    emax = jnp.max(el, axis=0, keepdims=True)
    ee = jnp.exp(el - emax)
    prob = ee / jnp.sum(ee, axis=0, keepdims=True)
    p1 = jnp.max(prob, axis=0, keepdims=True)
    e1 = jnp.min(jnp.where(prob == p1, ridx, 2 * N_EXPERTS), axis=0, keepdims=True)
    prob2 = jnp.where(ridx == e1, -1.0, prob)
    p2 = jnp.max(prob2, axis=0, keepdims=True)
    e2 = jnp.min(jnp.where(prob2 == p2, ridx, 2 * N_EXPERTS), axis=0, keepdims=True)
    denom = p1 + p2
    ex1 = (gidx * EXPERTS_PER_GROUP + e1).astype(F32)
    ex2 = (gidx * EXPERTS_PER_GROUP + e2).astype(F32)
    route_ref[...] = jnp.where(ridx == 0, ex1,
                               jnp.where(ridx == 1, ex2,
                                         jnp.where(ridx == 2, g_w * p1 / denom,
                                                   jnp.where(ridx == 3, g_w * p2 / denom, 0.0))))


def _outproj(o_dn, o_fox, w1, w2, x2, g1, sc2, sh2, ln_g, ln_b, wr_t, br_t, seq):
    n = x2.shape[0]
    tm = min(512, seq)
    per_batch = seq // tm
    row = lambda i: (i, 0)
    fixed = lambda i: (0, 0)
    mod = pl.BlockSpec((1, 1, D_MODEL), lambda i: (i // per_batch, 0, 0))
    return pl.pallas_call(
        _outproj_kernel,
        grid=(n // tm,),
        in_specs=[pl.BlockSpec((tm, DN_WIDTH), row),
                  pl.BlockSpec((tm, FOX_WIDTH), row),
                  pl.BlockSpec((DN_WIDTH, D_MODEL), fixed),
                  pl.BlockSpec((FOX_WIDTH, D_MODEL), fixed),
                  pl.BlockSpec((tm, D_MODEL), row),
                  mod, mod, mod,
                  pl.BlockSpec((1, D_MODEL), fixed),
                  pl.BlockSpec((1, D_MODEL), fixed),
                  pl.BlockSpec((ROUTE_ROWS, D_MODEL), fixed),
                  pl.BlockSpec((ROUTE_ROWS, 1), fixed)],
        out_specs=[pl.BlockSpec((tm, D_MODEL), row),
                   pl.BlockSpec((tm, HALF), row),
                   pl.BlockSpec((8, tm), lambda i: (0, i))],
        out_shape=[jax.ShapeDtypeStruct((n, D_MODEL), F32),
                   jax.ShapeDtypeStruct((n, HALF), U32),
                   jax.ShapeDtypeStruct((8, n), F32)],
        compiler_params=pltpu.CompilerParams(dimension_semantics=("arbitrary",),
                                             vmem_limit_bytes=VMEM_LIMIT),
    )(o_dn, o_fox, w1, w2, x2, g1, sc2, sh2, ln_g.reshape(1, D_MODEL), ln_b.reshape(1, D_MODEL),
      wr_t, br_t)


def _row_gather_start(idx_ref, base, src_hbm, dst, sem, rows):
    def issue(r, carry):
        tok = idx_ref[base + r]
        pltpu.make_async_copy(src_hbm.at[pl.ds(tok, 1)], dst.at[pl.ds(r, 1)], sem).start()
        return carry
    lax.fori_loop(0, rows, issue, 0, unroll=8)


def _row_gather_wait(src_hbm, dst, sem, rows):
    pltpu.make_async_copy(src_hbm.at[pl.ds(0, rows)], dst, sem).wait()


def _moe_kernel(blk_expert_ref, n_active_ref, tok_ref, h_hbm, wt_ref, wg_ref, wu_ref, wd_ref,
                y_ref, xbuf, sem):
    b = pl.program_id(0)
    nb = pl.num_programs(0)
    slot = b % 2
    n_active = n_active_ref[0]

    @pl.when(b == 0)
    def _():
        _row_gather_start(tok_ref, 0, h_hbm, xbuf.at[0], sem.at[0], MOE_BLOCK)

    @pl.when(jnp.logical_and(b + 1 < nb, b + 1 < n_active))
    def _():
        _row_gather_start(tok_ref, (b + 1) * MOE_BLOCK, h_hbm, xbuf.at[1 - slot], sem.at[1 - slot],
                          MOE_BLOCK)

    @pl.when(jnp.logical_or(b < n_active, b == 0))
    def _():
        _row_gather_wait(h_hbm, xbuf.at[slot], sem.at[slot], MOE_BLOCK)
        lo, hi = _unpack_halves(xbuf[slot])
        lo = lo.astype(BF16)
        hi = hi.astype(BF16)
        gate = (jnp.dot(lo, wg_ref[0, :HALF, :], preferred_element_type=F32)
                + jnp.dot(hi, wg_ref[0, HALF:, :], preferred_element_type=F32))
        up = (jnp.dot(lo, wu_ref[0, :HALF, :], preferred_element_type=F32)
              + jnp.dot(hi, wu_ref[0, HALF:, :], preferred_element_type=F32))
        hid = (gate * _sigmoid(gate) * up).astype(BF16)
        y = jnp.dot(hid, wd_ref[0], preferred_element_type=F32) * wt_ref[...]
        y_ref[...] = _pack_halves(y)

    @pl.when(jnp.logical_and(b >= n_active, b > 0))
    def _():
        y_ref[...] = jnp.zeros_like(y_ref)


def _moe(blk_expert, n_active, tok_buf, h2p, w_buf, wg, wu, wd):
    n_blocks = blk_expert.shape[0]
    p = n_blocks * MOE_BLOCK
    grid_spec = pltpu.PrefetchScalarGridSpec(
        num_scalar_prefetch=3,
        grid=(n_blocks,),
        in_specs=[pl.BlockSpec(memory_space=pl.ANY),
                  pl.BlockSpec((MOE_BLOCK, 1), lambda b, be, na, tk: (b, 0)),
                  pl.BlockSpec((1, D_MODEL, D_EXPERT), lambda b, be, na, tk: (be[b], 0, 0)),
                  pl.BlockSpec((1, D_MODEL, D_EXPERT), lambda b, be, na, tk: (be[b], 0, 0)),
                  pl.BlockSpec((1, D_EXPERT, D_MODEL), lambda b, be, na, tk: (be[b], 0, 0))],
        out_specs=pl.BlockSpec((MOE_BLOCK, HALF), lambda b, be, na, tk: (b, 0)),
        scratch_shapes=[pltpu.VMEM((2, MOE_BLOCK, HALF), U32),
                        pltpu.SemaphoreType.DMA((2,))])
    return pl.pallas_call(
        _moe_kernel,
        grid_spec=grid_spec,
        out_shape=jax.ShapeDtypeStruct((p, HALF), U32),
        compiler_params=pltpu.CompilerParams(dimension_semantics=("arbitrary",),
                                             vmem_limit_bytes=VMEM_LIMIT),
    )(blk_expert, n_active, tok_buf, h2p, w_buf, wg, wu, wd)


def _combine_kernel(dest_ref, y_hbm, x1_ref, g2_ref, lng_ref, lnb_ref, o_ref, ybuf, sem):
    i = pl.program_id(0)
    nt = pl.num_programs(0)
    slot = i % 2
    rows = 2 * x1_ref.shape[0]

    @pl.when(i == 0)
    def _():
        _row_gather_start(dest_ref, 0, y_hbm, ybuf.at[0], sem.at[0], rows)

    @pl.when(i + 1 < nt)
    def _():
        _row_gather_start(dest_ref, (i + 1) * rows, y_hbm, ybuf.at[1 - slot], sem.at[1 - slot], rows)

    _row_gather_wait(y_hbm, ybuf.at[slot], sem.at[slot], rows)
    tm = x1_ref.shape[0]
    lo0, hi0 = _unpack_halves(ybuf[slot, :tm, :])
    lo1, hi1 = _unpack_halves(ybuf[slot, tm:, :])
    y = jnp.concatenate([lo0 + lo1, hi0 + hi1], axis=1)
    r = ALPHA * x1_ref[...] + (1.0 + g2_ref[0]) * y
    o_ref[...] = _layer_norm(r, lng_ref[...], lnb_ref[...])


def _combine(dest_tiled, y_buf, x1, g2, ln_g, ln_b, seq, tm):
    n = x1.shape[0]
    per_batch = seq // tm
    grid_spec = pltpu.PrefetchScalarGridSpec(
        num_scalar_prefetch=1,
        grid=(n // tm,),
        in_specs=[pl.BlockSpec(memory_space=pl.ANY),
                  pl.BlockSpec((tm, D_MODEL), lambda i, d: (i, 0)),
                  pl.BlockSpec((1, 1, D_MODEL), lambda i, d: (i // per_batch, 0, 0)),
                  pl.BlockSpec((1, D_MODEL), lambda i, d: (0, 0)),
                  pl.BlockSpec((1, D_MODEL), lambda i, d: (0, 0))],
        out_specs=pl.BlockSpec((tm, D_MODEL), lambda i, d: (i, 0)),
        scratch_shapes=[pltpu.VMEM((2, 2 * tm, HALF), U32),
                        pltpu.SemaphoreType.DMA((2,))])
    return pl.pallas_call(
        _combine_kernel,
        grid_spec=grid_spec,
        out_shape=jax.ShapeDtypeStruct((n, D_MODEL), F32),
        compiler_params=pltpu.CompilerParams(dimension_semantics=("arbitrary",),
                                             vmem_limit_bytes=VMEM_LIMIT),
    )(dest_tiled, y_buf, x1, g2, ln_g.reshape(1, D_MODEL), ln_b.reshape(1, D_MODEL))


def _dispatch_plan(route, n):
    m = n * TOP_K
    n_blocks = -(-m // MOE_BLOCK) + N_EXPERTS
    p = n_blocks * MOE_BLOCK
    expert = route[0:2].astype(jnp.int32).T.reshape(-1)
    weight = route[2:4].T.reshape(-1)
    onehot = (expert[:, None] == jnp.arange(N_EXPERTS, dtype=jnp.int32)[None, :]).astype(jnp.int32)
    csum = jnp.cumsum(onehot, axis=0)
    rank = jnp.take_along_axis(csum, expert[:, None], axis=1)[:, 0] - 1
    sizes = csum[-1]
    padded = (sizes + MOE_BLOCK - 1) // MOE_BLOCK * MOE_BLOCK
    p_end = jnp.cumsum(padded)
    p_start = p_end - padded
    dest = p_start[expert] + rank
    tok = jnp.arange(m, dtype=jnp.int32) // TOP_K
    tok_buf = jnp.zeros((p,), jnp.int32).at[dest].set(tok, unique_indices=True)
    w_buf = jnp.zeros((p,), F32).at[dest].set(weight, unique_indices=True)
    blk_expert = jnp.minimum(
        jnp.searchsorted(p_end, jnp.arange(n_blocks, dtype=jnp.int32) * MOE_BLOCK, side='right'),
        N_EXPERTS - 1).astype(jnp.int32)
    n_active = (p_end[-1] // MOE_BLOCK).astype(jnp.int32).reshape(1)
    return dest, tok_buf, w_buf.reshape(p, 1), blk_expert, n_active


def kernel(x, c, w_ada, b_ada, w_in, dn_conv_w, dn_a_log, dn_dt_bias, dn_norm_w, fox_f_bias, w_out,
           ln1_g, ln1_b, w_router_group, b_router_group, w_router_expert, b_router_expert,
           w_gate, w_up, w_down, ln2_g, ln2_b):
    bsz, seq, _ = x.shape
    n = bsz * seq
    x2 = x.reshape(n, D_MODEL)

    mod = _ada(c, w_ada[0], b_ada[0])
    sh1, sc1, g1, sh2, sc2, g2 = [m.reshape(bsz, 1, D_MODEL) for m in jnp.split(mod, 6, axis=-1)]

    wi = w_in[0]
    o_b = 4 * DN_WIDTH
    o_a = o_b + DN_HEADS
    o_fq = o_a + DN_HEADS
    o_f = o_fq + 3 * FOX_WIDTH
    w_main = jnp.concatenate([wi[:, :o_b], wi[:, o_fq:o_f]], axis=1).astype(BF16)
    w_small = jnp.concatenate([wi[:, o_b:o_fq], wi[:, o_f:],
                               jnp.zeros((D_MODEL, LANES - 2 * DN_HEADS - FOX_HEADS), F32)],
                              axis=1).astype(BF16)
    zeros8 = jnp.zeros((DN_HEADS,), F32)
    pad = jnp.zeros((LANES - 2 * DN_HEADS - FOX_HEADS,), F32)
    gate_par = jnp.stack([jnp.concatenate([zeros8, dn_dt_bias[0], fox_f_bias[0], pad]),
                          jnp.concatenate([zeros8, dn_a_log[0], zeros8, pad])])

    main, gates = _inproj(x2, sc1, sh1, w_main, w_small, gate_par, seq)
    main3 = main.reshape(bsz, seq, D_MAIN)
    gates3 = gates.reshape(bsz, seq, LANES)
    f_col, gc3 = _cumsum(gates3)
    gct = jnp.swapaxes(gc3[:, :, DN_HEADS:2 * DN_HEADS], 1, 2)
    f_row = jnp.swapaxes(f_col[:, :, 2 * DN_HEADS:2 * DN_HEADS + FOX_HEADS], 1, 2)

    o_dn = _deltanet(main3, gates3, gc3, gct, dn_conv_w[0], dn_norm_w[0])
    o_fox = _fox(main3, f_col, f_row)

    wo = w_out[0].astype(BF16)
    wr_t = jnp.concatenate([w_router_group[0].T, jnp.zeros((8 - N_GROUPS, D_MODEL), F32),
                            jnp.transpose(w_router_expert[0], (0, 2, 1)).reshape(N_EXPERTS, D_MODEL)],
                           axis=0)
    br_t = jnp.concatenate([b_router_group[0], jnp.zeros((8 - N_GROUPS,), F32),
                            b_router_expert[0].reshape(-1)]).reshape(ROUTE_ROWS, 1)
    x1, h2p, route = _outproj(o_dn.reshape(n, DN_WIDTH), o_fox.reshape(n, FOX_WIDTH),
                              wo[:DN_WIDTH], wo[DN_WIDTH:], x2, g1, sc2, sh2, ln1_g[0], ln1_b[0],
                              wr_t, br_t, seq)

    dest, tok_buf, w_buf, blk_expert, n_active = _dispatch_plan(route, n)
    y_buf = _moe(blk_expert, n_active, tok_buf, h2p, w_buf,
                 w_gate[0].astype(BF16), w_up[0].astype(BF16), w_down[0].astype(BF16))

    tm = min(256, seq)
    dest_tiled = dest.reshape(n // tm, tm, TOP_K).transpose(0, 2, 1).reshape(-1)
    out = _combine(dest_tiled, y_buf, x1, g2, ln2_g[0], ln2_b[0], seq, tm)
    return out.reshape(bsz, seq, D_MODEL)
```

```python
import math

import jax
import jax.numpy as jnp
from jax import lax
from jax.experimental import pallas as pl
from jax.experimental.pallas import tpu as pltpu

F32 = jnp.float32
BF16 = jnp.bfloat16
U32 = jnp.uint32
HIGHEST = lax.Precision.HIGHEST

D_MODEL = 2048
HEAD_DIM = 128
DN_HEADS = 8
FOX_HEADS = 8
DN_WIDTH = DN_HEADS * HEAD_DIM
FOX_WIDTH = FOX_HEADS * HEAD_DIM
CONV_WIDTH = 4
N_GROUPS = 4
EXPERTS_PER_GROUP = 8
N_EXPERTS = N_GROUPS * EXPERTS_PER_GROUP
TOP_K = 2
D_EXPERT = 512
MOE_BLOCK = 256
LN_EPS = 1e-5
NORM_EPS = 1e-6
ALPHA = 2.0 ** 0.25

D_MAIN = 3 * DN_WIDTH + DN_WIDTH + 3 * FOX_WIDTH
LANES = 128
BF16_SUBLANES = 16
DN_CHUNK = 128
NEG_BIG = -1e30
LOG2E = 1.4426950408889634
HALF = D_MODEL // 2
VMEM_LIMIT = 56 * 1024 * 1024


def _sigmoid(x):
    return 1.0 / (1.0 + jnp.exp(-x))


def _softplus(x):
    return jnp.maximum(x, 0.0) + jnp.log(1.0 + jnp.exp(-jnp.abs(x)))


def _mm(a, b):
    return jnp.dot(a.astype(BF16), b.astype(BF16), preferred_element_type=F32)


def _mm_nt(a, b):
    return lax.dot_general(a.astype(BF16), b.astype(BF16), (((1,), (1,)), ((), ())),
                           preferred_element_type=F32)


def _mm_tn(a, b):
    return lax.dot_general(a.astype(BF16), b.astype(BF16), (((0,), (0,)), ((), ())),
                           preferred_element_type=F32)


def _pack_halves(x):
    c = x.shape[1] // 2
    lo = pltpu.bitcast(x[:, :c].astype(BF16).astype(F32), U32)
    hi = pltpu.bitcast(x[:, c:].astype(BF16).astype(F32), U32)
    return (lo >> 16) | (hi & jnp.uint32(0xFFFF0000))


def _unpack_halves(p):
    lo = pltpu.bitcast(p << 16, F32)
    hi = pltpu.bitcast(p & jnp.uint32(0xFFFF0000), F32)
    return lo, hi


def _layer_norm(r, g, b):
    mu = jnp.mean(r, axis=-1, keepdims=True)
    d = r - mu
    var = jnp.mean(d * d, axis=-1, keepdims=True)
    return d * lax.rsqrt(var + LN_EPS) * g + b


def _ada_kernel(c_ref, w_ref, b_ref, o_ref):
    c = c_ref[...]
    a = c * _sigmoid(c)
    o_ref[...] = jnp.dot(a, w_ref[...], preferred_element_type=F32, precision=HIGHEST) + b_ref[...]


def _ada(c, w_ada, b_ada):
    bsz = c.shape[0]
    n_out = w_ada.shape[1]
    tn = 1024
    return pl.pallas_call(
        _ada_kernel,
        grid=(n_out // tn,),
        in_specs=[pl.BlockSpec((bsz, D_MODEL), lambda j: (0, 0)),
                  pl.BlockSpec((D_MODEL, tn), lambda j: (0, j)),
                  pl.BlockSpec((1, tn), lambda j: (0, j))],
        out_specs=pl.BlockSpec((bsz, tn), lambda j: (0, j)),
        out_shape=jax.ShapeDtypeStruct((bsz, n_out), F32),
        compiler_params=pltpu.CompilerParams(dimension_semantics=("arbitrary",),
                                             vmem_limit_bytes=VMEM_LIMIT),
    )(c, w_ada, b_ada.reshape(1, n_out))


def _inproj_kernel(x_ref, sc_ref, sh_ref, w_ref, ws_ref, par_ref, o_ref, g_ref, h_scr):
    @pl.when(pl.program_id(1) == 0)
    def _():
        h = (x_ref[...] * (1.0 + sc_ref[0]) + sh_ref[0]).astype(BF16)
        h_scr[...] = h
        s = jnp.dot(h, ws_ref[...], preferred_element_type=F32)
        lane = lax.broadcasted_iota(jnp.int32, s.shape, 1)
        sb = s + par_ref[0:1, :]
        neg_a = -jnp.exp(par_ref[1:2, :])
        beta = _sigmoid(s)
        g = neg_a * _softplus(sb)
        log_f = -_softplus(-sb)
        g_ref[...] = jnp.where(lane < DN_HEADS, beta,
                               jnp.where(lane < 2 * DN_HEADS, g,
                                         jnp.where(lane < 2 * DN_HEADS + FOX_HEADS, log_f, 0.0)))

    o_ref[...] = jnp.dot(h_scr[...], w_ref[...], preferred_element_type=F32).astype(BF16)


def _inproj(x2, sc1, sh1, w_main, w_small, gate_par, seq):
    n = x2.shape[0]
    tm = min(1024, seq)
    tn = 1024
    per_batch = seq // tm
    return pl.pallas_call(
        _inproj_kernel,
        grid=(n // tm, D_MAIN // tn),
        in_specs=[pl.BlockSpec((tm, D_MODEL), lambda i, j: (i, 0)),
                  pl.BlockSpec((1, 1, D_MODEL), lambda i, j: (i // per_batch, 0, 0)),
                  pl.BlockSpec((1, 1, D_MODEL), lambda i, j: (i // per_batch, 0, 0)),
                  pl.BlockSpec((D_MODEL, tn), lambda i, j: (0, j)),
                  pl.BlockSpec((D_MODEL, LANES), lambda i, j: (0, 0)),
                  pl.BlockSpec((2, LANES), lambda i, j: (0, 0))],
        out_specs=[pl.BlockSpec((tm, tn), lambda i, j: (i, j)),
                   pl.BlockSpec((tm, LANES), lambda i, j: (i, 0))],
        out_shape=[jax.ShapeDtypeStruct((n, D_MAIN), BF16),
                   jax.ShapeDtypeStruct((n, LANES), F32)],
        scratch_shapes=[pltpu.VMEM((tm, D_MODEL), BF16)],
        compiler_params=pltpu.CompilerParams(dimension_semantics=("arbitrary", "arbitrary"),
                                             vmem_limit_bytes=VMEM_LIMIT),
    )(x2, sc1, sh1, w_main, w_small, gate_par)


def _cumsum_kernel(g_ref, gc_ref, fk_ref, carry):
    @pl.when(pl.program_id(1) == 0)
    def _():
        carry[...] = jnp.zeros_like(carry)

    x = g_ref[0]
    t = x.shape[0]
    row = lax.broadcasted_iota(jnp.int32, (t, t), 0)
    col = lax.broadcasted_iota(jnp.int32, (t, t), 1)
    tri = (col <= row).astype(F32)
    shift = int(math.log2(DN_CHUNK))
    tri_chunk = jnp.where((row >> shift) == (col >> shift), tri, 0.0)
    cs = jnp.dot(tri, x, preferred_element_type=F32, precision=HIGHEST)
    f = cs + carry[...]
    gc_ref[0] = jnp.dot(tri_chunk, x, preferred_element_type=F32, precision=HIGHEST)
    carry[...] = carry[...] + cs[t - 1:t, :]

    lane = lax.broadcasted_iota(jnp.int32, (t, LANES), 1)
    for h in range(FOX_HEADS):
        c0 = 2 * DN_HEADS + h
        val = jnp.broadcast_to(f[:, c0:c0 + 1] * (-LOG2E), (t, LANES))
        hi = val.astype(BF16)
        r1 = val - hi.astype(F32)
        mid = r1.astype(BF16)
        lo = r1 - mid.astype(F32)
        fk_ref[0, :, h * LANES:(h + 1) * LANES] = jnp.where(
            lane == 0, hi.astype(F32),
            jnp.where(lane == 1, mid.astype(F32), jnp.where(lane == 2, lo, 0.0))).astype(BF16)


def _cumsum(gates3):
    bsz, seq, _ = gates3.shape
    t = min(512, seq)
    spec = pl.BlockSpec((1, t, LANES), lambda b, i: (b, i, 0))
    return pl.pallas_call(
        _cumsum_kernel,
        grid=(bsz, seq // t),
        in_specs=[spec],
        out_specs=[spec, pl.BlockSpec((1, t, FOX_HEADS * LANES), lambda b, i: (b, i, 0))],
        out_shape=[jax.ShapeDtypeStruct(gates3.shape, F32),
                   jax.ShapeDtypeStruct((bsz, seq, FOX_HEADS * LANES), BF16)],
        scratch_shapes=[pltpu.VMEM((1, LANES), F32)],
        compiler_params=pltpu.CompilerParams(dimension_semantics=("arbitrary", "arbitrary")),
    )(gates3)


def _dn_kernel(qkv_ref, halo_ref, z_ref, gate_ref, gc_ref, gct_ref, convw_ref, normw_ref,
               o_ref, state_ref):
    t = pl.program_id(1)

    @pl.when(t == 0)
    def _():
        state_ref[...] = jnp.zeros_like(state_ref)

    c = DN_CHUNK
    d = HEAD_DIM
    heads = range(DN_HEADS)
    row = lax.broadcasted_iota(jnp.int32, (c, c), 0)
    col = lax.broadcasted_iota(jnp.int32, (c, c), 1)
    incl = col <= row
    strict_f = (col < row).astype(F32)
    incl_f = incl.astype(F32)
    eye = (row == col).astype(F32)
    diag8_f = ((row >> 3) == (col >> 3)).astype(F32)
    merge_levels = range(3, int(math.log2(c)))
    low_f = [jnp.where((row >> (s + 1)) == (col >> (s + 1)),
                       jnp.where((row >> s) == (col >> s), 0.0, 1.0), 0.0) for s in merge_levels]
    halo_on = (t > 0).astype(F32)
    gates = gate_ref[0]
    gcs = gc_ref[0]
    gct = gct_ref[0]
    normw = normw_ref[...]

    def conv_silu(off):
        x = qkv_ref[0, :, off:off + d].astype(F32)
        hx = halo_ref[0, :, off:off + d].astype(F32) * halo_on
        xc = jnp.concatenate([hx, x], axis=0)
        w = convw_ref[:, off:off + d]
        acc = x * w[CONV_WIDTH - 1:CONV_WIDTH]
        for k in range(1, CONV_WIDTH):
            acc = acc + pltpu.roll(xc, k, axis=0)[BF16_SUBLANES:] * w[CONV_WIDTH - 1 - k:CONV_WIDTH - k]
        return acc * _sigmoid(acc)

    def l2n(x):
        return x * lax.rsqrt(jnp.sum(x * x, axis=-1, keepdims=True) + NORM_EPS)

    def bdot(a, b):
        return jnp.dot(a, b, preferred_element_type=F32)

    q = [l2n(conv_silu(h * d)) * (d ** -0.5) for h in heads]
    k = [l2n(conv_silu(DN_WIDTH + h * d)) for h in heads]
    v = [conv_silu(2 * DN_WIDTH + h * d) for h in heads]
    beta = [gates[:, h:h + 1] for h in heads]
    g_col = [gcs[:, DN_HEADS + h:DN_HEADS + h + 1] for h in heads]
    e_g = [jnp.exp(g) for g in g_col]
    g_last = [g[c - 1:c, :] for g in g_col]
    decay = [jnp.exp(jnp.where(incl, g_col[h] - gct[h:h + 1, :], 0.0)) for h in heads]
    kb = [k[h] * beta[h] for h in heads]

    kk = [_mm_nt(jnp.concatenate([kb[h], q[h]], axis=0), k[h]) for h in heads]
    a = [kk[h][:c] * decay[h] * strict_f for h in heads]
    qk = [(kk[h][c:] * decay[h] * incl_f).astype(BF16) for h in heads]
    n1 = [-(a[h] * diag8_f) for h in heads]
    n1b = [n.astype(BF16) for n in n1]
    x = [eye + n for n in n1]
    xb = [xx.astype(BF16) for xx in x]
    n2b = [bdot(n, n).astype(BF16) for n in n1b]
    x = [x[h] + bdot(n2b[h], xb[h]) for h in heads]
    n4b = [bdot(n, n).astype(BF16) for n in n2b]
    xb = [xx.astype(BF16) for xx in x]
    x = [x[h] + bdot(n4b[h], xb[h]) for h in heads]
    for lf in low_f:
        xb = [xx.astype(BF16) for xx in x]
        lx = [bdot((a[h] * lf).astype(BF16), xb[h]).astype(BF16) for h in heads]
        x = [x[h] - bdot(xb[h], lx[h]) for h in heads]

    uw = [_mm(x[h], jnp.concatenate([v[h] * beta[h], kb[h] * e_g[h]], axis=1)) for h in heads]
    s_old = [state_ref[h] for h in heads]
    r = [_mm(jnp.concatenate([uw[h][:, d:], q[h] * e_g[h]], axis=0), s_old[h]) for h in heads]
    v_new = [(uw[h][:, :d] - r[h][:c]).astype(BF16) for h in heads]
    o = [r[h][c:] + bdot(qk[h], v_new[h]) for h in heads]
    for h in heads:
        kd = k[h] * jnp.exp(g_last[h] - g_col[h])
        state_ref[h] = s_old[h] * jnp.exp(g_last[h]) + _mm_tn(kd, v_new[h])
    for h in heads:
        z = z_ref[0, :, h * d:(h + 1) * d].astype(F32)
        rms = lax.rsqrt(jnp.mean(o[h] * o[h], axis=-1, keepdims=True) + NORM_EPS)
        o_ref[0, :, h * d:(h + 1) * d] = (o[h] * rms * normw * (z * _sigmoid(z))).astype(BF16)


def _deltanet(main3, gates3, gc3, gct, conv_w, norm_w):
    bsz, seq, _ = main3.shape
    c = DN_CHUNK
    halo_blocks = c // BF16_SUBLANES
    return pl.pallas_call(
        _dn_kernel,
        grid=(bsz, seq // c),
        in_specs=[pl.BlockSpec((1, c, 3 * DN_WIDTH), lambda b, t: (b, t, 0)),
                  pl.BlockSpec((1, BF16_SUBLANES, 3 * DN_WIDTH),
                               lambda b, t: (b, jnp.maximum(t * halo_blocks - 1, 0), 0)),
                  pl.BlockSpec((1, c, DN_WIDTH), lambda b, t: (b, t, 3)),
                  pl.BlockSpec((1, c, LANES), lambda b, t: (b, t, 0)),
                  pl.BlockSpec((1, c, LANES), lambda b, t: (b, t, 0)),
                  pl.BlockSpec((1, DN_HEADS, c), lambda b, t: (b, 0, t)),
                  pl.BlockSpec((CONV_WIDTH, 3 * DN_WIDTH), lambda b, t: (0, 0)),
                  pl.BlockSpec((1, HEAD_DIM), lambda b, t: (0, 0))],
        out_specs=pl.BlockSpec((1, c, DN_WIDTH), lambda b, t: (b, t, 0)),
        out_shape=jax.ShapeDtypeStruct((bsz, seq, DN_WIDTH), BF16),
        scratch_shapes=[pltpu.VMEM((DN_HEADS, HEAD_DIM, HEAD_DIM), F32)],
        compiler_params=pltpu.CompilerParams(dimension_semantics=("arbitrary", "arbitrary"),
                                             vmem_limit_bytes=VMEM_LIMIT),
    )(main3, main3, main3, gates3, gc3, gct, conv_w, norm_w.reshape(1, HEAD_DIM))


def _fox_kernel(q_ref, k_ref, v_ref, fk_ref, o_ref, qa_scr, m_scr, l_scr, acc_scr):
    i = pl.program_id(1)
    j = pl.program_id(2)
    d = HEAD_DIM
    tq = q_ref.shape[1]
    tk = k_ref.shape[1]

    @pl.when(j == 0)
    def _():
        m_scr[...] = jnp.full_like(m_scr, NEG_BIG)
        l_scr[...] = jnp.zeros_like(l_scr)
        acc_scr[...] = jnp.zeros_like(acc_scr)
        lane = lax.broadcasted_iota(jnp.int32, (tq, LANES), 1)
        ones = jnp.where(lane < 3, 1.0, 0.0).astype(BF16)
        for h in range(FOX_HEADS):
            q = (q_ref[0, :, h * d:(h + 1) * d].astype(F32) * (d ** -0.5 * LOG2E)).astype(BF16)
            qa_scr[h] = jnp.concatenate([q, ones], axis=1)

    def update(diagonal):
        if diagonal:
            key_pos = lax.broadcasted_iota(jnp.int32, (tk, tq), 0)
            q_pos = lax.broadcasted_iota(jnp.int32, (tk, tq), 1)
            visible = key_pos <= q_pos

        def scores(h):
            k_aug = jnp.concatenate([k_ref[0, :, h * d:(h + 1) * d],
                                     fk_ref[0, :, h * LANES:(h + 1) * LANES]], axis=1)
            return lax.dot_general(k_aug, qa_scr[h], (((1,), (1,)), ((), ())),
                                   preferred_element_type=F32)

        s_next = scores(0)
        for h in range(FOX_HEADS):
            s = s_next
            if h + 1 < FOX_HEADS:
                s_next = scores(h + 1)
            v = v_ref[0, :, h * d:(h + 1) * d]
            if diagonal:
                s = jnp.where(visible, s, NEG_BIG)
            m_old = m_scr[h]
            m_new = jnp.maximum(m_old, jnp.max(s, axis=0, keepdims=True))
            alpha = jnp.exp2(m_old - m_new)
            p = jnp.exp2(s - m_new)
            l_scr[h] = alpha * l_scr[h] + jnp.sum(p, axis=0, keepdims=True)
            acc_scr[h] = alpha * acc_scr[h] + _mm_tn(v, p)
            m_scr[h] = m_new

    @pl.when(j < i)
    def _():
        update(False)

    @pl.when(j == i)
    def _():
        update(True)
        for h in range(FOX_HEADS):
            o = acc_scr[h] / l_scr[h]
            o_ref[0, :, h * d:(h + 1) * d] = o.T.astype(BF16)


def _fox(main3, fk_aug):
    bsz, seq, _ = main3.shape
    tq = min(512, seq)
    n_blk = seq // tq
    w = FOX_WIDTH
    base = (4 * DN_WIDTH) // w
    return pl.pallas_call(
        _fox_kernel,
        grid=(bsz, n_blk, n_blk),
        in_specs=[pl.BlockSpec((1, tq, w), lambda b, i, j: (b, i, base)),
                  pl.BlockSpec((1, tq, w), lambda b, i, j: (b, jnp.minimum(i, j), base + 1)),
                  pl.BlockSpec((1, tq, w), lambda b, i, j: (b, jnp.minimum(i, j), base + 2)),
                  pl.BlockSpec((1, tq, FOX_HEADS * LANES), lambda b, i, j: (b, jnp.minimum(i, j), 0))],
        out_specs=pl.BlockSpec((1, tq, w), lambda b, i, j: (b, i, 0)),
        out_shape=jax.ShapeDtypeStruct((bsz, seq, w), BF16),
        scratch_shapes=[pltpu.VMEM((FOX_HEADS, tq, 2 * HEAD_DIM), BF16),
                        pltpu.VMEM((FOX_HEADS, 1, tq), F32),
                        pltpu.VMEM((FOX_HEADS, 1, tq), F32),
                        pltpu.VMEM((FOX_HEADS, HEAD_DIM, tq), F32)],
        compiler_params=pltpu.CompilerParams(
            dimension_semantics=("arbitrary", "arbitrary", "arbitrary"),
            vmem_limit_bytes=VMEM_LIMIT),
    )(main3, main3, main3, fk_aug)


ROUTE_ROWS = 8 + N_EXPERTS
TOKEN_TILE = 8


def _store_token_tiles(ref, packed):
    m = packed.shape[0]
    for s in range(TOKEN_TILE):
        ref[pl.ds(s, m, stride=TOKEN_TILE), :] = packed[:, s * LANES:(s + 1) * LANES]


def _load_token_tiles(ref, start, m):
    return [ref[pl.ds(start + s, m, stride=TOKEN_TILE), :] for s in range(TOKEN_TILE)]


def _outproj_kernel(odn_ref, ofox_ref, w1_ref, w2_ref, x_ref, g1_ref, sc2_ref, sh2_ref,
                    lng_ref, lnb_ref, wr_ref, br_ref, x1_ref, h2p_ref, route_ref, count_ref,
                    upper_scr, carry_scr):
    tm = x_ref.shape[0]

    @pl.when(pl.program_id(0) == 0)
    def _():
        r = lax.broadcasted_iota(jnp.int32, (tm, tm), 0)
        c = lax.broadcasted_iota(jnp.int32, (tm, tm), 1)
        upper_scr[...] = jnp.where(r < c, 1.0, 0.0).astype(BF16)
        carry_scr[...] = jnp.zeros_like(carry_scr)

    y = jnp.dot(odn_ref[...], w1_ref[...], preferred_element_type=F32)
    y = y + jnp.dot(ofox_ref[...], w2_ref[...], preferred_element_type=F32)
    r = ALPHA * x_ref[...] + (1.0 + g1_ref[0]) * y
    x1 = _layer_norm(r, lng_ref[...], lnb_ref[...])
    x1_ref[...] = x1
    h2 = x1 * (1.0 + sc2_ref[0]) + sh2_ref[0]
    _store_token_tiles(h2p_ref, _pack_halves(h2))

    lt = lax.dot_general(wr_ref[...], h2, (((1,), (1,)), ((), ())),
                         preferred_element_type=F32, precision=HIGHEST) + br_ref[...]
    ridx = lax.broadcasted_iota(jnp.int32, (8, tm), 0)
    is_group = ridx < N_GROUPS
    gl = jnp.where(is_group, lt[0:8], NEG_BIG)
    gmax = jnp.max(gl, axis=0, keepdims=True)
    gidx = jnp.min(jnp.where(gl == gmax, ridx, 2 * N_EXPERTS), axis=0, keepdims=True)
    gsum = jnp.sum(jnp.where(is_group, jnp.exp(gl - gmax), 0.0), axis=0, keepdims=True)
    g_w = 1.0 / gsum
    el = jnp.zeros((8, tm), F32)
    for g in range(N_GROUPS):
        el = jnp.where(gidx == g, lt[8 + 8 * g:16 + 8 * g], el)
    emax = jnp.max(el, axis=0, keepdims=True)
    ee = jnp.exp(el - emax)
    prob = ee / jnp.sum(ee, axis=0, keepdims=True)
    p1 = jnp.max(prob, axis=0, keepdims=True)
    e1 = jnp.min(jnp.where(prob == p1, ridx, 2 * N_EXPERTS), axis=0, keepdims=True)
    prob2 = jnp.where(ridx == e1, -1.0, prob)
    p2 = jnp.max(prob2, axis=0, keepdims=True)
    e2 = jnp.min(jnp.where(prob2 == p2, ridx, 2 * N_EXPERTS), axis=0, keepdims=True)
    denom = p1 + p2
    ex1 = gidx * EXPERTS_PER_GROUP + e1
    ex2 = gidx * EXPERTS_PER_GROUP + e2

    eidx = lax.broadcasted_iota(jnp.int32, (N_EXPERTS, tm), 0)
    oh1 = jnp.where(eidx == ex1, 1.0, 0.0)
    oh2 = jnp.where(eidx == ex2, 1.0, 0.0)
    both = oh1 + oh2
    before = jnp.dot(both.astype(BF16), upper_scr[...], preferred_element_type=F32) + carry_scr[...]
    rank1 = jnp.sum(oh1 * before, axis=0, keepdims=True)
    rank2 = jnp.sum(oh2 * before, axis=0, keepdims=True)
    carry = carry_scr[...] + jnp.sum(both, axis=1, keepdims=True)
    carry_scr[...] = carry
    count_ref[...] = jnp.broadcast_to(carry, count_ref.shape)

    rows = [ex1.astype(F32), ex2.astype(F32), g_w * p1 / denom, g_w * p2 / denom, rank1, rank2]
    out = jnp.zeros((8, tm), F32)
    for n, val in enumerate(rows):
        out = jnp.where(ridx == n, val, out)
    route_ref[...] = out


def _outproj(o_dn, o_fox, w1, w2, x2, g1, sc2, sh2, ln_g, ln_b, wr_t, br_t, seq):
    n = x2.shape[0]
    tm = min(512, seq)
    per_batch = seq // tm
    row = lambda i: (i, 0)
    fixed = lambda i: (0, 0)
    mod = pl.BlockSpec((1, 1, D_MODEL), lambda i: (i // per_batch, 0, 0))
    return pl.pallas_call(
        _outproj_kernel,
        grid=(n // tm,),
        in_specs=[pl.BlockSpec((tm, DN_WIDTH), row),
                  pl.BlockSpec((tm, FOX_WIDTH), row),
                  pl.BlockSpec((DN_WIDTH, D_MODEL), fixed),
                  pl.BlockSpec((FOX_WIDTH, D_MODEL), fixed),
                  pl.BlockSpec((tm, D_MODEL), row),
                  mod, mod, mod,
                  pl.BlockSpec((1, D_MODEL), fixed),
                  pl.BlockSpec((1, D_MODEL), fixed),
                  pl.BlockSpec((ROUTE_ROWS, D_MODEL), fixed),
                  pl.BlockSpec((ROUTE_ROWS, 1), fixed)],
        out_specs=[pl.BlockSpec((tm, D_MODEL), row),
                   pl.BlockSpec((tm * TOKEN_TILE, LANES), row),
                   pl.BlockSpec((8, tm), lambda i: (0, i)),
                   pl.BlockSpec((N_EXPERTS, LANES), fixed)],
        out_shape=[jax.ShapeDtypeStruct((n, D_MODEL), F32),
                   jax.ShapeDtypeStruct((n * TOKEN_TILE, LANES), U32),
                   jax.ShapeDtypeStruct((8, n), F32),
                   jax.ShapeDtypeStruct((N_EXPERTS, LANES), F32)],
        scratch_shapes=[pltpu.VMEM((tm, tm), BF16),
                        pltpu.VMEM((N_EXPERTS, 1), F32)],
        compiler_params=pltpu.CompilerParams(dimension_semantics=("arbitrary",),
                                             vmem_limit_bytes=VMEM_LIMIT),
    )(o_dn, o_fox, w1, w2, x2, g1, sc2, sh2, ln_g.reshape(1, D_MODEL), ln_b.reshape(1, D_MODEL),
      wr_t, br_t)


INVERT_STEPS = 32


def _invert_kernel(dest_ref, tok_ref):
    phase = pl.program_id(0)
    i = pl.program_id(1)
    n_tok = dest_ref.shape[0] // TOP_K
    slots_per = tok_ref.shape[0] // INVERT_STEPS
    toks_per = n_tok // INVERT_STEPS

    @pl.when(phase == 0)
    def _():
        def clear(p, carry):
            tok_ref[i * slots_per + p] = 0
            return carry
        lax.fori_loop(0, slots_per, clear, 0, unroll=16)

    @pl.when(phase == 1)
    def _():
        for k in range(TOP_K):
            def put(t, carry):
                tok = i * toks_per + t
                tok_ref[dest_ref[k * n_tok + tok]] = tok * TOKEN_TILE
                return carry
            lax.fori_loop(0, toks_per, put, 0, unroll=8)


def _invert(dest_flat, n_slots):
    assert n_slots % INVERT_STEPS == 0 and (dest_flat.shape[0] // TOP_K) % INVERT_STEPS == 0
    return pl.pallas_call(
        _invert_kernel,
        grid=(2, INVERT_STEPS),
        in_specs=[pl.BlockSpec(memory_space=pltpu.SMEM)],
        out_specs=pl.BlockSpec(memory_space=pltpu.SMEM),
        out_shape=jax.ShapeDtypeStruct((n_slots,), jnp.int32),
        compiler_params=pltpu.CompilerParams(dimension_semantics=("arbitrary", "arbitrary")),
    )(dest_flat)


def _tile_gather_start(idx_ref, base, src_hbm, dst, dst_row0, sem, count):
    for r in range(count):
        row = pl.multiple_of(idx_ref[base + r], TOKEN_TILE)
        pltpu.make_async_copy(src_hbm.at[pl.ds(row, TOKEN_TILE)],
                              dst.at[pl.ds(dst_row0 + r * TOKEN_TILE, TOKEN_TILE)],
                              sem).start(priority=r % 2)


def _tile_gather_wait(src_hbm, dst, dst_row0, sem, count):
    rows = count * TOKEN_TILE
    pltpu.make_async_copy(src_hbm.at[pl.ds(0, rows)], dst.at[pl.ds(dst_row0, rows)], sem).wait()


def _moe_kernel(blk_expert_ref, n_active_ref, tok_ref, h_hbm, wg_ref, wu_ref, wd_ref,
                y_ref, xbuf0, xbuf1, sem):
    b = pl.program_id(0)
    nb = pl.num_programs(0)
    n_active = n_active_ref[0]
    last = jnp.maximum(jnp.minimum(n_active, nb), 1) - 1

    @pl.when(b == 0)
    def _():
        _tile_gather_start(tok_ref, 0, h_hbm, xbuf0, 0, sem.at[0], MOE_BLOCK)

    def block(cur, cur_sem, nxt_buf, nxt_sem):
        _tile_gather_wait(h_hbm, cur, 0, cur_sem, MOE_BLOCK)
        parts = [_unpack_halves(p) for p in _load_token_tiles(cur, 0, MOE_BLOCK)]
        lo = jnp.concatenate([p[0] for p in parts], axis=1).astype(BF16)
        hi = jnp.concatenate([p[1] for p in parts], axis=1).astype(BF16)
        nxt = jnp.minimum(b + 1, last)
        _tile_gather_start(tok_ref, nxt * MOE_BLOCK, h_hbm, nxt_buf, 0, nxt_sem, MOE_BLOCK)
        gate = (jnp.dot(lo, wg_ref[0, :HALF, :], preferred_element_type=F32)
                + jnp.dot(hi, wg_ref[0, HALF:, :], preferred_element_type=F32))
        up = (jnp.dot(lo, wu_ref[0, :HALF, :], preferred_element_type=F32)
              + jnp.dot(hi, wu_ref[0, HALF:, :], preferred_element_type=F32))
        hid = (gate * _sigmoid(gate) * up).astype(BF16)
        y = jnp.dot(hid, wd_ref[0], preferred_element_type=F32)
        _store_token_tiles(y_ref, _pack_halves(y))

        @pl.when(b == last)
        def _():
            _tile_gather_wait(h_hbm, nxt_buf, 0, nxt_sem, MOE_BLOCK)

    @pl.when(jnp.logical_and(b <= last, b % 2 == 0))
    def _():
        block(xbuf0, sem.at[0], xbuf1, sem.at[1])

    @pl.when(jnp.logical_and(b <= last, b % 2 == 1))
    def _():
        block(xbuf1, sem.at[1], xbuf0, sem.at[0])

    @pl.when(b > last)
    def _():
        y_ref[...] = jnp.zeros_like(y_ref)


def _moe(blk_expert, n_active, tok_buf, h2p, wg, wu, wd):
    n_blocks = blk_expert.shape[0]
    buf_rows = MOE_BLOCK * TOKEN_TILE
    grid_spec = pltpu.PrefetchScalarGridSpec(
        num_scalar_prefetch=3,
        grid=(n_blocks,),
        in_specs=[pl.BlockSpec(memory_space=pl.ANY),
                  pl.BlockSpec((1, D_MODEL, D_EXPERT), lambda b, be, na, tk: (be[b], 0, 0)),
                  pl.BlockSpec((1, D_MODEL, D_EXPERT), lambda b, be, na, tk: (be[b], 0, 0)),
                  pl.BlockSpec((1, D_EXPERT, D_MODEL), lambda b, be, na, tk: (be[b], 0, 0))],
        out_specs=pl.BlockSpec((buf_rows, LANES), lambda b, be, na, tk: (b, 0)),
        scratch_shapes=[pltpu.VMEM((buf_rows, LANES), U32),
                        pltpu.VMEM((buf_rows, LANES), U32),
                        pltpu.SemaphoreType.DMA((2,))])
    return pl.pallas_call(
        _moe_kernel,
        grid_spec=grid_spec,
        out_shape=jax.ShapeDtypeStruct((n_blocks * buf_rows, LANES), U32),
        compiler_params=pltpu.CompilerParams(dimension_semantics=("arbitrary",),
                                             vmem_limit_bytes=VMEM_LIMIT),
    )(blk_expert, n_active, tok_buf, h2p, wg, wu, wd)


def _combine_kernel(dest_ref, y_hbm, x1_ref, wt_ref, g2_ref, lng_ref, lnb_ref, o_ref, ybuf0, ybuf1, sem):
    i = pl.program_id(0)
    nt = pl.num_programs(0)
    tm = x1_ref.shape[0]
    count = TOP_K * tm

    @pl.when(i == 0)
    def _():
        _tile_gather_start(dest_ref, 0, y_hbm, ybuf0, 0, sem.at[0], count)

    def tile(cur, cur_sem, nxt_buf, nxt_sem):
        _tile_gather_wait(y_hbm, cur, 0, cur_sem, count)
        first = [_unpack_halves(p) for p in _load_token_tiles(cur, 0, tm)]
        second = [_unpack_halves(p) for p in _load_token_tiles(cur, tm * TOKEN_TILE, tm)]
        nxt = jnp.minimum(i + 1, nt - 1)
        _tile_gather_start(dest_ref, nxt * count, y_hbm, nxt_buf, 0, nxt_sem, count)
        w1 = wt_ref[:, 0:1]
        w2 = wt_ref[:, 1:2]
        lo = [w1 * a[0] + w2 * b[0] for a, b in zip(first, second)]
        hi = [w1 * a[1] + w2 * b[1] for a, b in zip(first, second)]
        y = jnp.concatenate(lo + hi, axis=1)
        r = ALPHA * x1_ref[...] + (1.0 + g2_ref[0]) * y
        o_ref[...] = _layer_norm(r, lng_ref[...], lnb_ref[...])

        @pl.when(i == nt - 1)
        def _():
            _tile_gather_wait(y_hbm, nxt_buf, 0, nxt_sem, count)

    @pl.when(i % 2 == 0)
    def _():
        tile(ybuf0, sem.at[0], ybuf1, sem.at[1])

    @pl.when(i % 2 == 1)
    def _():
        tile(ybuf1, sem.at[1], ybuf0, sem.at[0])


def _combine(dest_tiled, y_buf, x1, wt, g2, ln_g, ln_b, seq, tm):
    n = x1.shape[0]
    per_batch = seq // tm
    grid_spec = pltpu.PrefetchScalarGridSpec(
        num_scalar_prefetch=1,
        grid=(n // tm,),
        in_specs=[pl.BlockSpec(memory_space=pl.ANY),
                  pl.BlockSpec((tm, D_MODEL), lambda i, d: (i, 0)),
                  pl.BlockSpec((tm, LANES), lambda i, d: (i, 0)),
                  pl.BlockSpec((1, 1, D_MODEL), lambda i, d: (i // per_batch, 0, 0)),
                  pl.BlockSpec((1, D_MODEL), lambda i, d: (0, 0)),
                  pl.BlockSpec((1, D_MODEL), lambda i, d: (0, 0))],
        out_specs=pl.BlockSpec((tm, D_MODEL), lambda i, d: (i, 0)),
        scratch_shapes=[pltpu.VMEM((TOP_K * tm * TOKEN_TILE, LANES), U32),
                        pltpu.VMEM((TOP_K * tm * TOKEN_TILE, LANES), U32),
                        pltpu.SemaphoreType.DMA((2,))])
    return pl.pallas_call(
        _combine_kernel,
        grid_spec=grid_spec,
        out_shape=jax.ShapeDtypeStruct((n, D_MODEL), F32),
        compiler_params=pltpu.CompilerParams(dimension_semantics=("arbitrary",),
                                             vmem_limit_bytes=VMEM_LIMIT),
    )(dest_tiled, y_buf, x1, wt, g2, ln_g.reshape(1, D_MODEL), ln_b.reshape(1, D_MODEL))


def _dispatch_plan(route, counts, n):
    m = n * TOP_K
    n_blocks = -(-m // MOE_BLOCK) + N_EXPERTS
    expert = route[0:2].astype(jnp.int32)
    rank = route[4:6].astype(jnp.int32)
    sizes = counts[:, 0].astype(jnp.int32)
    padded = (sizes + MOE_BLOCK - 1) // MOE_BLOCK * MOE_BLOCK
    p_end = jnp.cumsum(padded)
    p_start = p_end - padded
    ids = jnp.arange(N_EXPERTS, dtype=jnp.int32)
    start_of = jnp.sum(jnp.where(expert[..., None] == ids, p_start, 0), axis=-1)
    dest = start_of + rank
    blk_expert = jnp.minimum(
        jnp.searchsorted(p_end, jnp.arange(n_blocks, dtype=jnp.int32) * MOE_BLOCK, side='right'),
        N_EXPERTS - 1).astype(jnp.int32)
    n_active = (p_end[-1] // MOE_BLOCK).astype(jnp.int32).reshape(1)
    return dest, blk_expert, n_active, n_blocks


def kernel(x, c, w_ada, b_ada, w_in, dn_conv_w, dn_a_log, dn_dt_bias, dn_norm_w, fox_f_bias, w_out,
           ln1_g, ln1_b, w_router_group, b_router_group, w_router_expert, b_router_expert,
           w_gate, w_up, w_down, ln2_g, ln2_b):
    bsz, seq, _ = x.shape
    n = bsz * seq
    x2 = x.reshape(n, D_MODEL)

    mod = _ada(c, w_ada[0], b_ada[0])
    sh1, sc1, g1, sh2, sc2, g2 = [m.reshape(bsz, 1, D_MODEL) for m in jnp.split(mod, 6, axis=-1)]

    wi = w_in[0]
    o_b = 4 * DN_WIDTH
    o_a = o_b + DN_HEADS
    o_fq = o_a + DN_HEADS
    o_f = o_fq + 3 * FOX_WIDTH
    w_main = jnp.concatenate([wi[:, :o_b], wi[:, o_fq:o_f]], axis=1).astype(BF16)
    w_small = jnp.concatenate([wi[:, o_b:o_fq], wi[:, o_f:],
                               jnp.zeros((D_MODEL, LANES - 2 * DN_HEADS - FOX_HEADS), F32)],
                              axis=1).astype(BF16)
    zeros8 = jnp.zeros((DN_HEADS,), F32)
    pad = jnp.zeros((LANES - 2 * DN_HEADS - FOX_HEADS,), F32)
    gate_par = jnp.stack([jnp.concatenate([zeros8, dn_dt_bias[0], fox_f_bias[0], pad]),
                          jnp.concatenate([zeros8, dn_a_log[0], zeros8, pad])])

    main, gates = _inproj(x2, sc1, sh1, w_main, w_small, gate_par, seq)
    main3 = main.reshape(bsz, seq, D_MAIN)
    gates3 = gates.reshape(bsz, seq, LANES)
    gc3, fk_aug = _cumsum(gates3)
    gct = jnp.swapaxes(gc3[:, :, DN_HEADS:2 * DN_HEADS], 1, 2)

    o_dn = _deltanet(main3, gates3, gc3, gct, dn_conv_w[0], dn_norm_w[0])
    o_fox = _fox(main3, fk_aug)

    wo = w_out[0].astype(BF16)
    wr_t = jnp.concatenate([w_router_group[0].T, jnp.zeros((8 - N_GROUPS, D_MODEL), F32),
                            jnp.transpose(w_router_expert[0], (0, 2, 1)).reshape(N_EXPERTS, D_MODEL)],
                           axis=0)
    br_t = jnp.concatenate([b_router_group[0], jnp.zeros((8 - N_GROUPS,), F32),
                            b_router_expert[0].reshape(-1)]).reshape(ROUTE_ROWS, 1)
    x1, h2p, route, counts = _outproj(o_dn.reshape(n, DN_WIDTH), o_fox.reshape(n, FOX_WIDTH),
                                      wo[:DN_WIDTH], wo[DN_WIDTH:], x2, g1, sc2, sh2, ln1_g[0], ln1_b[0],
                                      wr_t, br_t, seq)

    dest, blk_expert, n_active, n_blocks = _dispatch_plan(route, counts, n)
    tok_buf = _invert(dest.reshape(-1), n_blocks * MOE_BLOCK)
    y_buf = _moe(blk_expert, n_active, tok_buf, h2p,
                 w_gate[0].astype(BF16), w_up[0].astype(BF16), w_down[0].astype(BF16))

    tm = min(256, seq)
    dest_tiled = (dest * TOKEN_TILE).reshape(TOP_K, n // tm, tm).transpose(1, 0, 2).reshape(-1)
    wt = jnp.pad(route[2:4].T, ((0, 0), (0, LANES - TOP_K)))
    out = _combine(dest_tiled, y_buf, x1, wt, g2, ln2_g[0], ln2_b[0], seq, tm)
    return out.reshape(bsz, seq, D_MODEL)
```

```python
import functools
import math

import jax
import jax.numpy as jnp
from jax import lax
from jax.experimental import pallas as pl
from jax.experimental.pallas import tpu as pltpu

F32 = jnp.float32
BF16 = jnp.bfloat16
U32 = jnp.uint32
HIGHEST = lax.Precision.HIGHEST

D_MODEL = 2048
HEAD_DIM = 128
DN_HEADS = 8
FOX_HEADS = 8
DN_WIDTH = DN_HEADS * HEAD_DIM
FOX_WIDTH = FOX_HEADS * HEAD_DIM
CONV_WIDTH = 4
N_GROUPS = 4
EXPERTS_PER_GROUP = 8
N_EXPERTS = N_GROUPS * EXPERTS_PER_GROUP
TOP_K = 2
D_EXPERT = 512
MOE_BLOCK = 256
LN_EPS = 1e-5
NORM_EPS = 1e-6
ALPHA = 2.0 ** 0.25

D_MAIN = 3 * DN_WIDTH + DN_WIDTH + 3 * FOX_WIDTH
LANES = 128
BF16_SUBLANES = 16
DN_CHUNK = 128
NEG_BIG = -1e30
LOG2E = 1.4426950408889634
HALF = D_MODEL // 2
VMEM_LIMIT = 56 * 1024 * 1024


def _sigmoid(x):
    return 1.0 / (1.0 + jnp.exp(-x))


def _softplus(x):
    return jnp.maximum(x, 0.0) + jnp.log(1.0 + jnp.exp(-jnp.abs(x)))


def _mm(a, b):
    return jnp.dot(a.astype(BF16), b.astype(BF16), preferred_element_type=F32)


def _mm_nt(a, b):
    return lax.dot_general(a.astype(BF16), b.astype(BF16), (((1,), (1,)), ((), ())),
                           preferred_element_type=F32)


def _mm_tn(a, b):
    return lax.dot_general(a.astype(BF16), b.astype(BF16), (((0,), (0,)), ((), ())),
                           preferred_element_type=F32)


def _pack_halves(x):
    c = x.shape[1] // 2
    lo = pltpu.bitcast(x[:, :c].astype(BF16).astype(F32), U32)
    hi = pltpu.bitcast(x[:, c:].astype(BF16).astype(F32), U32)
    return (lo >> 16) | (hi & jnp.uint32(0xFFFF0000))


def _unpack_halves(p):
    lo = pltpu.bitcast(p << 16, F32)
    hi = pltpu.bitcast(p & jnp.uint32(0xFFFF0000), F32)
    return lo, hi


def _layer_norm(r, g, b):
    mu = jnp.mean(r, axis=-1, keepdims=True)
    d = r - mu
    var = jnp.mean(d * d, axis=-1, keepdims=True)
    return d * lax.rsqrt(var + LN_EPS) * g + b


def _ada_kernel(c_ref, w_ref, b_ref, o_ref):
    c = c_ref[...]
    a = c * _sigmoid(c)
    o_ref[...] = jnp.dot(a, w_ref[...], preferred_element_type=F32, precision=HIGHEST) + b_ref[...]


def _ada(c, w_ada, b_ada):
    bsz = c.shape[0]
    n_out = w_ada.shape[1]
    tn = 1024
    return pl.pallas_call(
        _ada_kernel,
        grid=(n_out // tn,),
        in_specs=[pl.BlockSpec((bsz, D_MODEL), lambda j: (0, 0)),
                  pl.BlockSpec((D_MODEL, tn), lambda j: (0, j)),
                  pl.BlockSpec((1, tn), lambda j: (0, j))],
        out_specs=pl.BlockSpec((bsz, tn), lambda j: (0, j)),
        out_shape=jax.ShapeDtypeStruct((bsz, n_out), F32),
        compiler_params=pltpu.CompilerParams(dimension_semantics=("arbitrary",),
                                             vmem_limit_bytes=VMEM_LIMIT),
    )(c, w_ada, b_ada.reshape(1, n_out))


def _inproj_kernel(x_ref, sc_ref, sh_ref, w_ref, ws_ref, par_ref, o_ref, g_ref, h_scr):
    @pl.when(pl.program_id(1) == 0)
    def _():
        h = (x_ref[...] * (1.0 + sc_ref[0]) + sh_ref[0]).astype(BF16)
        h_scr[...] = h
        s = jnp.dot(h, ws_ref[...], preferred_element_type=F32)
        lane = lax.broadcasted_iota(jnp.int32, s.shape, 1)
        sb = s + par_ref[0:1, :]
        neg_a = -jnp.exp(par_ref[1:2, :])
        beta = _sigmoid(s)
        g = neg_a * _softplus(sb)
        log_f = -_softplus(-sb)
        g_ref[...] = jnp.where(lane < DN_HEADS, beta,
                               jnp.where(lane < 2 * DN_HEADS, g,
                                         jnp.where(lane < 2 * DN_HEADS + FOX_HEADS, log_f, 0.0)))

    o_ref[...] = jnp.dot(h_scr[...], w_ref[...], preferred_element_type=F32).astype(BF16)


def _inproj(x2, sc1, sh1, w_main, w_small, gate_par, seq):
    n = x2.shape[0]
    tm = min(1024, seq)
    tn = 1024
    per_batch = seq // tm
    return pl.pallas_call(
        _inproj_kernel,
        grid=(n // tm, D_MAIN // tn),
        in_specs=[pl.BlockSpec((tm, D_MODEL), lambda i, j: (i, 0)),
                  pl.BlockSpec((1, 1, D_MODEL), lambda i, j: (i // per_batch, 0, 0)),
                  pl.BlockSpec((1, 1, D_MODEL), lambda i, j: (i // per_batch, 0, 0)),
                  pl.BlockSpec((D_MODEL, tn), lambda i, j: (0, j)),
                  pl.BlockSpec((D_MODEL, LANES), lambda i, j: (0, 0)),
                  pl.BlockSpec((2, LANES), lambda i, j: (0, 0))],
        out_specs=[pl.BlockSpec((tm, tn), lambda i, j: (i, j)),
                   pl.BlockSpec((tm, LANES), lambda i, j: (i, 0))],
        out_shape=[jax.ShapeDtypeStruct((n, D_MAIN), BF16),
                   jax.ShapeDtypeStruct((n, LANES), F32)],
        scratch_shapes=[pltpu.VMEM((tm, D_MODEL), BF16)],
        compiler_params=pltpu.CompilerParams(dimension_semantics=("arbitrary", "arbitrary"),
                                             vmem_limit_bytes=VMEM_LIMIT),
    )(x2, sc1, sh1, w_main, w_small, gate_par)


def _cumsum_kernel(g_ref, gc_ref, fk_ref, carry):
    @pl.when(pl.program_id(1) == 0)
    def _():
        carry[...] = jnp.zeros_like(carry)

    x = g_ref[0]
    t = x.shape[0]
    row = lax.broadcasted_iota(jnp.int32, (t, t), 0)
    col = lax.broadcasted_iota(jnp.int32, (t, t), 1)
    tri = (col <= row).astype(F32)
    shift = int(math.log2(DN_CHUNK))
    tri_chunk = jnp.where((row >> shift) == (col >> shift), tri, 0.0)
    cs = jnp.dot(tri, x, preferred_element_type=F32, precision=HIGHEST)
    f = cs + carry[...]
    gc_ref[0] = jnp.dot(tri_chunk, x, preferred_element_type=F32, precision=HIGHEST)
    carry[...] = carry[...] + cs[t - 1:t, :]

    lane = lax.broadcasted_iota(jnp.int32, (t, LANES), 1)
    for h in range(FOX_HEADS):
        c0 = 2 * DN_HEADS + h
        val = jnp.broadcast_to(f[:, c0:c0 + 1] * (-LOG2E), (t, LANES))
        hi = val.astype(BF16)
        r1 = val - hi.astype(F32)
        mid = r1.astype(BF16)
        lo = r1 - mid.astype(F32)
        fk_ref[0, :, h * LANES:(h + 1) * LANES] = jnp.where(
            lane == 0, hi.astype(F32),
            jnp.where(lane == 1, mid.astype(F32), jnp.where(lane == 2, lo, 0.0))).astype(BF16)


def _cumsum(gates3):
    bsz, seq, _ = gates3.shape
    t = min(512, seq)
    spec = pl.BlockSpec((1, t, LANES), lambda b, i: (b, i, 0))
    return pl.pallas_call(
        _cumsum_kernel,
        grid=(bsz, seq // t),
        in_specs=[spec],
        out_specs=[spec, pl.BlockSpec((1, t, FOX_HEADS * LANES), lambda b, i: (b, i, 0))],
        out_shape=[jax.ShapeDtypeStruct(gates3.shape, F32),
                   jax.ShapeDtypeStruct((bsz, seq, FOX_HEADS * LANES), BF16)],
        scratch_shapes=[pltpu.VMEM((1, LANES), F32)],
        compiler_params=pltpu.CompilerParams(dimension_semantics=("arbitrary", "arbitrary")),
    )(gates3)


def _dn_kernel(qkv_ref, halo_ref, z_ref, gate_ref, gc_ref, gct_ref, convw_ref, normw_ref,
               o_ref, state_ref):
    t = pl.program_id(1)

    @pl.when(t == 0)
    def _():
        state_ref[...] = jnp.zeros_like(state_ref)

    c = DN_CHUNK
    d = HEAD_DIM
    heads = range(DN_HEADS)
    row = lax.broadcasted_iota(jnp.int32, (c, c), 0)
    col = lax.broadcasted_iota(jnp.int32, (c, c), 1)
    incl = col <= row
    strict_f = (col < row).astype(F32)
    incl_f = incl.astype(F32)
    eye = (row == col).astype(F32)
    diag8_f = ((row >> 3) == (col >> 3)).astype(F32)
    merge_levels = range(3, int(math.log2(c)))
    low_f = [jnp.where((row >> (s + 1)) == (col >> (s + 1)),
                       jnp.where((row >> s) == (col >> s), 0.0, 1.0), 0.0) for s in merge_levels]
    halo_on = (t > 0).astype(F32)
    gates = gate_ref[0]
    gcs = gc_ref[0]
    gct = gct_ref[0]
    normw = normw_ref[...]

    def conv_silu(off):
        x = qkv_ref[0, :, off:off + d].astype(F32)
        hx = halo_ref[0, :, off:off + d].astype(F32) * halo_on
        xc = jnp.concatenate([hx, x], axis=0)
        w = convw_ref[:, off:off + d]
        acc = x * w[CONV_WIDTH - 1:CONV_WIDTH]
        for k in range(1, CONV_WIDTH):
            acc = acc + pltpu.roll(xc, k, axis=0)[BF16_SUBLANES:] * w[CONV_WIDTH - 1 - k:CONV_WIDTH - k]
        return acc * _sigmoid(acc)

    def l2n(x):
        return x * lax.rsqrt(jnp.sum(x * x, axis=-1, keepdims=True) + NORM_EPS)

    def bdot(a, b):
        return jnp.dot(a, b, preferred_element_type=F32)

    q = [l2n(conv_silu(h * d)) * (d ** -0.5) for h in heads]
    k = [l2n(conv_silu(DN_WIDTH + h * d)) for h in heads]
    v = [conv_silu(2 * DN_WIDTH + h * d) for h in heads]
    beta = [gates[:, h:h + 1] for h in heads]
    g_col = [gcs[:, DN_HEADS + h:DN_HEADS + h + 1] for h in heads]
    e_g = [jnp.exp(g) for g in g_col]
    g_last = [g[c - 1:c, :] for g in g_col]
    decay = [jnp.exp(jnp.where(incl, g_col[h] - gct[h:h + 1, :], 0.0)) for h in heads]
    kb = [k[h] * beta[h] for h in heads]

    kk = [_mm_nt(jnp.concatenate([kb[h], q[h]], axis=0), k[h]) for h in heads]
    a = [kk[h][:c] * decay[h] * strict_f for h in heads]
    qk = [(kk[h][c:] * decay[h] * incl_f).astype(BF16) for h in heads]
    n1 = [-(a[h] * diag8_f) for h in heads]
    n1b = [n.astype(BF16) for n in n1]
    x = [eye + n for n in n1]
    xb = [xx.astype(BF16) for xx in x]
    n2b = [bdot(n, n).astype(BF16) for n in n1b]
    x = [x[h] + bdot(n2b[h], xb[h]) for h in heads]
    n4b = [bdot(n, n).astype(BF16) for n in n2b]
    xb = [xx.astype(BF16) for xx in x]
    x = [x[h] + bdot(n4b[h], xb[h]) for h in heads]
    for lf in low_f:
        xb = [xx.astype(BF16) for xx in x]
        lx = [bdot((a[h] * lf).astype(BF16), xb[h]).astype(BF16) for h in heads]
        x = [x[h] - bdot(xb[h], lx[h]) for h in heads]

    uw = [_mm(x[h], jnp.concatenate([v[h] * beta[h], kb[h] * e_g[h]], axis=1)) for h in heads]
    s_old = [state_ref[h] for h in heads]
    r = [_mm(jnp.concatenate([uw[h][:, d:], q[h] * e_g[h]], axis=0), s_old[h]) for h in heads]
    v_new = [(uw[h][:, :d] - r[h][:c]).astype(BF16) for h in heads]
    o = [r[h][c:] + bdot(qk[h], v_new[h]) for h in heads]
    for h in heads:
        kd = k[h] * jnp.exp(g_last[h] - g_col[h])
        state_ref[h] = s_old[h] * jnp.exp(g_last[h]) + _mm_tn(kd, v_new[h])
    for h in heads:
        z = z_ref[0, :, h * d:(h + 1) * d].astype(F32)
        rms = lax.rsqrt(jnp.mean(o[h] * o[h], axis=-1, keepdims=True) + NORM_EPS)
        o_ref[0, :, h * d:(h + 1) * d] = (o[h] * rms * normw * (z * _sigmoid(z))).astype(BF16)


def _deltanet(main3, gates3, gc3, gct, conv_w, norm_w):
    bsz, seq, _ = main3.shape
    c = DN_CHUNK
    halo_blocks = c // BF16_SUBLANES
    return pl.pallas_call(
        _dn_kernel,
        grid=(bsz, seq // c),
        in_specs=[pl.BlockSpec((1, c, 3 * DN_WIDTH), lambda b, t: (b, t, 0)),
                  pl.BlockSpec((1, BF16_SUBLANES, 3 * DN_WIDTH),
                               lambda b, t: (b, jnp.maximum(t * halo_blocks - 1, 0), 0)),
                  pl.BlockSpec((1, c, DN_WIDTH), lambda b, t: (b, t, 3)),
                  pl.BlockSpec((1, c, LANES), lambda b, t: (b, t, 0)),
                  pl.BlockSpec((1, c, LANES), lambda b, t: (b, t, 0)),
                  pl.BlockSpec((1, DN_HEADS, c), lambda b, t: (b, 0, t)),
                  pl.BlockSpec((CONV_WIDTH, 3 * DN_WIDTH), lambda b, t: (0, 0)),
                  pl.BlockSpec((1, HEAD_DIM), lambda b, t: (0, 0))],
        out_specs=pl.BlockSpec((1, c, DN_WIDTH), lambda b, t: (b, t, 0)),
        out_shape=jax.ShapeDtypeStruct((bsz, seq, DN_WIDTH), BF16),
        scratch_shapes=[pltpu.VMEM((DN_HEADS, HEAD_DIM, HEAD_DIM), F32)],
        compiler_params=pltpu.CompilerParams(dimension_semantics=("arbitrary", "arbitrary"),
                                             vmem_limit_bytes=VMEM_LIMIT),
    )(main3, main3, main3, gates3, gc3, gct, conv_w, norm_w.reshape(1, HEAD_DIM))


def _fox_kernel(q_ref, k_ref, v_ref, fk_ref, o_ref, qa_scr, m_scr, l_scr, acc_scr):
    i = pl.program_id(1)
    j = pl.program_id(2)
    d = HEAD_DIM
    tq = q_ref.shape[1]
    tk = k_ref.shape[1]

    @pl.when(j == 0)
    def _():
        m_scr[...] = jnp.full_like(m_scr, NEG_BIG)
        l_scr[...] = jnp.zeros_like(l_scr)
        acc_scr[...] = jnp.zeros_like(acc_scr)
        lane = lax.broadcasted_iota(jnp.int32, (tq, LANES), 1)
        ones = jnp.where(lane < 3, 1.0, 0.0).astype(BF16)
        for h in range(FOX_HEADS):
            q = (q_ref[0, :, h * d:(h + 1) * d].astype(F32) * (d ** -0.5 * LOG2E)).astype(BF16)
            qa_scr[h] = jnp.concatenate([q, ones], axis=1)

    def update(diagonal):
        if diagonal:
            key_pos = lax.broadcasted_iota(jnp.int32, (tk, tq), 0)
            q_pos = lax.broadcasted_iota(jnp.int32, (tk, tq), 1)
            visible = key_pos <= q_pos

        def scores(h):
            k_aug = jnp.concatenate([k_ref[0, :, h * d:(h + 1) * d],
                                     fk_ref[0, :, h * LANES:(h + 1) * LANES]], axis=1)
            return lax.dot_general(k_aug, qa_scr[h], (((1,), (1,)), ((), ())),
                                   preferred_element_type=F32)

        s_next = scores(0)
        for h in range(FOX_HEADS):
            s = s_next
            if h + 1 < FOX_HEADS:
                s_next = scores(h + 1)
            v = v_ref[0, :, h * d:(h + 1) * d]
            if diagonal:
                s = jnp.where(visible, s, NEG_BIG)
            m_old = m_scr[h]
            m_new = jnp.maximum(m_old, jnp.max(s, axis=0, keepdims=True))
            alpha = jnp.exp2(m_old - m_new)
            p = jnp.exp2(s - m_new)
            l_scr[h] = alpha * l_scr[h] + jnp.sum(p, axis=0, keepdims=True)
            acc_scr[h] = alpha * acc_scr[h] + _mm_tn(v, p)
            m_scr[h] = m_new

    @pl.when(j < i)
    def _():
        update(False)

    @pl.when(j == i)
    def _():
        update(True)
        for h in range(FOX_HEADS):
            o = acc_scr[h] / l_scr[h]
            o_ref[0, :, h * d:(h + 1) * d] = o.T.astype(BF16)


def _fox(main3, fk_aug):
    bsz, seq, _ = main3.shape
    tq = min(512, seq)
    n_blk = seq // tq
    w = FOX_WIDTH
    base = (4 * DN_WIDTH) // w
    return pl.pallas_call(
        _fox_kernel,
        grid=(bsz, n_blk, n_blk),
        in_specs=[pl.BlockSpec((1, tq, w), lambda b, i, j: (b, i, base)),
                  pl.BlockSpec((1, tq, w), lambda b, i, j: (b, jnp.minimum(i, j), base + 1)),
                  pl.BlockSpec((1, tq, w), lambda b, i, j: (b, jnp.minimum(i, j), base + 2)),
                  pl.BlockSpec((1, tq, FOX_HEADS * LANES), lambda b, i, j: (b, jnp.minimum(i, j), 0))],
        out_specs=pl.BlockSpec((1, tq, w), lambda b, i, j: (b, i, 0)),
        out_shape=jax.ShapeDtypeStruct((bsz, seq, w), BF16),
        scratch_shapes=[pltpu.VMEM((FOX_HEADS, tq, 2 * HEAD_DIM), BF16),
                        pltpu.VMEM((FOX_HEADS, 1, tq), F32),
                        pltpu.VMEM((FOX_HEADS, 1, tq), F32),
                        pltpu.VMEM((FOX_HEADS, HEAD_DIM, tq), F32)],
        compiler_params=pltpu.CompilerParams(
            dimension_semantics=("arbitrary", "arbitrary", "arbitrary"),
            vmem_limit_bytes=VMEM_LIMIT),
    )(main3, main3, main3, fk_aug)


ROUTE_ROWS = 8 + N_EXPERTS
TOKEN_TILE = 8


OUTPROJ_SUBTILES = 1


def _store_token_tiles(ref, packed, first_token=0):
    m = packed.shape[0]
    for s in range(TOKEN_TILE):
        ref[pl.ds(first_token * TOKEN_TILE + s, m, stride=TOKEN_TILE), :] = packed[:, s * LANES:(s + 1) * LANES]


def _load_token_tiles(ref, start, m):
    return [ref[pl.ds(start + s, m, stride=TOKEN_TILE), :] for s in range(TOKEN_TILE)]


def _outproj_kernel(odn_ref, ofox_ref, w1_ref, w2_ref, x_ref, g1_ref, sc2_ref, sh2_ref,
                    lng_ref, lnb_ref, wr_ref, br_ref, x1_ref, h2p_ref, route_ref, count_ref,
                    upper_scr, carry_scr):
    tm = x_ref.shape[0]

    @pl.when(pl.program_id(0) == 0)
    def _():
        r = lax.broadcasted_iota(jnp.int32, (tm, tm), 0)
        c = lax.broadcasted_iota(jnp.int32, (tm, tm), 1)
        upper_scr[...] = jnp.where(r < c, 1.0, 0.0).astype(BF16)
        carry_scr[...] = jnp.zeros_like(carry_scr)

    sub = tm // OUTPROJ_SUBTILES
    ys = []
    for u in range(OUTPROJ_SUBTILES):
        rows = slice(u * sub, (u + 1) * sub)
        ys.append(jnp.dot(odn_ref[rows, :], w1_ref[...], preferred_element_type=F32)
                  + jnp.dot(ofox_ref[rows, :], w2_ref[...], preferred_element_type=F32))
    for u in range(OUTPROJ_SUBTILES):
        _outproj_tail(u, sub, ys[u], x_ref, g1_ref, sc2_ref, sh2_ref, lng_ref, lnb_ref, wr_ref, br_ref,
                      x1_ref, h2p_ref, route_ref, upper_scr, carry_scr)
    count_ref[...] = jnp.broadcast_to(carry_scr[...], count_ref.shape)


def _outproj_tail(u, tm, y, x_ref, g1_ref, sc2_ref, sh2_ref, lng_ref, lnb_ref, wr_ref, br_ref,
                  x1_ref, h2p_ref, route_ref, upper_scr, carry_scr):
    rows = slice(u * tm, (u + 1) * tm)
    r = ALPHA * x_ref[rows, :] + (1.0 + g1_ref[0]) * y
    x1 = _layer_norm(r, lng_ref[...], lnb_ref[...])
    x1_ref[rows, :] = x1
    h2 = x1 * (1.0 + sc2_ref[0]) + sh2_ref[0]
    _store_token_tiles(h2p_ref, _pack_halves(h2), u * tm)

    h_hi = h2.astype(BF16)
    h_lo = (h2 - h_hi.astype(F32)).astype(BF16)
    parts = jnp.dot(jnp.concatenate([h_hi, h_lo], axis=0), wr_ref[...], preferred_element_type=F32)
    terms = parts[:tm] + parts[tm:]
    lt = (terms[:, :LANES] + terms[:, LANES:]).T + br_ref[...]
    row = [lt[n:n + 1, :] for n in range(ROUTE_ROWS)]

    def first_max(vals):
        best = vals[0]
        for val in vals[1:]:
            best = jnp.maximum(best, val)
        idx = jnp.full(best.shape, len(vals) - 1, jnp.int32)
        for n in range(len(vals) - 2, -1, -1):
            idx = jnp.where(vals[n] == best, n, idx)
        return best, idx

    gmax, gidx = first_max(row[:N_GROUPS])
    gsum = jnp.exp(row[0] - gmax)
    for n in range(1, N_GROUPS):
        gsum = gsum + jnp.exp(row[n] - gmax)
    g_w = 1.0 / gsum
    el = []
    for e in range(EXPERTS_PER_GROUP):
        val = row[8 + e]
        for g in range(1, N_GROUPS):
            val = jnp.where(gidx == g, row[8 + EXPERTS_PER_GROUP * g + e], val)
        el.append(val)
    emax, _ = first_max(el)
    ee = [jnp.exp(val - emax) for val in el]
    esum = ee[0]
    for val in ee[1:]:
        esum = esum + val
    prob = [val / esum for val in ee]
    p1, e1 = first_max(prob)
    p2, e2 = first_max([jnp.where(e1 == e, -1.0, prob[e]) for e in range(EXPERTS_PER_GROUP)])
    denom = p1 + p2
    ex1 = gidx * EXPERTS_PER_GROUP + e1
    ex2 = gidx * EXPERTS_PER_GROUP + e2
    ridx = lax.broadcasted_iota(jnp.int32, (8, tm), 0)

    eidx = lax.broadcasted_iota(jnp.int32, (N_EXPERTS, tm), 0)
    oh1 = jnp.where(eidx == ex1, 1.0, 0.0)
    oh2 = jnp.where(eidx == ex2, 1.0, 0.0)
    both = oh1 + oh2
    before = jnp.dot(both.astype(BF16), upper_scr[:tm, :tm], preferred_element_type=F32) + carry_scr[...]
    rank1 = jnp.sum(oh1 * before, axis=0, keepdims=True)
    rank2 = jnp.sum(oh2 * before, axis=0, keepdims=True)
    carry_scr[...] = carry_scr[...] + jnp.sum(both, axis=1, keepdims=True)

    fields = [ex1.astype(F32), ex2.astype(F32), g_w * p1 / denom, g_w * p2 / denom, rank1, rank2]
    out = jnp.zeros((8, tm), F32)
    for n, val in enumerate(fields):
        out = jnp.where(ridx == n, val, out)
    route_ref[:, rows] = out


def _outproj(o_dn, o_fox, w1, w2, x2, g1, sc2, sh2, ln_g, ln_b, wr_t, br_t, seq):
    n = x2.shape[0]
    tm = min(512, seq)
    per_batch = seq // tm
    row = lambda i: (i, 0)
    fixed = lambda i: (0, 0)
    mod = pl.BlockSpec((1, 1, D_MODEL), lambda i: (i // per_batch, 0, 0))
    return pl.pallas_call(
        _outproj_kernel,
        grid=(n // tm,),
        in_specs=[pl.BlockSpec((tm, DN_WIDTH), row),
                  pl.BlockSpec((tm, FOX_WIDTH), row),
                  pl.BlockSpec((DN_WIDTH, D_MODEL), fixed),
                  pl.BlockSpec((FOX_WIDTH, D_MODEL), fixed),
                  pl.BlockSpec((tm, D_MODEL), row),
                  mod, mod, mod,
                  pl.BlockSpec((1, D_MODEL), fixed),
                  pl.BlockSpec((1, D_MODEL), fixed),
                  pl.BlockSpec((D_MODEL, 2 * LANES), fixed),
                  pl.BlockSpec((LANES, 1), fixed)],
        out_specs=[pl.BlockSpec((tm, D_MODEL), row),
                   pl.BlockSpec((tm * TOKEN_TILE, LANES), row),
                   pl.BlockSpec((8, tm), lambda i: (0, i)),
                   pl.BlockSpec((N_EXPERTS, LANES), fixed)],
        out_shape=[jax.ShapeDtypeStruct((n, D_MODEL), F32),
                   jax.ShapeDtypeStruct((n * TOKEN_TILE, LANES), U32),
                   jax.ShapeDtypeStruct((8, n), F32),
                   jax.ShapeDtypeStruct((N_EXPERTS, LANES), F32)],
        scratch_shapes=[pltpu.VMEM((tm, tm), BF16),
                        pltpu.VMEM((N_EXPERTS, 1), F32)],
        compiler_params=pltpu.CompilerParams(dimension_semantics=("arbitrary",),
                                             vmem_limit_bytes=VMEM_LIMIT),
    )(o_dn, o_fox, w1, w2, x2, g1, sc2, sh2, ln_g.reshape(1, D_MODEL), ln_b.reshape(1, D_MODEL),
      wr_t, br_t)


INVERT_STEPS = N_EXPERTS


def _invert_kernel(dest_ref, pad_lo_ref, pad_hi_ref, tok_ref):
    phase = pl.program_id(0)
    i = pl.program_id(1)
    n_tok = dest_ref.shape[0] // TOP_K
    toks_per = n_tok // INVERT_STEPS

    @pl.when(phase == 0)
    def _():
        def clear(p, carry):
            tok_ref[p] = 0
            return carry
        lax.fori_loop(pad_lo_ref[i], pad_hi_ref[i], clear, 0)

    @pl.when(phase == 1)
    def _():
        for k in range(TOP_K):
            def put(t, carry):
                tok = i * toks_per + t
                tok_ref[dest_ref[k * n_tok + tok]] = tok * TOKEN_TILE
                return carry
            lax.fori_loop(0, toks_per, put, 0, unroll=8)


def _invert(dest_flat, pad_lo, pad_hi, n_slots):
    assert (dest_flat.shape[0] // TOP_K) % INVERT_STEPS == 0
    smem = pl.BlockSpec(memory_space=pltpu.SMEM)
    return pl.pallas_call(
        _invert_kernel,
        grid=(2, INVERT_STEPS),
        in_specs=[smem, smem, smem],
        out_specs=smem,
        out_shape=jax.ShapeDtypeStruct((n_slots,), jnp.int32),
        compiler_params=pltpu.CompilerParams(dimension_semantics=("arbitrary", "arbitrary")),
    )(dest_flat, pad_lo, pad_hi)


def _tile_gather_start(idx_ref, base, src_hbm, dst, dst_row0, sem, count):
    for r in range(count):
        row = pl.multiple_of(idx_ref[base + r], TOKEN_TILE)
        pltpu.make_async_copy(src_hbm.at[pl.ds(row, TOKEN_TILE)],
                              dst.at[pl.ds(dst_row0 + r * TOKEN_TILE, TOKEN_TILE)],
                              sem).start(priority=r % 2)


def _tile_gather_wait(src_hbm, dst, dst_row0, sem, count):
    rows = count * TOKEN_TILE
    pltpu.make_async_copy(src_hbm.at[pl.ds(0, rows)], dst.at[pl.ds(dst_row0, rows)], sem).wait()


GATHER_DEPTH = 3


def _moe_kernel(blk_expert_ref, n_active_ref, tok_ref, h_hbm, wg_ref, wu_ref, wd_ref,
                y_ref, xbuf0, xbuf1, xbuf2, sem):
    b = pl.program_id(0)
    nb = pl.num_programs(0)
    n_active = n_active_ref[0]
    last = jnp.maximum(jnp.minimum(n_active, nb), 1) - 1
    bufs = (xbuf0, xbuf1, xbuf2)

    def fetch(blk, c):
        _tile_gather_start(tok_ref, blk * MOE_BLOCK, h_hbm, bufs[c], 0, sem.at[c], MOE_BLOCK)

    def drain(c):
        _tile_gather_wait(h_hbm, bufs[c], 0, sem.at[c], MOE_BLOCK)

    @pl.when(b == 0)
    def _():
        fetch(0, 0)
        fetch(jnp.minimum(1, last), 1)

    def block(c):
        drain(c)
        parts = [_unpack_halves(p) for p in _load_token_tiles(bufs[c], 0, MOE_BLOCK)]
        lo = jnp.concatenate([p[0] for p in parts], axis=1).astype(BF16)
        hi = jnp.concatenate([p[1] for p in parts], axis=1).astype(BF16)
        fetch(jnp.minimum(b + 2, last), (c + 2) % GATHER_DEPTH)
        gate = (jnp.dot(lo, wg_ref[0, :HALF, :], preferred_element_type=F32)
                + jnp.dot(hi, wg_ref[0, HALF:, :], preferred_element_type=F32))
        up = (jnp.dot(lo, wu_ref[0, :HALF, :], preferred_element_type=F32)
              + jnp.dot(hi, wu_ref[0, HALF:, :], preferred_element_type=F32))
        hid = (gate * _sigmoid(gate) * up).astype(BF16)
        y = jnp.dot(hid, wd_ref[0], preferred_element_type=F32)
        _store_token_tiles(y_ref, _pack_halves(y))

        @pl.when(b == last)
        def _():
            drain((c + 1) % GATHER_DEPTH)
            drain((c + 2) % GATHER_DEPTH)

    for c in range(GATHER_DEPTH):
        pl.when(jnp.logical_and(b <= last, lax.rem(b, GATHER_DEPTH) == c))(functools.partial(block, c))

    @pl.when(b > last)
    def _():
        y_ref[...] = jnp.zeros_like(y_ref)


def _moe(blk_expert, n_active, tok_buf, h2p, wg, wu, wd):
    n_blocks = blk_expert.shape[0]
    buf_rows = MOE_BLOCK * TOKEN_TILE
    grid_spec = pltpu.PrefetchScalarGridSpec(
        num_scalar_prefetch=3,
        grid=(n_blocks,),
        in_specs=[pl.BlockSpec(memory_space=pl.ANY),
                  pl.BlockSpec((1, D_MODEL, D_EXPERT), lambda b, be, na, tk: (be[b], 0, 0)),
                  pl.BlockSpec((1, D_MODEL, D_EXPERT), lambda b, be, na, tk: (be[b], 0, 0)),
                  pl.BlockSpec((1, D_EXPERT, D_MODEL), lambda b, be, na, tk: (be[b], 0, 0))],
        out_specs=pl.BlockSpec((buf_rows, LANES), lambda b, be, na, tk: (b, 0)),
        scratch_shapes=[pltpu.VMEM((buf_rows, LANES), U32)] * GATHER_DEPTH
        + [pltpu.SemaphoreType.DMA((GATHER_DEPTH,))])
    return pl.pallas_call(
        _moe_kernel,
        grid_spec=grid_spec,
        out_shape=jax.ShapeDtypeStruct((n_blocks * buf_rows, LANES), U32),
        compiler_params=pltpu.CompilerParams(dimension_semantics=("arbitrary",),
                                             vmem_limit_bytes=VMEM_LIMIT),
    )(blk_expert, n_active, tok_buf, h2p, wg, wu, wd)


def _combine_kernel(dest_ref, y_hbm, x1_ref, wt_ref, g2_ref, lng_ref, lnb_ref, o_ref,
                    ybuf0, ybuf1, ybuf2, sem):
    i = pl.program_id(0)
    nt = pl.num_programs(0)
    tm = x1_ref.shape[0]
    count = TOP_K * tm
    bufs = (ybuf0, ybuf1, ybuf2)

    def fetch(tile_idx, c):
        _tile_gather_start(dest_ref, tile_idx * count, y_hbm, bufs[c], 0, sem.at[c], count)

    def drain(c):
        _tile_gather_wait(y_hbm, bufs[c], 0, sem.at[c], count)

    @pl.when(i == 0)
    def _():
        fetch(0, 0)
        fetch(jnp.minimum(1, nt - 1), 1)

    def tile(c):
        drain(c)
        first = [_unpack_halves(p) for p in _load_token_tiles(bufs[c], 0, tm)]
        second = [_unpack_halves(p) for p in _load_token_tiles(bufs[c], tm * TOKEN_TILE, tm)]
        fetch(jnp.minimum(i + 2, nt - 1), (c + 2) % GATHER_DEPTH)
        w1 = wt_ref[:, 0:1]
        w2 = wt_ref[:, 1:2]
        lo = [w1 * a[0] + w2 * b[0] for a, b in zip(first, second)]
        hi = [w1 * a[1] + w2 * b[1] for a, b in zip(first, second)]
        y = jnp.concatenate(lo + hi, axis=1)
        r = ALPHA * x1_ref[...] + (1.0 + g2_ref[0]) * y
        o_ref[...] = _layer_norm(r, lng_ref[...], lnb_ref[...])

        @pl.when(i == nt - 1)
        def _():
            drain((c + 1) % GATHER_DEPTH)
            drain((c + 2) % GATHER_DEPTH)

    for c in range(GATHER_DEPTH):
        pl.when(lax.rem(i, GATHER_DEPTH) == c)(functools.partial(tile, c))


def _combine(dest_tiled, y_buf, x1, wt, g2, ln_g, ln_b, seq, tm):
    n = x1.shape[0]
    per_batch = seq // tm
    grid_spec = pltpu.PrefetchScalarGridSpec(
        num_scalar_prefetch=1,
        grid=(n // tm,),
        in_specs=[pl.BlockSpec(memory_space=pl.ANY),
                  pl.BlockSpec((tm, D_MODEL), lambda i, d: (i, 0)),
                  pl.BlockSpec((tm, LANES), lambda i, d: (i, 0)),
                  pl.BlockSpec((1, 1, D_MODEL), lambda i, d: (i // per_batch, 0, 0)),
                  pl.BlockSpec((1, D_MODEL), lambda i, d: (0, 0)),
                  pl.BlockSpec((1, D_MODEL), lambda i, d: (0, 0))],
        out_specs=pl.BlockSpec((tm, D_MODEL), lambda i, d: (i, 0)),
        scratch_shapes=[pltpu.VMEM((TOP_K * tm * TOKEN_TILE, LANES), U32)] * GATHER_DEPTH
        + [pltpu.SemaphoreType.DMA((GATHER_DEPTH,))])
    return pl.pallas_call(
        _combine_kernel,
        grid_spec=grid_spec,
        out_shape=jax.ShapeDtypeStruct((n, D_MODEL), F32),
        compiler_params=pltpu.CompilerParams(dimension_semantics=("arbitrary",),
                                             vmem_limit_bytes=VMEM_LIMIT),
    )(dest_tiled, y_buf, x1, wt, g2, ln_g.reshape(1, D_MODEL), ln_b.reshape(1, D_MODEL))


def _dispatch_plan(route, counts, n):
    m = n * TOP_K
    n_blocks = -(-m // MOE_BLOCK) + N_EXPERTS
    expert = route[0:2].astype(jnp.int32)
    rank = route[4:6].astype(jnp.int32)
    sizes = counts[:, 0].astype(jnp.int32)
    padded = (sizes + MOE_BLOCK - 1) // MOE_BLOCK * MOE_BLOCK
    p_end = jnp.cumsum(padded)
    p_start = p_end - padded
    ids = jnp.arange(N_EXPERTS, dtype=jnp.int32)
    start_of = jnp.sum(jnp.where(expert[..., None] == ids, p_start, 0), axis=-1)
    dest = start_of + rank
    blk_start = jnp.arange(n_blocks, dtype=jnp.int32) * MOE_BLOCK
    blk_expert = jnp.minimum(jnp.sum((p_end[None, :] <= blk_start[:, None]).astype(jnp.int32), axis=1),
                             N_EXPERTS - 1)
    n_active = (p_end[-1] // MOE_BLOCK).astype(jnp.int32).reshape(1)
    pad_lo = p_start + sizes
    pad_hi = p_end.at[N_EXPERTS - 1].set(n_blocks * MOE_BLOCK)
    return dest, blk_expert, n_active, n_blocks, pad_lo, pad_hi


def kernel(x, c, w_ada, b_ada, w_in, dn_conv_w, dn_a_log, dn_dt_bias, dn_norm_w, fox_f_bias, w_out,
           ln1_g, ln1_b, w_router_group, b_router_group, w_router_expert, b_router_expert,
           w_gate, w_up, w_down, ln2_g, ln2_b):
    bsz, seq, _ = x.shape
    n = bsz * seq
    x2 = x.reshape(n, D_MODEL)

    mod = _ada(c, w_ada[0], b_ada[0])
    sh1, sc1, g1, sh2, sc2, g2 = [m.reshape(bsz, 1, D_MODEL) for m in jnp.split(mod, 6, axis=-1)]

    wi = w_in[0]
    o_b = 4 * DN_WIDTH
    o_a = o_b + DN_HEADS
    o_fq = o_a + DN_HEADS
    o_f = o_fq + 3 * FOX_WIDTH
    w_main = jnp.concatenate([wi[:, :o_b], wi[:, o_fq:o_f]], axis=1).astype(BF16)
    w_small = jnp.concatenate([wi[:, o_b:o_fq], wi[:, o_f:],
                               jnp.zeros((D_MODEL, LANES - 2 * DN_HEADS - FOX_HEADS), F32)],
                              axis=1).astype(BF16)
    zeros8 = jnp.zeros((DN_HEADS,), F32)
    pad = jnp.zeros((LANES - 2 * DN_HEADS - FOX_HEADS,), F32)
    gate_par = jnp.stack([jnp.concatenate([zeros8, dn_dt_bias[0], fox_f_bias[0], pad]),
                          jnp.concatenate([zeros8, dn_a_log[0], zeros8, pad])])

    main, gates = _inproj(x2, sc1, sh1, w_main, w_small, gate_par, seq)
    main3 = main.reshape(bsz, seq, D_MAIN)
    gates3 = gates.reshape(bsz, seq, LANES)
    gc3, fk_aug = _cumsum(gates3)
    gct = jnp.swapaxes(gc3[:, :, DN_HEADS:2 * DN_HEADS], 1, 2)

    o_dn = _deltanet(main3, gates3, gc3, gct, dn_conv_w[0], dn_norm_w[0])
    o_fox = _fox(main3, fk_aug)

    wo = w_out[0].astype(BF16)
    wr_t = jnp.concatenate([w_router_group[0].T, jnp.zeros((8 - N_GROUPS, D_MODEL), F32),
                            jnp.transpose(w_router_expert[0], (0, 2, 1)).reshape(N_EXPERTS, D_MODEL)],
                           axis=0)
    br_t = jnp.concatenate([b_router_group[0], jnp.zeros((8 - N_GROUPS,), F32),
                            b_router_expert[0].reshape(-1)])
    br_col = jnp.pad(br_t, (0, LANES - ROUTE_ROWS)).reshape(LANES, 1)
    wr_hi = wr_t.astype(BF16)
    wr_lo = (wr_t - wr_hi.astype(F32)).astype(BF16)
    lane_pad = ((0, 0), (0, LANES - ROUTE_ROWS))
    wr_split = jnp.concatenate([jnp.pad(wr_hi.T, lane_pad), jnp.pad(wr_lo.T, lane_pad)], axis=1)
    x1, h2p, route, counts = _outproj(o_dn.reshape(n, DN_WIDTH), o_fox.reshape(n, FOX_WIDTH),
                                      wo[:DN_WIDTH], wo[DN_WIDTH:], x2, g1, sc2, sh2, ln1_g[0], ln1_b[0],
                                      wr_split, br_col, seq)

    dest, blk_expert, n_active, n_blocks, pad_lo, pad_hi = _dispatch_plan(route, counts, n)
    tok_buf = _invert(dest.reshape(-1), pad_lo, pad_hi, n_blocks * MOE_BLOCK)
    y_buf = _moe(blk_expert, n_active, tok_buf, h2p,
                 w_gate[0].astype(BF16), w_up[0].astype(BF16), w_down[0].astype(BF16))

    tm = min(256, seq)
    dest_tiled = (dest * TOKEN_TILE).reshape(TOP_K, n // tm, tm).transpose(1, 0, 2).reshape(-1)
    wt = jnp.pad(route[2:4].T, ((0, 0), (0, LANES - TOP_K)))
    out = _combine(dest_tiled, y_buf, x1, wt, g2, ln2_g[0], ln2_b[0], seq, tm)
    return out.reshape(bsz, seq, D_MODEL)
```

```python
import functools
import math

import jax
import jax.numpy as jnp
from jax import lax
from jax.experimental import pallas as pl
from jax.experimental.pallas import tpu as pltpu

F32 = jnp.float32
BF16 = jnp.bfloat16
U32 = jnp.uint32
HIGHEST = lax.Precision.HIGHEST

D_MODEL = 2048
HEAD_DIM = 128
DN_HEADS = 8
FOX_HEADS = 8
DN_WIDTH = DN_HEADS * HEAD_DIM
FOX_WIDTH = FOX_HEADS * HEAD_DIM
CONV_WIDTH = 4
N_GROUPS = 4
EXPERTS_PER_GROUP = 8
N_EXPERTS = N_GROUPS * EXPERTS_PER_GROUP
TOP_K = 2
D_EXPERT = 512
MOE_BLOCK = 256
LN_EPS = 1e-5
NORM_EPS = 1e-6
ALPHA = 2.0 ** 0.25

D_MAIN = 3 * DN_WIDTH + DN_WIDTH + 3 * FOX_WIDTH
LANES = 128
BF16_SUBLANES = 16
DN_CHUNK = 128
NEG_BIG = -1e30
LOG2E = 1.4426950408889634
HALF = D_MODEL // 2
VMEM_LIMIT = 56 * 1024 * 1024


def _sigmoid(x):
    return 1.0 / (1.0 + jnp.exp(-x))


def _softplus(x):
    return jnp.maximum(x, 0.0) + jnp.log(1.0 + jnp.exp(-jnp.abs(x)))


def _mm(a, b):
    return jnp.dot(a.astype(BF16), b.astype(BF16), preferred_element_type=F32)


def _mm_nt(a, b):
    return lax.dot_general(a.astype(BF16), b.astype(BF16), (((1,), (1,)), ((), ())),
                           preferred_element_type=F32)


def _mm_tn(a, b):
    return lax.dot_general(a.astype(BF16), b.astype(BF16), (((0,), (0,)), ((), ())),
                           preferred_element_type=F32)


def _pack_halves(x):
    c = x.shape[1] // 2
    lo = pltpu.bitcast(x[:, :c].astype(BF16).astype(F32), U32)
    hi = pltpu.bitcast(x[:, c:].astype(BF16).astype(F32), U32)
    return (lo >> 16) | (hi & jnp.uint32(0xFFFF0000))


def _unpack_halves(p):
    lo = pltpu.bitcast(p << 16, F32)
    hi = pltpu.bitcast(p & jnp.uint32(0xFFFF0000), F32)
    return lo, hi


def _layer_norm(r, g, b):
    mu = jnp.mean(r, axis=-1, keepdims=True)
    d = r - mu
    var = jnp.mean(d * d, axis=-1, keepdims=True)
    return d * lax.rsqrt(var + LN_EPS) * g + b


def _ada_kernel(c_ref, w_ref, b_ref, o_ref):
    c = c_ref[...]
    a = c * _sigmoid(c)
    o_ref[...] = jnp.dot(a, w_ref[...], preferred_element_type=F32, precision=HIGHEST) + b_ref[...]


def _ada(c, w_ada, b_ada):
    bsz = c.shape[0]
    n_out = w_ada.shape[1]
    tn = 1024
    return pl.pallas_call(
        _ada_kernel,
        grid=(n_out // tn,),
        in_specs=[pl.BlockSpec((bsz, D_MODEL), lambda j: (0, 0)),
                  pl.BlockSpec((D_MODEL, tn), lambda j: (0, j)),
                  pl.BlockSpec((1, tn), lambda j: (0, j))],
        out_specs=pl.BlockSpec((bsz, tn), lambda j: (0, j)),
        out_shape=jax.ShapeDtypeStruct((bsz, n_out), F32),
        compiler_params=pltpu.CompilerParams(dimension_semantics=("arbitrary",),
                                             vmem_limit_bytes=VMEM_LIMIT),
    )(c, w_ada, b_ada.reshape(1, n_out))


def _inproj_kernel(x_ref, sc_ref, sh_ref, w_ref, ws_ref, par_ref, o_ref, g_ref, h_scr):
    @pl.when(pl.program_id(1) == 0)
    def _():
        h = (x_ref[...] * (1.0 + sc_ref[0]) + sh_ref[0]).astype(BF16)
        h_scr[...] = h
        s = jnp.dot(h, ws_ref[...], preferred_element_type=F32)
        lane = lax.broadcasted_iota(jnp.int32, s.shape, 1)
        sb = s + par_ref[0:1, :]
        neg_a = -jnp.exp(par_ref[1:2, :])
        beta = _sigmoid(s)
        g = neg_a * _softplus(sb)
        log_f = -_softplus(-sb)
        g_ref[...] = jnp.where(lane < DN_HEADS, beta,
                               jnp.where(lane < 2 * DN_HEADS, g,
                                         jnp.where(lane < 2 * DN_HEADS + FOX_HEADS, log_f, 0.0)))

    o_ref[...] = jnp.dot(h_scr[...], w_ref[...], preferred_element_type=F32).astype(BF16)


def _inproj(x2, sc1, sh1, w_main, w_small, gate_par, seq):
    n = x2.shape[0]
    tm = min(1024, seq)
    tn = 1024
    per_batch = seq // tm
    return pl.pallas_call(
        _inproj_kernel,
        grid=(n // tm, D_MAIN // tn),
        in_specs=[pl.BlockSpec((tm, D_MODEL), lambda i, j: (i, 0)),
                  pl.BlockSpec((1, 1, D_MODEL), lambda i, j: (i // per_batch, 0, 0)),
                  pl.BlockSpec((1, 1, D_MODEL), lambda i, j: (i // per_batch, 0, 0)),
                  pl.BlockSpec((D_MODEL, tn), lambda i, j: (0, j)),
                  pl.BlockSpec((D_MODEL, LANES), lambda i, j: (0, 0)),
                  pl.BlockSpec((2, LANES), lambda i, j: (0, 0))],
        out_specs=[pl.BlockSpec((tm, tn), lambda i, j: (i, j)),
                   pl.BlockSpec((tm, LANES), lambda i, j: (i, 0))],
        out_shape=[jax.ShapeDtypeStruct((n, D_MAIN), BF16),
                   jax.ShapeDtypeStruct((n, LANES), F32)],
        scratch_shapes=[pltpu.VMEM((tm, D_MODEL), BF16)],
        compiler_params=pltpu.CompilerParams(dimension_semantics=("arbitrary", "arbitrary"),
                                             vmem_limit_bytes=VMEM_LIMIT),
    )(x2, sc1, sh1, w_main, w_small, gate_par)


def _cumsum_kernel(g_ref, gc_ref, fk_ref, carry, tri_scr):
    first = jnp.logical_and(pl.program_id(0) == 0, pl.program_id(1) == 0)

    @pl.when(pl.program_id(1) == 0)
    def _():
        carry[...] = jnp.zeros_like(carry)

    x = g_ref[0]
    t = x.shape[0]

    @pl.when(first)
    def _():
        row = lax.broadcasted_iota(jnp.int32, (t, t), 0)
        col = lax.broadcasted_iota(jnp.int32, (t, t), 1)
        shift = int(math.log2(DN_CHUNK))
        tri_scr[:t, :] = jnp.where(col <= row, 1.0, 0.0).astype(BF16)
        tri_scr[t:, :] = jnp.where(jnp.logical_and(col <= row, (row >> shift) == (col >> shift)),
                                   1.0, 0.0).astype(BF16)

    hi = x.astype(BF16)
    r1 = x - hi.astype(F32)
    mid = r1.astype(BF16)
    lo = (r1 - mid.astype(F32)).astype(BF16)
    parts = jnp.dot(tri_scr[...], jnp.concatenate([hi, mid, lo], axis=1), preferred_element_type=F32)
    sums = parts[:, :LANES] + parts[:, LANES:2 * LANES] + parts[:, 2 * LANES:]
    cs = sums[:t]
    f = cs + carry[...]
    gc_ref[0] = sums[t:]
    carry[...] = carry[...] + cs[t - 1:t, :]

    lane = lax.broadcasted_iota(jnp.int32, (t, LANES), 1)
    for h in range(FOX_HEADS):
        c0 = 2 * DN_HEADS + h
        val = jnp.broadcast_to(f[:, c0:c0 + 1] * (-LOG2E), (t, LANES))
        hi = val.astype(BF16)
        r1 = val - hi.astype(F32)
        mid = r1.astype(BF16)
        lo = r1 - mid.astype(F32)
        fk_ref[0, :, h * LANES:(h + 1) * LANES] = jnp.where(
            lane == 0, hi.astype(F32),
            jnp.where(lane == 1, mid.astype(F32), jnp.where(lane == 2, lo, 0.0))).astype(BF16)


def _cumsum(gates3):
    bsz, seq, _ = gates3.shape
    t = min(512, seq)
    spec = pl.BlockSpec((1, t, LANES), lambda b, i: (b, i, 0))
    return pl.pallas_call(
        _cumsum_kernel,
        grid=(bsz, seq // t),
        in_specs=[spec],
        out_specs=[spec, pl.BlockSpec((1, t, FOX_HEADS * LANES), lambda b, i: (b, i, 0))],
        out_shape=[jax.ShapeDtypeStruct(gates3.shape, F32),
                   jax.ShapeDtypeStruct((bsz, seq, FOX_HEADS * LANES), BF16)],
        scratch_shapes=[pltpu.VMEM((1, LANES), F32), pltpu.VMEM((2 * t, t), BF16)],
        compiler_params=pltpu.CompilerParams(dimension_semantics=("arbitrary", "arbitrary")),
    )(gates3)


def _dn_kernel(qkv_ref, halo_ref, z_ref, gate_ref, gc_ref, gct_ref, convw_ref, normw_ref,
               o_ref, state_ref, xs_scr):
    t = pl.program_id(1)

    @pl.when(t == 0)
    def _():
        state_ref[...] = jnp.zeros_like(state_ref)

    c = DN_CHUNK
    d = HEAD_DIM
    heads = range(DN_HEADS)
    row = lax.broadcasted_iota(jnp.int32, (c, c), 0)
    col = lax.broadcasted_iota(jnp.int32, (c, c), 1)
    strict = col < row
    eye = (row == col).astype(F32)
    neg_diag8_f = jnp.where((row >> 3) == (col >> 3), -1.0, 0.0)
    merge_levels = range(3, int(math.log2(c)))
    low_f = [jnp.where((row >> (s + 1)) == (col >> (s + 1)),
                       jnp.where((row >> s) == (col >> s), 0.0, 1.0), 0.0) for s in merge_levels]
    halo_on = (t > 0).astype(F32)
    gates = gate_ref[0]
    gcs = gc_ref[0]
    gct = gct_ref[0]
    normw = normw_ref[...]

    def conv_silu(off):
        slab = off // d
        x = qkv_ref[0, :, off:off + d].astype(F32)
        xs_scr[slab, 0:8, :] = halo_ref[0, BF16_SUBLANES - 8:, off:off + d].astype(F32) * halo_on
        xs_scr[slab, 8:, :] = x
        w = convw_ref[:, off:off + d]
        acc = x * w[CONV_WIDTH - 1:CONV_WIDTH]
        for k in range(1, CONV_WIDTH):
            acc = acc + xs_scr[slab, pl.ds(8 - k, c), :] * w[CONV_WIDTH - 1 - k:CONV_WIDTH - k]
        return acc * _sigmoid(acc)

    def l2n(x):
        return x * lax.rsqrt(jnp.sum(x * x, axis=-1, keepdims=True) + NORM_EPS)

    def bdot(a, b):
        return jnp.dot(a, b, preferred_element_type=F32)

    q = [l2n(conv_silu(h * d)) * (d ** -0.5) for h in heads]
    k = [l2n(conv_silu(DN_WIDTH + h * d)) for h in heads]
    v = [conv_silu(2 * DN_WIDTH + h * d) for h in heads]
    beta = [gates[:, h:h + 1] for h in heads]
    g_col = [gcs[:, DN_HEADS + h:DN_HEADS + h + 1] for h in heads]
    e_g = [jnp.exp(g) for g in g_col]
    g_last = [g[c - 1:c, :] for g in g_col]
    decay = [jnp.exp(jnp.where(strict, g_col[h] - gct[h:h + 1, :], NEG_BIG)) for h in heads]
    kb = [k[h] * beta[h] for h in heads]

    kk = [_mm_nt(jnp.concatenate([kb[h], q[h]], axis=0), k[h]) for h in heads]
    a = [kk[h][:c] * decay[h] for h in heads]
    qk = [(kk[h][c:] * (decay[h] + eye)).astype(BF16) for h in heads]
    n1 = [a[h] * neg_diag8_f for h in heads]
    n1b = [n.astype(BF16) for n in n1]
    x = [eye + n for n in n1]
    xb = [xx.astype(BF16) for xx in x]
    n2b = [bdot(n, n).astype(BF16) for n in n1b]
    x = [x[h] + bdot(n2b[h], xb[h]) for h in heads]
    n4b = [bdot(n, n).astype(BF16) for n in n2b]
    xb = [xx.astype(BF16) for xx in x]
    x = [x[h] + bdot(n4b[h], xb[h]) for h in heads]
    for lf in low_f:
        xb = [xx.astype(BF16) for xx in x]
        lx = [bdot((a[h] * lf).astype(BF16), xb[h]).astype(BF16) for h in heads]
        x = [x[h] - bdot(xb[h], lx[h]) for h in heads]

    uw = [_mm(x[h], jnp.concatenate([v[h] * beta[h], kb[h] * e_g[h]], axis=1)) for h in heads]
    s_old = [state_ref[h] for h in heads]
    r = [_mm(jnp.concatenate([uw[h][:, d:], q[h] * e_g[h]], axis=0), s_old[h]) for h in heads]
    v_new = [(uw[h][:, :d] - r[h][:c]).astype(BF16) for h in heads]
    o = [r[h][c:] + bdot(qk[h], v_new[h]) for h in heads]
    for h in heads:
        kd = k[h] * jnp.exp(g_last[h] - g_col[h])
        state_ref[h] = s_old[h] * jnp.exp(g_last[h]) + _mm_tn(kd, v_new[h])
    for h in heads:
        z = z_ref[0, :, h * d:(h + 1) * d].astype(F32)
        rms = lax.rsqrt(jnp.mean(o[h] * o[h], axis=-1, keepdims=True) + NORM_EPS)
        o_ref[0, :, h * d:(h + 1) * d] = (o[h] * rms * normw * (z * _sigmoid(z))).astype(BF16)


def _deltanet(main3, gates3, gc3, gct, conv_w, norm_w):
    bsz, seq, _ = main3.shape
    c = DN_CHUNK
    halo_blocks = c // BF16_SUBLANES
    return pl.pallas_call(
        _dn_kernel,
        grid=(bsz, seq // c),
        in_specs=[pl.BlockSpec((1, c, 3 * DN_WIDTH), lambda b, t: (b, t, 0)),
                  pl.BlockSpec((1, BF16_SUBLANES, 3 * DN_WIDTH),
                               lambda b, t: (b, jnp.maximum(t * halo_blocks - 1, 0), 0)),
                  pl.BlockSpec((1, c, DN_WIDTH), lambda b, t: (b, t, 3)),
                  pl.BlockSpec((1, c, LANES), lambda b, t: (b, t, 0)),
                  pl.BlockSpec((1, c, LANES), lambda b, t: (b, t, 0)),
                  pl.BlockSpec((1, DN_HEADS, c), lambda b, t: (b, 0, t)),
                  pl.BlockSpec((CONV_WIDTH, 3 * DN_WIDTH), lambda b, t: (0, 0)),
                  pl.BlockSpec((1, HEAD_DIM), lambda b, t: (0, 0))],
        out_specs=pl.BlockSpec((1, c, DN_WIDTH), lambda b, t: (b, t, 0)),
        out_shape=jax.ShapeDtypeStruct((bsz, seq, DN_WIDTH), BF16),
        scratch_shapes=[pltpu.VMEM((DN_HEADS, HEAD_DIM, HEAD_DIM), F32),
                        pltpu.VMEM((3 * DN_HEADS, 8 + c, HEAD_DIM), F32)],
        compiler_params=pltpu.CompilerParams(dimension_semantics=("arbitrary", "arbitrary"),
                                             vmem_limit_bytes=VMEM_LIMIT),
    )(main3, main3, main3, gates3, gc3, gct, conv_w, norm_w.reshape(1, HEAD_DIM))


def _fox_kernel(qi_ref, kj_ref, q_ref, k_ref, v_ref, fk_ref, o_ref, qa_scr, m_scr, l_scr, acc_scr):
    step = pl.program_id(1)
    i = qi_ref[step]
    j = kj_ref[step]
    d = HEAD_DIM
    tq = q_ref.shape[1]
    tk = k_ref.shape[1]

    @pl.when(j == 0)
    def _():
        m_scr[...] = jnp.full_like(m_scr, NEG_BIG)
        l_scr[...] = jnp.zeros_like(l_scr)
        acc_scr[...] = jnp.zeros_like(acc_scr)
        lane = lax.broadcasted_iota(jnp.int32, (tq, LANES), 1)
        ones = jnp.where(lane < 3, 1.0, 0.0).astype(BF16)
        for h in range(FOX_HEADS):
            q = (q_ref[0, :, h * d:(h + 1) * d].astype(F32) * (d ** -0.5 * LOG2E)).astype(BF16)
            qa_scr[h] = jnp.concatenate([q, ones], axis=1)

    def update(diagonal):
        if diagonal:
            key_pos = lax.broadcasted_iota(jnp.int32, (tk, tq), 0)
            q_pos = lax.broadcasted_iota(jnp.int32, (tk, tq), 1)
            visible = key_pos <= q_pos

        def scores(h):
            k_aug = jnp.concatenate([k_ref[0, :, h * d:(h + 1) * d],
                                     fk_ref[0, :, h * LANES:(h + 1) * LANES]], axis=1)
            return lax.dot_general(k_aug, qa_scr[h], (((1,), (1,)), ((), ())),
                                   preferred_element_type=F32)

        s_next = scores(0)
        for h in range(FOX_HEADS):
            s = s_next
            if h + 1 < FOX_HEADS:
                s_next = scores(h + 1)
            v = v_ref[0, :, h * d:(h + 1) * d]
            if diagonal:
                s = jnp.where(visible, s, NEG_BIG)
            m_old = m_scr[h]
            m_new = jnp.maximum(m_old, jnp.max(s, axis=0, keepdims=True))
            alpha = jnp.exp2(m_old - m_new)
            p = jnp.exp2(s - m_new)
            l_scr[h] = alpha * l_scr[h] + jnp.sum(p, axis=0, keepdims=True)
            acc_scr[h] = alpha * acc_scr[h] + _mm_tn(v, p)
            m_scr[h] = m_new

    @pl.when(j < i)
    def _():
        update(False)

    @pl.when(j == i)
    def _():
        update(True)
        for h in range(FOX_HEADS):
            o = acc_scr[h] / l_scr[h]
            o_ref[0, :, h * d:(h + 1) * d] = o.T.astype(BF16)


def _fox(main3, fk_aug):
    bsz, seq, _ = main3.shape
    tq = min(512, seq)
    n_blk = seq // tq
    w = FOX_WIDTH
    base = (4 * DN_WIDTH) // w
    pairs = [(i, j) for i in range(n_blk) for j in range(i + 1)]
    qi = jnp.array([p[0] for p in pairs], jnp.int32)
    kj = jnp.array([p[1] for p in pairs], jnp.int32)
    grid_spec = pltpu.PrefetchScalarGridSpec(
        num_scalar_prefetch=2,
        grid=(bsz, len(pairs)),
        in_specs=[pl.BlockSpec((1, tq, w), lambda b, t, qi, kj: (b, qi[t], base)),
                  pl.BlockSpec((1, tq, w), lambda b, t, qi, kj: (b, kj[t], base + 1)),
                  pl.BlockSpec((1, tq, w), lambda b, t, qi, kj: (b, kj[t], base + 2)),
                  pl.BlockSpec((1, tq, FOX_HEADS * LANES), lambda b, t, qi, kj: (b, kj[t], 0))],
        out_specs=pl.BlockSpec((1, tq, w), lambda b, t, qi, kj: (b, qi[t], 0)),
        scratch_shapes=[pltpu.VMEM((FOX_HEADS, tq, 2 * HEAD_DIM), BF16),
                        pltpu.VMEM((FOX_HEADS, 1, tq), F32),
                        pltpu.VMEM((FOX_HEADS, 1, tq), F32),
                        pltpu.VMEM((FOX_HEADS, HEAD_DIM, tq), F32)])
    return pl.pallas_call(
        _fox_kernel,
        grid_spec=grid_spec,
        out_shape=jax.ShapeDtypeStruct((bsz, seq, w), BF16),
        compiler_params=pltpu.CompilerParams(
            dimension_semantics=("arbitrary", "arbitrary"),
            vmem_limit_bytes=VMEM_LIMIT),
    )(qi, kj, main3, main3, main3, fk_aug)


ROUTE_ROWS = 8 + N_EXPERTS
TOKEN_TILE = 8


OUTPROJ_SUBTILES = 1


def _store_token_tiles(ref, packed, first_token=0):
    m = packed.shape[0]
    for s in range(TOKEN_TILE):
        ref[pl.ds(first_token * TOKEN_TILE + s, m, stride=TOKEN_TILE), :] = packed[:, s * LANES:(s + 1) * LANES]


def _load_token_tiles(ref, start, m):
    return [ref[pl.ds(start + s, m, stride=TOKEN_TILE), :] for s in range(TOKEN_TILE)]


def _outproj_kernel(odn_ref, ofox_ref, w1_ref, w2_ref, x_ref, g1_ref, sc2_ref, sh2_ref,
                    lng_ref, lnb_ref, wr_ref, br_ref, x1_ref, h2p_ref, route_ref, count_ref,
                    upper_scr, carry_scr):
    tm = x_ref.shape[0]

    @pl.when(pl.program_id(0) == 0)
    def _():
        r = lax.broadcasted_iota(jnp.int32, (tm, tm), 0)
        c = lax.broadcasted_iota(jnp.int32, (tm, tm), 1)
        upper_scr[...] = jnp.where(r < c, 1.0, 0.0).astype(BF16)
        carry_scr[...] = jnp.zeros_like(carry_scr)

    sub = tm // OUTPROJ_SUBTILES
    ys = []
    for u in range(OUTPROJ_SUBTILES):
        rows = slice(u * sub, (u + 1) * sub)
        ys.append(jnp.dot(odn_ref[rows, :], w1_ref[...], preferred_element_type=F32)
                  + jnp.dot(ofox_ref[rows, :], w2_ref[...], preferred_element_type=F32))
    for u in range(OUTPROJ_SUBTILES):
        _outproj_tail(u, sub, ys[u], x_ref, g1_ref, sc2_ref, sh2_ref, lng_ref, lnb_ref, wr_ref, br_ref,
                      x1_ref, h2p_ref, route_ref, upper_scr, carry_scr)
    count_ref[...] = jnp.broadcast_to(carry_scr[...], count_ref.shape)


def _outproj_tail(u, tm, y, x_ref, g1_ref, sc2_ref, sh2_ref, lng_ref, lnb_ref, wr_ref, br_ref,
                  x1_ref, h2p_ref, route_ref, upper_scr, carry_scr):
    rows = slice(u * tm, (u + 1) * tm)
    r = ALPHA * x_ref[rows, :] + (1.0 + g1_ref[0]) * y
    x1 = _layer_norm(r, lng_ref[...], lnb_ref[...])
    x1_ref[rows, :] = x1
    h2 = x1 * (1.0 + sc2_ref[0]) + sh2_ref[0]
    _store_token_tiles(h2p_ref, _pack_halves(h2), u * tm)

    h_hi = h2.astype(BF16)
    h_lo = (h2 - h_hi.astype(F32)).astype(BF16)
    parts = jnp.dot(jnp.concatenate([h_hi, h_lo], axis=0), wr_ref[...], preferred_element_type=F32)
    terms = parts[:tm] + parts[tm:]
    lt = (terms[:, :LANES] + terms[:, LANES:]).T + br_ref[...]
    row = [lt[n:n + 1, :] for n in range(ROUTE_ROWS)]

    def first_max(vals):
        best = vals[0]
        for val in vals[1:]:
            best = jnp.maximum(best, val)
        idx = jnp.full(best.shape, len(vals) - 1, jnp.int32)
        for n in range(len(vals) - 2, -1, -1):
            idx = jnp.where(vals[n] == best, n, idx)
        return best, idx

    gmax, gidx = first_max(row[:N_GROUPS])
    gsum = jnp.exp(row[0] - gmax)
    for n in range(1, N_GROUPS):
        gsum = gsum + jnp.exp(row[n] - gmax)
    g_w = 1.0 / gsum
    el = []
    for e in range(EXPERTS_PER_GROUP):
        val = row[8 + e]
        for g in range(1, N_GROUPS):
            val = jnp.where(gidx == g, row[8 + EXPERTS_PER_GROUP * g + e], val)
        el.append(val)
    emax, _ = first_max(el)
    ee = [jnp.exp(val - emax) for val in el]
    esum = ee[0]
    for val in ee[1:]:
        esum = esum + val
    prob = [val / esum for val in ee]
    p1, e1 = first_max(prob)
    p2, e2 = first_max([jnp.where(e1 == e, -1.0, prob[e]) for e in range(EXPERTS_PER_GROUP)])
    denom = p1 + p2
    ex1 = gidx * EXPERTS_PER_GROUP + e1
    ex2 = gidx * EXPERTS_PER_GROUP + e2
    ridx = lax.broadcasted_iota(jnp.int32, (8, tm), 0)

    eidx = lax.broadcasted_iota(jnp.int32, (N_EXPERTS, tm), 0)
    oh1 = jnp.where(eidx == ex1, 1.0, 0.0)
    oh2 = jnp.where(eidx == ex2, 1.0, 0.0)
    both = oh1 + oh2
    before = jnp.dot(both.astype(BF16), upper_scr[:tm, :tm], preferred_element_type=F32) + carry_scr[...]
    rank1 = jnp.sum(oh1 * before, axis=0, keepdims=True)
    rank2 = jnp.sum(oh2 * before, axis=0, keepdims=True)
    carry_scr[...] = carry_scr[...] + jnp.sum(both, axis=1, keepdims=True)

    fields = [ex1.astype(F32), ex2.astype(F32), g_w * p1 / denom, g_w * p2 / denom, rank1, rank2]
    out = jnp.zeros((8, tm), F32)
    for n, val in enumerate(fields):
        out = jnp.where(ridx == n, val, out)
    route_ref[:, rows] = out


def _outproj(o_dn, o_fox, w1, w2, x2, g1, sc2, sh2, ln_g, ln_b, wr_t, br_t, seq):
    n = x2.shape[0]
    tm = min(512, seq)
    per_batch = seq // tm
    row = lambda i: (i, 0)
    fixed = lambda i: (0, 0)
    mod = pl.BlockSpec((1, 1, D_MODEL), lambda i: (i // per_batch, 0, 0))
    return pl.pallas_call(
        _outproj_kernel,
        grid=(n // tm,),
        in_specs=[pl.BlockSpec((tm, DN_WIDTH), row),
                  pl.BlockSpec((tm, FOX_WIDTH), row),
                  pl.BlockSpec((DN_WIDTH, D_MODEL), fixed),
                  pl.BlockSpec((FOX_WIDTH, D_MODEL), fixed),
                  pl.BlockSpec((tm, D_MODEL), row),
                  mod, mod, mod,
                  pl.BlockSpec((1, D_MODEL), fixed),
                  pl.BlockSpec((1, D_MODEL), fixed),
                  pl.BlockSpec((D_MODEL, 2 * LANES), fixed),
                  pl.BlockSpec((LANES, 1), fixed)],
        out_specs=[pl.BlockSpec((tm, D_MODEL), row),
                   pl.BlockSpec((tm * TOKEN_TILE, LANES), row),
                   pl.BlockSpec((8, tm), lambda i: (0, i)),
                   pl.BlockSpec((N_EXPERTS, LANES), fixed)],
        out_shape=[jax.ShapeDtypeStruct((n, D_MODEL), F32),
                   jax.ShapeDtypeStruct((n * TOKEN_TILE, LANES), U32),
                   jax.ShapeDtypeStruct((8, n), F32),
                   jax.ShapeDtypeStruct((N_EXPERTS, LANES), F32)],
        scratch_shapes=[pltpu.VMEM((tm, tm), BF16),
                        pltpu.VMEM((N_EXPERTS, 1), F32)],
        compiler_params=pltpu.CompilerParams(dimension_semantics=("arbitrary",),
                                             vmem_limit_bytes=VMEM_LIMIT),
    )(o_dn, o_fox, w1, w2, x2, g1, sc2, sh2, ln_g.reshape(1, D_MODEL), ln_b.reshape(1, D_MODEL),
      wr_t, br_t)


INVERT_STEPS = N_EXPERTS


def _invert_kernel(dest_ref, pad_lo_ref, pad_hi_ref, tok_ref):
    phase = pl.program_id(0)
    i = pl.program_id(1)
    n_tok = dest_ref.shape[0] // TOP_K
    toks_per = n_tok // INVERT_STEPS

    @pl.when(phase == 0)
    def _():
        def clear(p, carry):
            tok_ref[p] = 0
            return carry
        lax.fori_loop(pad_lo_ref[i], pad_hi_ref[i], clear, 0)

    @pl.when(phase == 1)
    def _():
        def put(t, carry):
            tok = i * toks_per + t
            for k in range(TOP_K):
                tok_ref[dest_ref[k * n_tok + tok]] = tok * TOKEN_TILE
            return carry
        lax.fori_loop(0, toks_per, put, 0, unroll=8)


def _invert(dest_flat, pad_lo, pad_hi, n_slots):
    assert (dest_flat.shape[0] // TOP_K) % INVERT_STEPS == 0
    smem = pl.BlockSpec(memory_space=pltpu.SMEM)
    return pl.pallas_call(
        _invert_kernel,
        grid=(2, INVERT_STEPS),
        in_specs=[smem, smem, smem],
        out_specs=smem,
        out_shape=jax.ShapeDtypeStruct((n_slots,), jnp.int32),
        compiler_params=pltpu.CompilerParams(dimension_semantics=("arbitrary", "arbitrary")),
    )(dest_flat, pad_lo, pad_hi)


def _tile_gather_start(idx_ref, base, src_hbm, dst, dst_row0, sem, count):
    for r in range(count):
        row = pl.multiple_of(idx_ref[base + r], TOKEN_TILE)
        pltpu.make_async_copy(src_hbm.at[pl.ds(row, TOKEN_TILE)],
                              dst.at[pl.ds(dst_row0 + r * TOKEN_TILE, TOKEN_TILE)],
                              sem).start(priority=r % 2)


def _tile_gather_wait(src_hbm, dst, dst_row0, sem, count):
    rows = count * TOKEN_TILE
    pltpu.make_async_copy(src_hbm.at[pl.ds(0, rows)], dst.at[pl.ds(dst_row0, rows)], sem).wait()


GATHER_DEPTH = 3


def _moe_kernel(blk_expert_ref, n_active_ref, tok_ref, h_hbm, wg_ref, wu_ref, wd_ref,
                y_ref, xbuf0, xbuf1, xbuf2, sem):
    b = pl.program_id(0)
    nb = pl.num_programs(0)
    n_active = n_active_ref[0]
    last = jnp.maximum(jnp.minimum(n_active, nb), 1) - 1
    bufs = (xbuf0, xbuf1, xbuf2)

    def fetch(blk, c):
        _tile_gather_start(tok_ref, blk * MOE_BLOCK, h_hbm, bufs[c], 0, sem.at[c], MOE_BLOCK)

    def drain(c):
        _tile_gather_wait(h_hbm, bufs[c], 0, sem.at[c], MOE_BLOCK)

    @pl.when(b == 0)
    def _():
        fetch(0, 0)
        fetch(jnp.minimum(1, last), 1)

    def block(c):
        drain(c)
        parts = [_unpack_halves(p) for p in _load_token_tiles(bufs[c], 0, MOE_BLOCK)]
        lo = jnp.concatenate([p[0] for p in parts], axis=1).astype(BF16)
        hi = jnp.concatenate([p[1] for p in parts], axis=1).astype(BF16)
        fetch(jnp.minimum(b + 2, last), (c + 2) % GATHER_DEPTH)
        gate = (jnp.dot(lo, wg_ref[0, :HALF, :], preferred_element_type=F32)
                + jnp.dot(hi, wg_ref[0, HALF:, :], preferred_element_type=F32))
        up = (jnp.dot(lo, wu_ref[0, :HALF, :], preferred_element_type=F32)
              + jnp.dot(hi, wu_ref[0, HALF:, :], preferred_element_type=F32))
        hid = (gate * _sigmoid(gate) * up).astype(BF16)
        y = jnp.dot(hid, wd_ref[0], preferred_element_type=F32)
        _store_token_tiles(y_ref, _pack_halves(y))

        @pl.when(b == last)
        def _():
            drain((c + 1) % GATHER_DEPTH)
            drain((c + 2) % GATHER_DEPTH)

    for c in range(GATHER_DEPTH):
        pl.when(jnp.logical_and(b <= last, lax.rem(b, GATHER_DEPTH) == c))(functools.partial(block, c))

    @pl.when(b > last)
    def _():
        y_ref[...] = jnp.zeros_like(y_ref)


def _moe(blk_expert, n_active, tok_buf, h2p, wg, wu, wd):
    n_blocks = blk_expert.shape[0]
    buf_rows = MOE_BLOCK * TOKEN_TILE
    grid_spec = pltpu.PrefetchScalarGridSpec(
        num_scalar_prefetch=3,
        grid=(n_blocks,),
        in_specs=[pl.BlockSpec(memory_space=pl.ANY),
                  pl.BlockSpec((1, D_MODEL, D_EXPERT), lambda b, be, na, tk: (be[b], 0, 0)),
                  pl.BlockSpec((1, D_MODEL, D_EXPERT), lambda b, be, na, tk: (be[b], 0, 0)),
                  pl.BlockSpec((1, D_EXPERT, D_MODEL), lambda b, be, na, tk: (be[b], 0, 0))],
        out_specs=pl.BlockSpec((buf_rows, LANES), lambda b, be, na, tk: (b, 0)),
        scratch_shapes=[pltpu.VMEM((buf_rows, LANES), U32)] * GATHER_DEPTH
        + [pltpu.SemaphoreType.DMA((GATHER_DEPTH,))])
    return pl.pallas_call(
        _moe_kernel,
        grid_spec=grid_spec,
        out_shape=jax.ShapeDtypeStruct((n_blocks * buf_rows, LANES), U32),
        compiler_params=pltpu.CompilerParams(dimension_semantics=("arbitrary",),
                                             vmem_limit_bytes=VMEM_LIMIT),
    )(blk_expert, n_active, tok_buf, h2p, wg, wu, wd)


def _combine_kernel(dest_ref, y_hbm, x1_ref, wt_ref, g2_ref, lng_ref, lnb_ref, o_ref,
                    ybuf0, ybuf1, ybuf2, sem):
    i = pl.program_id(0)
    nt = pl.num_programs(0)
    tm = x1_ref.shape[0]
    count = TOP_K * tm
    bufs = (ybuf0, ybuf1, ybuf2)

    def fetch(tile_idx, c):
        _tile_gather_start(dest_ref, tile_idx * count, y_hbm, bufs[c], 0, sem.at[c], count)

    def drain(c):
        _tile_gather_wait(y_hbm, bufs[c], 0, sem.at[c], count)

    @pl.when(i == 0)
    def _():
        fetch(0, 0)
        fetch(jnp.minimum(1, nt - 1), 1)

    def tile(c):
        drain(c)
        first = [_unpack_halves(p) for p in _load_token_tiles(bufs[c], 0, tm)]
        second = [_unpack_halves(p) for p in _load_token_tiles(bufs[c], tm * TOKEN_TILE, tm)]
        fetch(jnp.minimum(i + 2, nt - 1), (c + 2) % GATHER_DEPTH)
        w1 = wt_ref[:, 0:1]
        w2 = wt_ref[:, 1:2]
        lo = [w1 * a[0] + w2 * b[0] for a, b in zip(first, second)]
        hi = [w1 * a[1] + w2 * b[1] for a, b in zip(first, second)]
        y = jnp.concatenate(lo + hi, axis=1)
        r = ALPHA * x1_ref[...] + (1.0 + g2_ref[0]) * y
        o_ref[...] = _layer_norm(r, lng_ref[...], lnb_ref[...])

        @pl.when(i == nt - 1)
        def _():
            drain((c + 1) % GATHER_DEPTH)
            drain((c + 2) % GATHER_DEPTH)

    for c in range(GATHER_DEPTH):
        pl.when(lax.rem(i, GATHER_DEPTH) == c)(functools.partial(tile, c))


def _combine(dest_tiled, y_buf, x1, wt, g2, ln_g, ln_b, seq, tm):
    n = x1.shape[0]
    per_batch = seq // tm
    grid_spec = pltpu.PrefetchScalarGridSpec(
        num_scalar_prefetch=1,
        grid=(n // tm,),
        in_specs=[pl.BlockSpec(memory_space=pl.ANY),
                  pl.BlockSpec((tm, D_MODEL), lambda i, d: (i, 0)),
                  pl.BlockSpec((tm, LANES), lambda i, d: (i, 0)),
                  pl.BlockSpec((1, 1, D_MODEL), lambda i, d: (i // per_batch, 0, 0)),
                  pl.BlockSpec((1, D_MODEL), lambda i, d: (0, 0)),
                  pl.BlockSpec((1, D_MODEL), lambda i, d: (0, 0))],
        out_specs=pl.BlockSpec((tm, D_MODEL), lambda i, d: (i, 0)),
        scratch_shapes=[pltpu.VMEM((TOP_K * tm * TOKEN_TILE, LANES), U32)] * GATHER_DEPTH
        + [pltpu.SemaphoreType.DMA((GATHER_DEPTH,))])
    return pl.pallas_call(
        _combine_kernel,
        grid_spec=grid_spec,
        out_shape=jax.ShapeDtypeStruct((n, D_MODEL), F32),
        compiler_params=pltpu.CompilerParams(dimension_semantics=("arbitrary",),
                                             vmem_limit_bytes=VMEM_LIMIT),
    )(dest_tiled, y_buf, x1, wt, g2, ln_g.reshape(1, D_MODEL), ln_b.reshape(1, D_MODEL))


def _dispatch_plan(route, counts, n):
    m = n * TOP_K
    n_blocks = -(-m // MOE_BLOCK) + N_EXPERTS
    expert = route[0:2].astype(jnp.int32)
    rank = route[4:6].astype(jnp.int32)
    sizes = counts[:, 0].astype(jnp.int32)
    padded = (sizes + MOE_BLOCK - 1) // MOE_BLOCK * MOE_BLOCK
    p_end = jnp.cumsum(padded)
    p_start = p_end - padded
    ids = jnp.arange(N_EXPERTS, dtype=jnp.int32)
    start_of = jnp.sum(jnp.where(expert[..., None] == ids, p_start, 0), axis=-1)
    dest = start_of + rank
    blk_start = jnp.arange(n_blocks, dtype=jnp.int32) * MOE_BLOCK
    blk_expert = jnp.minimum(jnp.sum((p_end[None, :] <= blk_start[:, None]).astype(jnp.int32), axis=1),
                             N_EXPERTS - 1)
    n_active = (p_end[-1] // MOE_BLOCK).astype(jnp.int32).reshape(1)
    pad_lo = p_start + sizes
    pad_hi = p_end.at[N_EXPERTS - 1].set(n_blocks * MOE_BLOCK)
    return dest, blk_expert, n_active, n_blocks, pad_lo, pad_hi


def kernel(x, c, w_ada, b_ada, w_in, dn_conv_w, dn_a_log, dn_dt_bias, dn_norm_w, fox_f_bias, w_out,
           ln1_g, ln1_b, w_router_group, b_router_group, w_router_expert, b_router_expert,
           w_gate, w_up, w_down, ln2_g, ln2_b):
    bsz, seq, _ = x.shape
    n = bsz * seq
    x2 = x.reshape(n, D_MODEL)

    mod = _ada(c, w_ada[0], b_ada[0])
    sh1, sc1, g1, sh2, sc2, g2 = [m.reshape(bsz, 1, D_MODEL) for m in jnp.split(mod, 6, axis=-1)]

    wi = w_in[0]
    o_b = 4 * DN_WIDTH
    o_a = o_b + DN_HEADS
    o_fq = o_a + DN_HEADS
    o_f = o_fq + 3 * FOX_WIDTH
    w_main = jnp.concatenate([wi[:, :o_b], wi[:, o_fq:o_f]], axis=1).astype(BF16)
    w_small = jnp.concatenate([wi[:, o_b:o_fq], wi[:, o_f:],
                               jnp.zeros((D_MODEL, LANES - 2 * DN_HEADS - FOX_HEADS), F32)],
                              axis=1).astype(BF16)
    zeros8 = jnp.zeros((DN_HEADS,), F32)
    pad = jnp.zeros((LANES - 2 * DN_HEADS - FOX_HEADS,), F32)
    gate_par = jnp.stack([jnp.concatenate([zeros8, dn_dt_bias[0], fox_f_bias[0], pad]),
                          jnp.concatenate([zeros8, dn_a_log[0], zeros8, pad])])

    main, gates = _inproj(x2, sc1, sh1, w_main, w_small, gate_par, seq)
    main3 = main.reshape(bsz, seq, D_MAIN)
    gates3 = gates.reshape(bsz, seq, LANES)
    gc3, fk_aug = _cumsum(gates3)
    gct = jnp.swapaxes(gc3[:, :, DN_HEADS:2 * DN_HEADS], 1, 2)

    o_dn = _deltanet(main3, gates3, gc3, gct, dn_conv_w[0], dn_norm_w[0])
    o_fox = _fox(main3, fk_aug)

    wo = w_out[0].astype(BF16)
    wr_t = jnp.concatenate([w_router_group[0].T, jnp.zeros((8 - N_GROUPS, D_MODEL), F32),
                            jnp.transpose(w_router_expert[0], (0, 2, 1)).reshape(N_EXPERTS, D_MODEL)],
                           axis=0)
    br_t = jnp.concatenate([b_router_group[0], jnp.zeros((8 - N_GROUPS,), F32),
                            b_router_expert[0].reshape(-1)])
    br_col = jnp.pad(br_t, (0, LANES - ROUTE_ROWS)).reshape(LANES, 1)
    wr_hi = wr_t.astype(BF16)
    wr_lo = (wr_t - wr_hi.astype(F32)).astype(BF16)
    lane_pad = ((0, 0), (0, LANES - ROUTE_ROWS))
    wr_split = jnp.concatenate([jnp.pad(wr_hi.T, lane_pad), jnp.pad(wr_lo.T, lane_pad)], axis=1)
    x1, h2p, route, counts = _outproj(o_dn.reshape(n, DN_WIDTH), o_fox.reshape(n, FOX_WIDTH),
                                      wo[:DN_WIDTH], wo[DN_WIDTH:], x2, g1, sc2, sh2, ln1_g[0], ln1_b[0],
                                      wr_split, br_col, seq)

    dest, blk_expert, n_active, n_blocks, pad_lo, pad_hi = _dispatch_plan(route, counts, n)
    tok_buf = _invert(dest.reshape(-1), pad_lo, pad_hi, n_blocks * MOE_BLOCK)
    y_buf = _moe(blk_expert, n_active, tok_buf, h2p,
                 w_gate[0].astype(BF16), w_up[0].astype(BF16), w_down[0].astype(BF16))

    tm = min(256, seq)
    dest_tiled = (dest * TOKEN_TILE).reshape(TOP_K, n // tm, tm).transpose(1, 0, 2).reshape(-1)
    wt = jnp.pad(route[2:4].T, ((0, 0), (0, LANES - TOP_K)))
    out = _combine(dest_tiled, y_buf, x1, wt, g2, ln2_g[0], ln2_b[0], seq, tm)
    return out.reshape(bsz, seq, D_MODEL)
```

```python
import functools
import math

import jax
import jax.numpy as jnp
from jax import lax
from jax.experimental import pallas as pl
from jax.experimental.pallas import tpu as pltpu

F32 = jnp.float32
BF16 = jnp.bfloat16
U32 = jnp.uint32
HIGHEST = lax.Precision.HIGHEST

D_MODEL = 2048
HEAD_DIM = 128
DN_HEADS = 8
FOX_HEADS = 8
DN_WIDTH = DN_HEADS * HEAD_DIM
FOX_WIDTH = FOX_HEADS * HEAD_DIM
CONV_WIDTH = 4
N_GROUPS = 4
EXPERTS_PER_GROUP = 8
N_EXPERTS = N_GROUPS * EXPERTS_PER_GROUP
TOP_K = 2
D_EXPERT = 512
MOE_BLOCK = 256
LN_EPS = 1e-5
NORM_EPS = 1e-6
ALPHA = 2.0 ** 0.25

D_MAIN = 3 * DN_WIDTH + DN_WIDTH + 3 * FOX_WIDTH
LANES = 128
BF16_SUBLANES = 16
DN_CHUNK = 128
NEG_BIG = -1e30
LOG2E = 1.4426950408889634
HALF = D_MODEL // 2
VMEM_LIMIT = 56 * 1024 * 1024


def _sigmoid(x):
    return 1.0 / (1.0 + jnp.exp(-x))


def _softplus(x):
    return jnp.maximum(x, 0.0) + jnp.log(1.0 + jnp.exp(-jnp.abs(x)))


def _mm(a, b):
    return jnp.dot(a.astype(BF16), b.astype(BF16), preferred_element_type=F32)


def _mm_nt(a, b):
    return lax.dot_general(a.astype(BF16), b.astype(BF16), (((1,), (1,)), ((), ())),
                           preferred_element_type=F32)


def _mm_tn(a, b):
    return lax.dot_general(a.astype(BF16), b.astype(BF16), (((0,), (0,)), ((), ())),
                           preferred_element_type=F32)


def _pack_halves(x):
    c = x.shape[1] // 2
    lo = pltpu.bitcast(x[:, :c].astype(BF16).astype(F32), U32)
    hi = pltpu.bitcast(x[:, c:].astype(BF16).astype(F32), U32)
    return (lo >> 16) | (hi & jnp.uint32(0xFFFF0000))


def _unpack_halves(p):
    lo = pltpu.bitcast(p << 16, F32)
    hi = pltpu.bitcast(p & jnp.uint32(0xFFFF0000), F32)
    return lo, hi


def _layer_norm(r, g, b):
    mu = jnp.mean(r, axis=-1, keepdims=True)
    d = r - mu
    var = jnp.mean(d * d, axis=-1, keepdims=True)
    return d * lax.rsqrt(var + LN_EPS) * g + b


def _ada_kernel(c_ref, w_ref, b_ref, o_ref):
    c = c_ref[...]
    a = c * _sigmoid(c)
    o_ref[...] = jnp.dot(a, w_ref[...], preferred_element_type=F32, precision=HIGHEST) + b_ref[...]


def _ada(c, w_ada, b_ada):
    bsz = c.shape[0]
    n_out = w_ada.shape[1]
    tn = 1024
    return pl.pallas_call(
        _ada_kernel,
        grid=(n_out // tn,),
        in_specs=[pl.BlockSpec((bsz, D_MODEL), lambda j: (0, 0)),
                  pl.BlockSpec((D_MODEL, tn), lambda j: (0, j)),
                  pl.BlockSpec((1, tn), lambda j: (0, j))],
        out_specs=pl.BlockSpec((bsz, tn), lambda j: (0, j)),
        out_shape=jax.ShapeDtypeStruct((bsz, n_out), F32),
        compiler_params=pltpu.CompilerParams(dimension_semantics=("arbitrary",),
                                             vmem_limit_bytes=VMEM_LIMIT),
    )(c, w_ada, b_ada.reshape(1, n_out))


def _inproj_kernel(x_ref, sc_ref, sh_ref, w_ref, ws_ref, par_ref, o_ref, g_ref, h_scr):
    @pl.when(pl.program_id(1) == 0)
    def _():
        h = (x_ref[...] * (1.0 + sc_ref[0]) + sh_ref[0]).astype(BF16)
        h_scr[...] = h
        s = jnp.dot(h, ws_ref[...], preferred_element_type=F32)
        lane = lax.broadcasted_iota(jnp.int32, s.shape, 1)
        sb = s + par_ref[0:1, :]
        neg_a = -jnp.exp(par_ref[1:2, :])
        beta = _sigmoid(s)
        g = neg_a * _softplus(sb)
        log_f = -_softplus(-sb)
        g_ref[...] = jnp.where(lane < DN_HEADS, beta,
                               jnp.where(lane < 2 * DN_HEADS, g,
                                         jnp.where(lane < 2 * DN_HEADS + FOX_HEADS, log_f, 0.0)))

    o_ref[...] = jnp.dot(h_scr[...], w_ref[...], preferred_element_type=F32).astype(BF16)


def _inproj(x2, sc1, sh1, w_main, w_small, gate_par, seq):
    n = x2.shape[0]
    tm = min(1024, seq)
    tn = 1024
    per_batch = seq // tm
    return pl.pallas_call(
        _inproj_kernel,
        grid=(n // tm, D_MAIN // tn),
        in_specs=[pl.BlockSpec((tm, D_MODEL), lambda i, j: (i, 0)),
                  pl.BlockSpec((1, 1, D_MODEL), lambda i, j: (i // per_batch, 0, 0)),
                  pl.BlockSpec((1, 1, D_MODEL), lambda i, j: (i // per_batch, 0, 0)),
                  pl.BlockSpec((D_MODEL, tn), lambda i, j: (0, j)),
                  pl.BlockSpec((D_MODEL, LANES), lambda i, j: (0, 0)),
                  pl.BlockSpec((2, LANES), lambda i, j: (0, 0))],
        out_specs=[pl.BlockSpec((tm, tn), lambda i, j: (i, j)),
                   pl.BlockSpec((tm, LANES), lambda i, j: (i, 0))],
        out_shape=[jax.ShapeDtypeStruct((n, D_MAIN), BF16),
                   jax.ShapeDtypeStruct((n, LANES), F32)],
        scratch_shapes=[pltpu.VMEM((tm, D_MODEL), BF16)],
        compiler_params=pltpu.CompilerParams(dimension_semantics=("arbitrary", "arbitrary"),
                                             vmem_limit_bytes=VMEM_LIMIT),
    )(x2, sc1, sh1, w_main, w_small, gate_par)


def _cumsum_kernel(g_ref, gc_ref, fk_ref, carry, tri_scr):
    first = jnp.logical_and(pl.program_id(0) == 0, pl.program_id(1) == 0)

    @pl.when(pl.program_id(1) == 0)
    def _():
        carry[...] = jnp.zeros_like(carry)

    x = g_ref[0]
    t = x.shape[0]

    @pl.when(first)
    def _():
        row = lax.broadcasted_iota(jnp.int32, (t, t), 0)
        col = lax.broadcasted_iota(jnp.int32, (t, t), 1)
        shift = int(math.log2(DN_CHUNK))
        tri_scr[:t, :] = jnp.where(col <= row, 1.0, 0.0).astype(BF16)
        tri_scr[t:, :] = jnp.where(jnp.logical_and(col <= row, (row >> shift) == (col >> shift)),
                                   1.0, 0.0).astype(BF16)

    hi = x.astype(BF16)
    r1 = x - hi.astype(F32)
    mid = r1.astype(BF16)
    lo = (r1 - mid.astype(F32)).astype(BF16)
    parts = jnp.dot(tri_scr[...], jnp.concatenate([hi, mid, lo], axis=1), preferred_element_type=F32)
    sums = parts[:, :LANES] + parts[:, LANES:2 * LANES] + parts[:, 2 * LANES:]
    cs = sums[:t]
    f = cs + carry[...]
    gc_ref[0] = sums[t:]
    carry[...] = carry[...] + cs[t - 1:t, :]

    lane = lax.broadcasted_iota(jnp.int32, (t, LANES), 1)
    for h in range(FOX_HEADS):
        c0 = 2 * DN_HEADS + h
        val = jnp.broadcast_to(f[:, c0:c0 + 1] * (-LOG2E), (t, LANES))
        hi = val.astype(BF16)
        r1 = val - hi.astype(F32)
        mid = r1.astype(BF16)
        lo = r1 - mid.astype(F32)
        fk_ref[0, :, h * LANES:(h + 1) * LANES] = jnp.where(
            lane == 0, hi.astype(F32),
            jnp.where(lane == 1, mid.astype(F32), jnp.where(lane == 2, lo, 0.0))).astype(BF16)


def _cumsum(gates3):
    bsz, seq, _ = gates3.shape
    t = min(512, seq)
    spec = pl.BlockSpec((1, t, LANES), lambda b, i: (b, i, 0))
    return pl.pallas_call(
        _cumsum_kernel,
        grid=(bsz, seq // t),
        in_specs=[spec],
        out_specs=[spec, pl.BlockSpec((1, t, FOX_HEADS * LANES), lambda b, i: (b, i, 0))],
        out_shape=[jax.ShapeDtypeStruct(gates3.shape, F32),
                   jax.ShapeDtypeStruct((bsz, seq, FOX_HEADS * LANES), BF16)],
        scratch_shapes=[pltpu.VMEM((1, LANES), F32), pltpu.VMEM((2 * t, t), BF16)],
        compiler_params=pltpu.CompilerParams(dimension_semantics=("arbitrary", "arbitrary")),
    )(gates3)


DN_CHUNKS_PER_STEP = 4


def _dn_kernel(qkv_ref, halo_ref, z_ref, gate_ref, gc_ref, gct_ref, convw_ref, normw_ref,
               o_ref, state_ref, xs_scr):
    t = pl.program_id(1)

    @pl.when(t == 0)
    def _():
        state_ref[...] = jnp.zeros_like(state_ref)

    c = DN_CHUNK
    d = HEAD_DIM
    heads = range(DN_HEADS)
    row = lax.broadcasted_iota(jnp.int32, (c, c), 0)
    col = lax.broadcasted_iota(jnp.int32, (c, c), 1)
    strict = col < row
    eye = (row == col).astype(F32)
    neg_diag8_f = jnp.where((row >> 3) == (col >> 3), -1.0, 0.0)
    merge_levels = range(3, int(math.log2(c)))
    low_f = [jnp.where((row >> (s + 1)) == (col >> (s + 1)),
                       jnp.where((row >> s) == (col >> s), 0.0, 1.0), 0.0) for s in merge_levels]
    halo_on = (t > 0).astype(F32)
    normw = normw_ref[...]

    def l2n(x):
        return x * lax.rsqrt(jnp.sum(x * x, axis=-1, keepdims=True) + NORM_EPS)

    def bdot(a, b):
        return jnp.dot(a, b, preferred_element_type=F32)

    def conv_head(ci, h):
        r0 = ci * c

        def conv_silu(part):
            off = part * DN_WIDTH + h * d
            slab = (ci * 3 + part) * DN_HEADS + h
            x = qkv_ref[0, r0:r0 + c, off:off + d].astype(F32)
            if ci == 0:
                prev = halo_ref[0, BF16_SUBLANES - 8:, off:off + d].astype(F32) * halo_on
            else:
                prev = qkv_ref[0, r0 - 8:r0, off:off + d].astype(F32)
            xs_scr[slab, 0:8, :] = prev
            xs_scr[slab, 8:, :] = x
            w = convw_ref[:, off:off + d]
            acc = x * w[CONV_WIDTH - 1:CONV_WIDTH]
            for k in range(1, CONV_WIDTH):
                acc = acc + xs_scr[slab, pl.ds(8 - k, c), :] * w[CONV_WIDTH - 1 - k:CONV_WIDTH - k]
            return acc * _sigmoid(acc)

        return l2n(conv_silu(0)) * (d ** -0.5), l2n(conv_silu(1)), conv_silu(2)

    def chunk_steps(ci, qkv):
        r0 = ci * c
        q = [x[0] for x in qkv]
        k = [x[1] for x in qkv]
        v = [x[2] for x in qkv]
        gates = gate_ref[0, r0:r0 + c, :]
        gcs = gc_ref[0, r0:r0 + c, :]
        gct = gct_ref[0, :, r0:r0 + c]
        beta = [gates[:, h:h + 1] for h in heads]
        g_col = [gcs[:, DN_HEADS + h:DN_HEADS + h + 1] for h in heads]
        e_g = [jnp.exp(g) for g in g_col]
        g_last = [g[c - 1:c, :] for g in g_col]
        decay = [jnp.exp(jnp.where(strict, g_col[h] - gct[h:h + 1, :], NEG_BIG)) for h in heads]
        kb = [k[h] * beta[h] for h in heads]

        kk = [_mm_nt(jnp.concatenate([kb[h], q[h]], axis=0), k[h]) for h in heads]
        yield
        a = [kk[h][:c] * decay[h] for h in heads]
        qk = [(kk[h][c:] * (decay[h] + eye)).astype(BF16) for h in heads]
        n1 = [a[h] * neg_diag8_f for h in heads]
        n1b = [n.astype(BF16) for n in n1]
        x = [eye + n for n in n1]
        xb = [xx.astype(BF16) for xx in x]
        n2b = [bdot(n, n).astype(BF16) for n in n1b]
        yield
        x = [x[h] + bdot(n2b[h], xb[h]) for h in heads]
        n4b = [bdot(n, n).astype(BF16) for n in n2b]
        yield
        xb = [xx.astype(BF16) for xx in x]
        x = [x[h] + bdot(n4b[h], xb[h]) for h in heads]
        yield
        for lf in low_f:
            xb = [xx.astype(BF16) for xx in x]
            lx = [bdot((a[h] * lf).astype(BF16), xb[h]).astype(BF16) for h in heads]
            yield
            x = [x[h] - bdot(xb[h], lx[h]) for h in heads]
            yield

        uw = [_mm(x[h], jnp.concatenate([v[h] * beta[h], kb[h] * e_g[h]], axis=1)) for h in heads]
        yield
        s_old = [state_ref[h] for h in heads]
        r = [_mm(jnp.concatenate([uw[h][:, d:], q[h] * e_g[h]], axis=0), s_old[h]) for h in heads]
        yield
        v_new = [(uw[h][:, :d] - r[h][:c]).astype(BF16) for h in heads]
        o = [r[h][c:] + bdot(qk[h], v_new[h]) for h in heads]
        for h in heads:
            kd = k[h] * jnp.exp(g_last[h] - g_col[h])
            state_ref[h] = s_old[h] * jnp.exp(g_last[h]) + _mm_tn(kd, v_new[h])
        yield
        for h in heads:
            z = z_ref[0, r0:r0 + c, h * d:(h + 1) * d].astype(F32)
            rms = lax.rsqrt(jnp.mean(o[h] * o[h], axis=-1, keepdims=True) + NORM_EPS)
            o_ref[0, r0:r0 + c, h * d:(h + 1) * d] = (o[h] * rms * normw * (z * _sigmoid(z))).astype(BF16)

    qkv = [conv_head(0, h) for h in heads]
    for ci in range(DN_CHUNKS_PER_STEP):
        following = []
        more = ci + 1 < DN_CHUNKS_PER_STEP
        for _ in chunk_steps(ci, qkv):
            if more and len(following) < DN_HEADS:
                following.append(conv_head(ci + 1, len(following)))
        while more and len(following) < DN_HEADS:
            following.append(conv_head(ci + 1, len(following)))
        qkv = following


def _deltanet(main3, gates3, gc3, gct, conv_w, norm_w):
    bsz, seq, _ = main3.shape
    rows = DN_CHUNK * DN_CHUNKS_PER_STEP
    halo_blocks = rows // BF16_SUBLANES
    return pl.pallas_call(
        _dn_kernel,
        grid=(bsz, seq // rows),
        in_specs=[pl.BlockSpec((1, rows, 3 * DN_WIDTH), lambda b, t: (b, t, 0)),
                  pl.BlockSpec((1, BF16_SUBLANES, 3 * DN_WIDTH),
                               lambda b, t: (b, jnp.maximum(t * halo_blocks - 1, 0), 0)),
                  pl.BlockSpec((1, rows, DN_WIDTH), lambda b, t: (b, t, 3)),
                  pl.BlockSpec((1, rows, LANES), lambda b, t: (b, t, 0)),
                  pl.BlockSpec((1, rows, LANES), lambda b, t: (b, t, 0)),
                  pl.BlockSpec((1, DN_HEADS, rows), lambda b, t: (b, 0, t)),
                  pl.BlockSpec((CONV_WIDTH, 3 * DN_WIDTH), lambda b, t: (0, 0)),
                  pl.BlockSpec((1, HEAD_DIM), lambda b, t: (0, 0))],
        out_specs=pl.BlockSpec((1, rows, DN_WIDTH), lambda b, t: (b, t, 0)),
        out_shape=jax.ShapeDtypeStruct((bsz, seq, DN_WIDTH), BF16),
        scratch_shapes=[pltpu.VMEM((DN_HEADS, HEAD_DIM, HEAD_DIM), F32),
                        pltpu.VMEM((3 * DN_HEADS * DN_CHUNKS_PER_STEP, 8 + DN_CHUNK, HEAD_DIM), F32)],
        compiler_params=pltpu.CompilerParams(dimension_semantics=("arbitrary", "arbitrary"),
                                             vmem_limit_bytes=VMEM_LIMIT),
    )(main3, main3, main3, gates3, gc3, gct, conv_w, norm_w.reshape(1, HEAD_DIM))


def _fox_kernel(qi_ref, kj_ref, q_ref, k_ref, v_ref, fk_ref, o_ref, qa_scr, m_scr, l_scr, acc_scr):
    step = pl.program_id(1)
    i = qi_ref[step]
    j = kj_ref[step]
    d = HEAD_DIM
    tq = q_ref.shape[1]
    tk = k_ref.shape[1]

    @pl.when(j == 0)
    def _():
        m_scr[...] = jnp.full_like(m_scr, NEG_BIG)
        l_scr[...] = jnp.zeros_like(l_scr)
        acc_scr[...] = jnp.zeros_like(acc_scr)
        lane = lax.broadcasted_iota(jnp.int32, (tq, LANES), 1)
        ones = jnp.where(lane < 3, 1.0, 0.0).astype(BF16)
        for h in range(FOX_HEADS):
            q = (q_ref[0, :, h * d:(h + 1) * d].astype(F32) * (d ** -0.5 * LOG2E)).astype(BF16)
            qa_scr[h] = jnp.concatenate([q, ones], axis=1)

    def update(diagonal):
        if diagonal:
            key_pos = lax.broadcasted_iota(jnp.int32, (tk, tq), 0)
            q_pos = lax.broadcasted_iota(jnp.int32, (tk, tq), 1)
            visible = key_pos <= q_pos

        def scores(h):
            k_aug = jnp.concatenate([k_ref[0, :, h * d:(h + 1) * d],
                                     fk_ref[0, :, h * LANES:(h + 1) * LANES]], axis=1)
            return lax.dot_general(k_aug, qa_scr[h], (((1,), (1,)), ((), ())),
                                   preferred_element_type=F32)

        pending = [scores(0), scores(1)]
        for h in range(FOX_HEADS):
            s = pending.pop(0)
            if h + 2 < FOX_HEADS:
                pending.append(scores(h + 2))
            v = v_ref[0, :, h * d:(h + 1) * d]
            if diagonal:
                s = jnp.where(visible, s, NEG_BIG)
            m_old = m_scr[h]
            m_new = jnp.maximum(m_old, jnp.max(s, axis=0, keepdims=True))
            alpha = jnp.exp2(m_old - m_new)
            p = jnp.exp2(s - m_new)
            l_scr[h] = alpha * l_scr[h] + jnp.sum(p, axis=0, keepdims=True)
            acc_scr[h] = alpha * acc_scr[h] + _mm_tn(v, p)
            m_scr[h] = m_new

    @pl.when(j < i)
    def _():
        update(False)

    @pl.when(j == i)
    def _():
        update(True)
        for h in range(FOX_HEADS):
            o = acc_scr[h] / l_scr[h]
            o_ref[0, :, h * d:(h + 1) * d] = o.T.astype(BF16)


def _fox(main3, fk_aug):
    bsz, seq, _ = main3.shape
    tq = min(512, seq)
    n_blk = seq // tq
    w = FOX_WIDTH
    base = (4 * DN_WIDTH) // w
    pairs = [(i, j) for i in range(n_blk) for j in range(i + 1)]
    qi = jnp.array([p[0] for p in pairs], jnp.int32)
    kj = jnp.array([p[1] for p in pairs], jnp.int32)
    grid_spec = pltpu.PrefetchScalarGridSpec(
        num_scalar_prefetch=2,
        grid=(bsz, len(pairs)),
        in_specs=[pl.BlockSpec((1, tq, w), lambda b, t, qi, kj: (b, qi[t], base)),
                  pl.BlockSpec((1, tq, w), lambda b, t, qi, kj: (b, kj[t], base + 1)),
                  pl.BlockSpec((1, tq, w), lambda b, t, qi, kj: (b, kj[t], base + 2)),
                  pl.BlockSpec((1, tq, FOX_HEADS * LANES), lambda b, t, qi, kj: (b, kj[t], 0))],
        out_specs=pl.BlockSpec((1, tq, w), lambda b, t, qi, kj: (b, qi[t], 0)),
        scratch_shapes=[pltpu.VMEM((FOX_HEADS, tq, 2 * HEAD_DIM), BF16),
                        pltpu.VMEM((FOX_HEADS, 1, tq), F32),
                        pltpu.VMEM((FOX_HEADS, 1, tq), F32),
                        pltpu.VMEM((FOX_HEADS, HEAD_DIM, tq), F32)])
    return pl.pallas_call(
        _fox_kernel,
        grid_spec=grid_spec,
        out_shape=jax.ShapeDtypeStruct((bsz, seq, w), BF16),
        compiler_params=pltpu.CompilerParams(
            dimension_semantics=("arbitrary", "arbitrary"),
            vmem_limit_bytes=VMEM_LIMIT),
    )(qi, kj, main3, main3, main3, fk_aug)


ROUTE_ROWS = 8 + N_EXPERTS
TOKEN_TILE = 8


OUTPROJ_SUBTILES = 1


def _store_token_tiles(ref, packed, first_token=0):
    m = packed.shape[0]
    for s in range(TOKEN_TILE):
        ref[pl.ds(first_token * TOKEN_TILE + s, m, stride=TOKEN_TILE), :] = packed[:, s * LANES:(s + 1) * LANES]


def _load_token_tiles(ref, start, m):
    return [ref[pl.ds(start + s, m, stride=TOKEN_TILE), :] for s in range(TOKEN_TILE)]


def _outproj_kernel(odn_ref, ofox_ref, w1_ref, w2_ref, x_ref, g1_ref, sc2_ref, sh2_ref,
                    lng_ref, lnb_ref, wr_ref, br_ref, x1_ref, h2p_ref, route_ref, count_ref,
                    upper_scr, carry_scr):
    tm = x_ref.shape[0]

    @pl.when(pl.program_id(0) == 0)
    def _():
        r = lax.broadcasted_iota(jnp.int32, (tm, tm), 0)
        c = lax.broadcasted_iota(jnp.int32, (tm, tm), 1)
        upper_scr[...] = jnp.where(r < c, 1.0, 0.0).astype(BF16)
        carry_scr[...] = jnp.zeros_like(carry_scr)

    sub = tm // OUTPROJ_SUBTILES
    ys = []
    for u in range(OUTPROJ_SUBTILES):
        rows = slice(u * sub, (u + 1) * sub)
        ys.append(jnp.dot(odn_ref[rows, :], w1_ref[...], preferred_element_type=F32)
                  + jnp.dot(ofox_ref[rows, :], w2_ref[...], preferred_element_type=F32))
    for u in range(OUTPROJ_SUBTILES):
        _outproj_tail(u, sub, ys[u], x_ref, g1_ref, sc2_ref, sh2_ref, lng_ref, lnb_ref, wr_ref, br_ref,
                      x1_ref, h2p_ref, route_ref, upper_scr, carry_scr)
    count_ref[...] = jnp.broadcast_to(carry_scr[...], count_ref.shape)


def _outproj_tail(u, tm, y, x_ref, g1_ref, sc2_ref, sh2_ref, lng_ref, lnb_ref, wr_ref, br_ref,
                  x1_ref, h2p_ref, route_ref, upper_scr, carry_scr):
    rows = slice(u * tm, (u + 1) * tm)
    r = ALPHA * x_ref[rows, :] + (1.0 + g1_ref[0]) * y
    x1 = _layer_norm(r, lng_ref[...], lnb_ref[...])
    x1_ref[rows, :] = x1
    h2 = x1 * (1.0 + sc2_ref[0]) + sh2_ref[0]
    _store_token_tiles(h2p_ref, _pack_halves(h2), u * tm)

    h_hi = h2.astype(BF16)
    h_lo = (h2 - h_hi.astype(F32)).astype(BF16)
    parts = jnp.dot(jnp.concatenate([h_hi, h_lo], axis=0), wr_ref[...], preferred_element_type=F32)
    terms = parts[:tm] + parts[tm:]
    lt = (terms[:, :LANES] + terms[:, LANES:]).T + br_ref[...]
    row = [lt[n:n + 1, :] for n in range(ROUTE_ROWS)]

    def first_max(vals):
        best = vals[0]
        for val in vals[1:]:
            best = jnp.maximum(best, val)
        idx = jnp.full(best.shape, len(vals) - 1, jnp.int32)
        for n in range(len(vals) - 2, -1, -1):
            idx = jnp.where(vals[n] == best, n, idx)
        return best, idx

    gmax, gidx = first_max(row[:N_GROUPS])
    gsum = jnp.exp(row[0] - gmax)
    for n in range(1, N_GROUPS):
        gsum = gsum + jnp.exp(row[n] - gmax)
    g_w = 1.0 / gsum
    el = []
    for e in range(EXPERTS_PER_GROUP):
        val = row[8 + e]
        for g in range(1, N_GROUPS):
            val = jnp.where(gidx == g, row[8 + EXPERTS_PER_GROUP * g + e], val)
        el.append(val)
    emax, _ = first_max(el)
    ee = [jnp.exp(val - emax) for val in el]
    esum = ee[0]
    for val in ee[1:]:
        esum = esum + val
    prob = [val / esum for val in ee]
    p1, e1 = first_max(prob)
    p2, e2 = first_max([jnp.where(e1 == e, -1.0, prob[e]) for e in range(EXPERTS_PER_GROUP)])
    denom = p1 + p2
    ex1 = gidx * EXPERTS_PER_GROUP + e1
    ex2 = gidx * EXPERTS_PER_GROUP + e2
    ridx = lax.broadcasted_iota(jnp.int32, (8, tm), 0)

    eidx = lax.broadcasted_iota(jnp.int32, (N_EXPERTS, tm), 0)
    oh1 = jnp.where(eidx == ex1, 1.0, 0.0)
    oh2 = jnp.where(eidx == ex2, 1.0, 0.0)
    both = oh1 + oh2
    before = jnp.dot(both.astype(BF16), upper_scr[:tm, :tm], preferred_element_type=F32) + carry_scr[...]
    rank1 = jnp.sum(oh1 * before, axis=0, keepdims=True)
    rank2 = jnp.sum(oh2 * before, axis=0, keepdims=True)
    carry_scr[...] = carry_scr[...] + jnp.sum(both, axis=1, keepdims=True)

    fields = [ex1.astype(F32), ex2.astype(F32), g_w * p1 / denom, g_w * p2 / denom, rank1, rank2]
    out = jnp.zeros((8, tm), F32)
    for n, val in enumerate(fields):
        out = jnp.where(ridx == n, val, out)
    route_ref[:, rows] = out


def _outproj(o_dn, o_fox, w1, w2, x2, g1, sc2, sh2, ln_g, ln_b, wr_t, br_t, seq):
    n = x2.shape[0]
    tm = min(512, seq)
    per_batch = seq // tm
    row = lambda i: (i, 0)
    fixed = lambda i: (0, 0)
    mod = pl.BlockSpec((1, 1, D_MODEL), lambda i: (i // per_batch, 0, 0))
    return pl.pallas_call(
        _outproj_kernel,
        grid=(n // tm,),
        in_specs=[pl.BlockSpec((tm, DN_WIDTH), row),
                  pl.BlockSpec((tm, FOX_WIDTH), row),
                  pl.BlockSpec((DN_WIDTH, D_MODEL), fixed),
                  pl.BlockSpec((FOX_WIDTH, D_MODEL), fixed),
                  pl.BlockSpec((tm, D_MODEL), row),
                  mod, mod, mod,
                  pl.BlockSpec((1, D_MODEL), fixed),
                  pl.BlockSpec((1, D_MODEL), fixed),
                  pl.BlockSpec((D_MODEL, 2 * LANES), fixed),
                  pl.BlockSpec((LANES, 1), fixed)],
        out_specs=[pl.BlockSpec((tm, D_MODEL), row),
                   pl.BlockSpec((tm * TOKEN_TILE, LANES), row),
                   pl.BlockSpec((8, tm), lambda i: (0, i)),
                   pl.BlockSpec((N_EXPERTS, LANES), fixed)],
        out_shape=[jax.ShapeDtypeStruct((n, D_MODEL), F32),
                   jax.ShapeDtypeStruct((n * TOKEN_TILE, LANES), U32),
                   jax.ShapeDtypeStruct((8, n), F32),
                   jax.ShapeDtypeStruct((N_EXPERTS, LANES), F32)],
        scratch_shapes=[pltpu.VMEM((tm, tm), BF16),
                        pltpu.VMEM((N_EXPERTS, 1), F32)],
        compiler_params=pltpu.CompilerParams(dimension_semantics=("arbitrary",),
                                             vmem_limit_bytes=VMEM_LIMIT),
    )(o_dn, o_fox, w1, w2, x2, g1, sc2, sh2, ln_g.reshape(1, D_MODEL), ln_b.reshape(1, D_MODEL),
      wr_t, br_t)


INVERT_STEPS = N_EXPERTS


def _invert_kernel(dest_ref, pad_lo_ref, pad_hi_ref, tok_ref):
    phase = pl.program_id(0)
    i = pl.program_id(1)
    n_tok = dest_ref.shape[0] // TOP_K
    toks_per = n_tok // INVERT_STEPS

    @pl.when(phase == 0)
    def _():
        def clear(p, carry):
            tok_ref[p] = 0
            return carry
        lax.fori_loop(pad_lo_ref[i], pad_hi_ref[i], clear, 0)

    @pl.when(phase == 1)
    def _():
        def put(t, carry):
            tok = i * toks_per + t
            for k in range(TOP_K):
                tok_ref[dest_ref[k * n_tok + tok]] = tok * TOKEN_TILE
            return carry
        lax.fori_loop(0, toks_per, put, 0, unroll=8)


def _invert(dest_flat, pad_lo, pad_hi, n_slots):
    assert (dest_flat.shape[0] // TOP_K) % INVERT_STEPS == 0
    smem = pl.BlockSpec(memory_space=pltpu.SMEM)
    return pl.pallas_call(
        _invert_kernel,
        grid=(2, INVERT_STEPS),
        in_specs=[smem, smem, smem],
        out_specs=smem,
        out_shape=jax.ShapeDtypeStruct((n_slots,), jnp.int32),
        compiler_params=pltpu.CompilerParams(dimension_semantics=("arbitrary", "arbitrary")),
    )(dest_flat, pad_lo, pad_hi)


def _tile_gather_start(idx_ref, base, src_hbm, dst, dst_row0, sem, count):
    for r in range(count):
        row = pl.multiple_of(idx_ref[base + r], TOKEN_TILE)
        pltpu.make_async_copy(src_hbm.at[pl.ds(row, TOKEN_TILE)],
                              dst.at[pl.ds(dst_row0 + r * TOKEN_TILE, TOKEN_TILE)],
                              sem).start(priority=r % 2)


def _tile_gather_wait(src_hbm, dst, dst_row0, sem, count):
    rows = count * TOKEN_TILE
    pltpu.make_async_copy(src_hbm.at[pl.ds(0, rows)], dst.at[pl.ds(dst_row0, rows)], sem).wait()


GATHER_DEPTH = 3


def _moe_kernel(blk_expert_ref, n_active_ref, tok_ref, h_hbm, wg_ref, wu_ref, wd_ref,
                y_ref, xbuf0, xbuf1, xbuf2, sem):
    b = pl.program_id(0)
    nb = pl.num_programs(0)
    n_active = n_active_ref[0]
    last = jnp.maximum(jnp.minimum(n_active, nb), 1) - 1
    bufs = (xbuf0, xbuf1, xbuf2)

    def fetch(blk, c):
        _tile_gather_start(tok_ref, blk * MOE_BLOCK, h_hbm, bufs[c], 0, sem.at[c], MOE_BLOCK)

    def drain(c):
        _tile_gather_wait(h_hbm, bufs[c], 0, sem.at[c], MOE_BLOCK)

    @pl.when(b == 0)
    def _():
        fetch(0, 0)
        fetch(jnp.minimum(1, last), 1)

    def block(c):
        drain(c)
        parts = [_unpack_halves(p) for p in _load_token_tiles(bufs[c], 0, MOE_BLOCK)]
        lo = jnp.concatenate([p[0] for p in parts], axis=1).astype(BF16)
        hi = jnp.concatenate([p[1] for p in parts], axis=1).astype(BF16)
        fetch(jnp.minimum(b + 2, last), (c + 2) % GATHER_DEPTH)
        gate = (jnp.dot(lo, wg_ref[0, :HALF, :], preferred_element_type=F32)
                + jnp.dot(hi, wg_ref[0, HALF:, :], preferred_element_type=F32))
        up = (jnp.dot(lo, wu_ref[0, :HALF, :], preferred_element_type=F32)
              + jnp.dot(hi, wu_ref[0, HALF:, :], preferred_element_type=F32))
        hid = (gate * _sigmoid(gate) * up).astype(BF16)
        y = jnp.dot(hid, wd_ref[0], preferred_element_type=F32)
        _store_token_tiles(y_ref, _pack_halves(y))

        @pl.when(b == last)
        def _():
            drain((c + 1) % GATHER_DEPTH)
            drain((c + 2) % GATHER_DEPTH)

    for c in range(GATHER_DEPTH):
        pl.when(jnp.logical_and(b <= last, lax.rem(b, GATHER_DEPTH) == c))(functools.partial(block, c))

    @pl.when(b > last)
    def _():
        y_ref[...] = jnp.zeros_like(y_ref)


def _moe(blk_expert, n_active, tok_buf, h2p, wg, wu, wd):
    n_blocks = blk_expert.shape[0]
    buf_rows = MOE_BLOCK * TOKEN_TILE
    grid_spec = pltpu.PrefetchScalarGridSpec(
        num_scalar_prefetch=3,
        grid=(n_blocks,),
        in_specs=[pl.BlockSpec(memory_space=pl.ANY),
                  pl.BlockSpec((1, D_MODEL, D_EXPERT), lambda b, be, na, tk: (be[b], 0, 0)),
                  pl.BlockSpec((1, D_MODEL, D_EXPERT), lambda b, be, na, tk: (be[b], 0, 0)),
                  pl.BlockSpec((1, D_EXPERT, D_MODEL), lambda b, be, na, tk: (be[b], 0, 0))],
        out_specs=pl.BlockSpec((buf_rows, LANES), lambda b, be, na, tk: (b, 0)),
        scratch_shapes=[pltpu.VMEM((buf_rows, LANES), U32)] * GATHER_DEPTH
        + [pltpu.SemaphoreType.DMA((GATHER_DEPTH,))])
    return pl.pallas_call(
        _moe_kernel,
        grid_spec=grid_spec,
        out_shape=jax.ShapeDtypeStruct((n_blocks * buf_rows, LANES), U32),
        compiler_params=pltpu.CompilerParams(dimension_semantics=("arbitrary",),
                                             vmem_limit_bytes=VMEM_LIMIT),
    )(blk_expert, n_active, tok_buf, h2p, wg, wu, wd)


def _combine_kernel(dest_ref, y_hbm, x1_ref, wt_ref, g2_ref, lng_ref, lnb_ref, o_ref,
                    ybuf0, ybuf1, ybuf2, sem):
    i = pl.program_id(0)
    nt = pl.num_programs(0)
    tm = x1_ref.shape[0]
    count = TOP_K * tm
    bufs = (ybuf0, ybuf1, ybuf2)

    def fetch(tile_idx, c):
        _tile_gather_start(dest_ref, tile_idx * count, y_hbm, bufs[c], 0, sem.at[c], count)

    def drain(c):
        _tile_gather_wait(y_hbm, bufs[c], 0, sem.at[c], count)

    @pl.when(i == 0)
    def _():
        fetch(0, 0)
        fetch(jnp.minimum(1, nt - 1), 1)

    def tile(c):
        drain(c)
        first = [_unpack_halves(p) for p in _load_token_tiles(bufs[c], 0, tm)]
        second = [_unpack_halves(p) for p in _load_token_tiles(bufs[c], tm * TOKEN_TILE, tm)]
        fetch(jnp.minimum(i + 2, nt - 1), (c + 2) % GATHER_DEPTH)
        w1 = wt_ref[:, 0:1]
        w2 = wt_ref[:, 1:2]
        lo = [w1 * a[0] + w2 * b[0] for a, b in zip(first, second)]
        hi = [w1 * a[1] + w2 * b[1] for a, b in zip(first, second)]
        y = jnp.concatenate(lo + hi, axis=1)
        r = ALPHA * x1_ref[...] + (1.0 + g2_ref[0]) * y
        o_ref[...] = _layer_norm(r, lng_ref[...], lnb_ref[...])

        @pl.when(i == nt - 1)
        def _():
            drain((c + 1) % GATHER_DEPTH)
            drain((c + 2) % GATHER_DEPTH)

    for c in range(GATHER_DEPTH):
        pl.when(lax.rem(i, GATHER_DEPTH) == c)(functools.partial(tile, c))


def _combine(dest_tiled, y_buf, x1, wt, g2, ln_g, ln_b, seq, tm):
    n = x1.shape[0]
    per_batch = seq // tm
    grid_spec = pltpu.PrefetchScalarGridSpec(
        num_scalar_prefetch=1,
        grid=(n // tm,),
        in_specs=[pl.BlockSpec(memory_space=pl.ANY),
                  pl.BlockSpec((tm, D_MODEL), lambda i, d: (i, 0)),
                  pl.BlockSpec((tm, LANES), lambda i, d: (i, 0)),
                  pl.BlockSpec((1, 1, D_MODEL), lambda i, d: (i // per_batch, 0, 0)),
                  pl.BlockSpec((1, D_MODEL), lambda i, d: (0, 0)),
                  pl.BlockSpec((1, D_MODEL), lambda i, d: (0, 0))],
        out_specs=pl.BlockSpec((tm, D_MODEL), lambda i, d: (i, 0)),
        scratch_shapes=[pltpu.VMEM((TOP_K * tm * TOKEN_TILE, LANES), U32)] * GATHER_DEPTH
        + [pltpu.SemaphoreType.DMA((GATHER_DEPTH,))])
    return pl.pallas_call(
        _combine_kernel,
        grid_spec=grid_spec,
        out_shape=jax.ShapeDtypeStruct((n, D_MODEL), F32),
        compiler_params=pltpu.CompilerParams(dimension_semantics=("arbitrary",),
                                             vmem_limit_bytes=VMEM_LIMIT),
    )(dest_tiled, y_buf, x1, wt, g2, ln_g.reshape(1, D_MODEL), ln_b.reshape(1, D_MODEL))


def _dispatch_plan(route, counts, n):
    m = n * TOP_K
    n_blocks = -(-m // MOE_BLOCK) + N_EXPERTS
    expert = route[0:2].astype(jnp.int32)
    rank = route[4:6].astype(jnp.int32)
    sizes = counts[:, 0].astype(jnp.int32)
    padded = (sizes + MOE_BLOCK - 1) // MOE_BLOCK * MOE_BLOCK
    p_end = jnp.cumsum(padded)
    p_start = p_end - padded
    ids = jnp.arange(N_EXPERTS, dtype=jnp.int32)
    start_of = jnp.sum(jnp.where(expert[..., None] == ids, p_start, 0), axis=-1)
    dest = start_of + rank
    blk_start = jnp.arange(n_blocks, dtype=jnp.int32) * MOE_BLOCK
    blk_expert = jnp.minimum(jnp.sum((p_end[None, :] <= blk_start[:, None]).astype(jnp.int32), axis=1),
                             N_EXPERTS - 1)
    n_active = (p_end[-1] // MOE_BLOCK).astype(jnp.int32).reshape(1)
    pad_lo = p_start + sizes
    pad_hi = p_end.at[N_EXPERTS - 1].set(n_blocks * MOE_BLOCK)
    return dest, blk_expert, n_active, n_blocks, pad_lo, pad_hi


def kernel(x, c, w_ada, b_ada, w_in, dn_conv_w, dn_a_log, dn_dt_bias, dn_norm_w, fox_f_bias, w_out,
           ln1_g, ln1_b, w_router_group, b_router_group, w_router_expert, b_router_expert,
           w_gate, w_up, w_down, ln2_g, ln2_b):
    bsz, seq, _ = x.shape
    n = bsz * seq
    x2 = x.reshape(n, D_MODEL)

    mod = _ada(c, w_ada[0], b_ada[0])
    sh1, sc1, g1, sh2, sc2, g2 = [m.reshape(bsz, 1, D_MODEL) for m in jnp.split(mod, 6, axis=-1)]

    wi = w_in[0]
    o_b = 4 * DN_WIDTH
    o_a = o_b + DN_HEADS
    o_fq = o_a + DN_HEADS
    o_f = o_fq + 3 * FOX_WIDTH
    w_main = jnp.concatenate([wi[:, :o_b], wi[:, o_fq:o_f]], axis=1).astype(BF16)
    w_small = jnp.concatenate([wi[:, o_b:o_fq], wi[:, o_f:],
                               jnp.zeros((D_MODEL, LANES - 2 * DN_HEADS - FOX_HEADS), F32)],
                              axis=1).astype(BF16)
    zeros8 = jnp.zeros((DN_HEADS,), F32)
    pad = jnp.zeros((LANES - 2 * DN_HEADS - FOX_HEADS,), F32)
    gate_par = jnp.stack([jnp.concatenate([zeros8, dn_dt_bias[0], fox_f_bias[0], pad]),
                          jnp.concatenate([zeros8, dn_a_log[0], zeros8, pad])])

    main, gates = _inproj(x2, sc1, sh1, w_main, w_small, gate_par, seq)
    main3 = main.reshape(bsz, seq, D_MAIN)
    gates3 = gates.reshape(bsz, seq, LANES)
    gc3, fk_aug = _cumsum(gates3)
    gct = jnp.swapaxes(gc3[:, :, DN_HEADS:2 * DN_HEADS], 1, 2)

    o_dn = _deltanet(main3, gates3, gc3, gct, dn_conv_w[0], dn_norm_w[0])
    o_fox = _fox(main3, fk_aug)

    wo = w_out[0].astype(BF16)
    wr_t = jnp.concatenate([w_router_group[0].T, jnp.zeros((8 - N_GROUPS, D_MODEL), F32),
                            jnp.transpose(w_router_expert[0], (0, 2, 1)).reshape(N_EXPERTS, D_MODEL)],
                           axis=0)
    br_t = jnp.concatenate([b_router_group[0], jnp.zeros((8 - N_GROUPS,), F32),
                            b_router_expert[0].reshape(-1)])
    br_col = jnp.pad(br_t, (0, LANES - ROUTE_ROWS)).reshape(LANES, 1)
    wr_hi = wr_t.astype(BF16)
    wr_lo = (wr_t - wr_hi.astype(F32)).astype(BF16)
    lane_pad = ((0, 0), (0, LANES - ROUTE_ROWS))
    wr_split = jnp.concatenate([jnp.pad(wr_hi.T, lane_pad), jnp.pad(wr_lo.T, lane_pad)], axis=1)
    x1, h2p, route, counts = _outproj(o_dn.reshape(n, DN_WIDTH), o_fox.reshape(n, FOX_WIDTH),
                                      wo[:DN_WIDTH], wo[DN_WIDTH:], x2, g1, sc2, sh2, ln1_g[0], ln1_b[0],
                                      wr_split, br_col, seq)

    dest, blk_expert, n_active, n_blocks, pad_lo, pad_hi = _dispatch_plan(route, counts, n)
    tok_buf = _invert(dest.reshape(-1), pad_lo, pad_hi, n_blocks * MOE_BLOCK)
    y_buf = _moe(blk_expert, n_active, tok_buf, h2p,
                 w_gate[0].astype(BF16), w_up[0].astype(BF16), w_down[0].astype(BF16))

    tm = min(256, seq)
    dest_tiled = (dest * TOKEN_TILE).reshape(TOP_K, n // tm, tm).transpose(1, 0, 2).reshape(-1)
    wt = jnp.pad(route[2:4].T, ((0, 0), (0, LANES - TOP_K)))
    out = _combine(dest_tiled, y_buf, x1, wt, g2, ln2_g[0], ln2_b[0], seq, tm)
    return out.reshape(bsz, seq, D_MODEL)
```

```python
import functools
import math

import jax
import jax.numpy as jnp
from jax import lax
from jax.experimental import pallas as pl
from jax.experimental.pallas import tpu as pltpu

F32 = jnp.float32
BF16 = jnp.bfloat16
U32 = jnp.uint32
HIGHEST = lax.Precision.HIGHEST

D_MODEL = 2048
HEAD_DIM = 128
DN_HEADS = 8
FOX_HEADS = 8
DN_WIDTH = DN_HEADS * HEAD_DIM
FOX_WIDTH = FOX_HEADS * HEAD_DIM
CONV_WIDTH = 4
N_GROUPS = 4
EXPERTS_PER_GROUP = 8
N_EXPERTS = N_GROUPS * EXPERTS_PER_GROUP
TOP_K = 2
D_EXPERT = 512
MOE_BLOCK = 256
LN_EPS = 1e-5
NORM_EPS = 1e-6
ALPHA = 2.0 ** 0.25

D_MAIN = 3 * DN_WIDTH + DN_WIDTH + 3 * FOX_WIDTH
LANES = 128
BF16_SUBLANES = 16
DN_CHUNK = 128
NEG_BIG = -1e30
LOG2E = 1.4426950408889634
HALF = D_MODEL // 2
VMEM_LIMIT = 56 * 1024 * 1024


def _sigmoid(x):
    return 1.0 / (1.0 + jnp.exp(-x))


def _softplus(x):
    return jnp.maximum(x, 0.0) + jnp.log(1.0 + jnp.exp(-jnp.abs(x)))


def _mm(a, b):
    return jnp.dot(a.astype(BF16), b.astype(BF16), preferred_element_type=F32)


def _mm_nt(a, b):
    return lax.dot_general(a.astype(BF16), b.astype(BF16), (((1,), (1,)), ((), ())),
                           preferred_element_type=F32)


def _mm_tn(a, b):
    return lax.dot_general(a.astype(BF16), b.astype(BF16), (((0,), (0,)), ((), ())),
                           preferred_element_type=F32)


def _pack_halves(x):
    c = x.shape[1] // 2
    lo = pltpu.bitcast(x[:, :c].astype(BF16).astype(F32), U32)
    hi = pltpu.bitcast(x[:, c:].astype(BF16).astype(F32), U32)
    return (lo >> 16) | (hi & jnp.uint32(0xFFFF0000))


def _unpack_halves(p):
    lo = pltpu.bitcast(p << 16, F32)
    hi = pltpu.bitcast(p & jnp.uint32(0xFFFF0000), F32)
    return lo, hi


def _layer_norm(r, g, b):
    mu = jnp.mean(r, axis=-1, keepdims=True)
    d = r - mu
    var = jnp.mean(d * d, axis=-1, keepdims=True)
    return d * lax.rsqrt(var + LN_EPS) * g + b


def _ada_kernel(c_ref, w_ref, b_ref, o_ref):
    c = c_ref[...]
    a = c * _sigmoid(c)
    o_ref[...] = jnp.dot(a, w_ref[...], preferred_element_type=F32, precision=HIGHEST) + b_ref[...]


def _ada(c, w_ada, b_ada):
    bsz = c.shape[0]
    n_out = w_ada.shape[1]
    tn = 1024
    return pl.pallas_call(
        _ada_kernel,
        grid=(n_out // tn,),
        in_specs=[pl.BlockSpec((bsz, D_MODEL), lambda j: (0, 0)),
                  pl.BlockSpec((D_MODEL, tn), lambda j: (0, j)),
                  pl.BlockSpec((1, tn), lambda j: (0, j))],
        out_specs=pl.BlockSpec((bsz, tn), lambda j: (0, j)),
        out_shape=jax.ShapeDtypeStruct((bsz, n_out), F32),
        compiler_params=pltpu.CompilerParams(dimension_semantics=("arbitrary",),
                                             vmem_limit_bytes=VMEM_LIMIT),
    )(c, w_ada, b_ada.reshape(1, n_out))


def _inproj_kernel(x_ref, sc_ref, sh_ref, w_ref, ws_ref, par_ref, o_ref, g_ref, h_scr):
    @pl.when(pl.program_id(1) == 0)
    def _():
        h = (x_ref[...] * (1.0 + sc_ref[0]) + sh_ref[0]).astype(BF16)
        h_scr[...] = h
        s = jnp.dot(h, ws_ref[...], preferred_element_type=F32)
        lane = lax.broadcasted_iota(jnp.int32, s.shape, 1)
        sb = s + par_ref[0:1, :]
        neg_a = -jnp.exp(par_ref[1:2, :])
        beta = _sigmoid(s)
        g = neg_a * _softplus(sb)
        log_f = -_softplus(-sb)
        g_ref[...] = jnp.where(lane < DN_HEADS, beta,
                               jnp.where(lane < 2 * DN_HEADS, g,
                                         jnp.where(lane < 2 * DN_HEADS + FOX_HEADS, log_f, 0.0)))

    o_ref[...] = jnp.dot(h_scr[...], w_ref[...], preferred_element_type=F32).astype(BF16)


def _inproj(x2, sc1, sh1, w_main, w_small, gate_par, seq):
    n = x2.shape[0]
    tm = min(1024, seq)
    tn = D_MAIN // 4
    per_batch = seq // tm
    return pl.pallas_call(
        _inproj_kernel,
        grid=(n // tm, D_MAIN // tn),
        in_specs=[pl.BlockSpec((tm, D_MODEL), lambda i, j: (i, 0)),
                  pl.BlockSpec((1, 1, D_MODEL), lambda i, j: (i // per_batch, 0, 0)),
                  pl.BlockSpec((1, 1, D_MODEL), lambda i, j: (i // per_batch, 0, 0)),
                  pl.BlockSpec((D_MODEL, tn), lambda i, j: (0, j)),
                  pl.BlockSpec((D_MODEL, LANES), lambda i, j: (0, 0)),
                  pl.BlockSpec((2, LANES), lambda i, j: (0, 0))],
        out_specs=[pl.BlockSpec((tm, tn), lambda i, j: (i, j)),
                   pl.BlockSpec((tm, LANES), lambda i, j: (i, 0))],
        out_shape=[jax.ShapeDtypeStruct((n, D_MAIN), BF16),
                   jax.ShapeDtypeStruct((n, LANES), F32)],
        scratch_shapes=[pltpu.VMEM((tm, D_MODEL), BF16)],
        compiler_params=pltpu.CompilerParams(dimension_semantics=("arbitrary", "arbitrary"),
                                             vmem_limit_bytes=VMEM_LIMIT),
    )(x2, sc1, sh1, w_main, w_small, gate_par)


def _cumsum_kernel(g_ref, gc_ref, fk_ref, carry, tri_scr):
    first = jnp.logical_and(pl.program_id(0) == 0, pl.program_id(1) == 0)

    @pl.when(pl.program_id(1) == 0)
    def _():
        carry[...] = jnp.zeros_like(carry)

    x = g_ref[0]
    t = x.shape[0]

    @pl.when(first)
    def _():
        row = lax.broadcasted_iota(jnp.int32, (t, t), 0)
        col = lax.broadcasted_iota(jnp.int32, (t, t), 1)
        shift = int(math.log2(DN_CHUNK))
        tri_scr[:t, :] = jnp.where(col <= row, 1.0, 0.0).astype(BF16)
        tri_scr[t:, :] = jnp.where(jnp.logical_and(col <= row, (row >> shift) == (col >> shift)),
                                   1.0, 0.0).astype(BF16)

    hi = x.astype(BF16)
    r1 = x - hi.astype(F32)
    mid = r1.astype(BF16)
    lo = (r1 - mid.astype(F32)).astype(BF16)
    parts = jnp.dot(tri_scr[...], jnp.concatenate([hi, mid, lo], axis=1), preferred_element_type=F32)
    sums = parts[:, :LANES] + parts[:, LANES:2 * LANES] + parts[:, 2 * LANES:]
    cs = sums[:t]
    f = cs + carry[...]
    gc_ref[0] = sums[t:]
    carry[...] = carry[...] + cs[t - 1:t, :]

    lane = lax.broadcasted_iota(jnp.int32, (t, LANES), 1)
    for h in range(FOX_HEADS):
        c0 = 2 * DN_HEADS + h
        val = jnp.broadcast_to(f[:, c0:c0 + 1] * (-LOG2E), (t, LANES))
        hi = val.astype(BF16)
        r1 = val - hi.astype(F32)
        mid = r1.astype(BF16)
        lo = r1 - mid.astype(F32)
        fk_ref[0, :, h * LANES:(h + 1) * LANES] = jnp.where(
            lane == 0, hi.astype(F32),
            jnp.where(lane == 1, mid.astype(F32), jnp.where(lane == 2, lo, 0.0))).astype(BF16)


def _cumsum(gates3):
    bsz, seq, _ = gates3.shape
    t = min(512, seq)
    spec = pl.BlockSpec((1, t, LANES), lambda b, i: (b, i, 0))
    return pl.pallas_call(
        _cumsum_kernel,
        grid=(bsz, seq // t),
        in_specs=[spec],
        out_specs=[spec, pl.BlockSpec((1, t, FOX_HEADS * LANES), lambda b, i: (b, i, 0))],
        out_shape=[jax.ShapeDtypeStruct(gates3.shape, F32),
                   jax.ShapeDtypeStruct((bsz, seq, FOX_HEADS * LANES), BF16)],
        scratch_shapes=[pltpu.VMEM((1, LANES), F32), pltpu.VMEM((2 * t, t), BF16)],
        compiler_params=pltpu.CompilerParams(dimension_semantics=("arbitrary", "arbitrary")),
    )(gates3)


DN_CHUNKS_PER_STEP = 4


def _dn_kernel(qkv_ref, halo_ref, z_ref, gate_ref, gc_ref, gct_ref, convw_ref, normw_ref,
               o_ref, state_ref, xs_scr):
    t = pl.program_id(1)

    @pl.when(t == 0)
    def _():
        state_ref[...] = jnp.zeros_like(state_ref)

    c = DN_CHUNK
    d = HEAD_DIM
    heads = range(DN_HEADS)
    row = lax.broadcasted_iota(jnp.int32, (c, c), 0)
    col = lax.broadcasted_iota(jnp.int32, (c, c), 1)
    strict = col < row
    eye = (row == col).astype(F32)
    neg_diag8_f = jnp.where((row >> 3) == (col >> 3), -1.0, 0.0)
    merge_levels = range(3, int(math.log2(c)))
    low_f = [jnp.where((row >> (s + 1)) == (col >> (s + 1)),
                       jnp.where((row >> s) == (col >> s), 0.0, 1.0), 0.0) for s in merge_levels]
    halo_on = (t > 0).astype(F32)
    normw = normw_ref[...]

    def l2n(x):
        return x * lax.rsqrt(jnp.sum(x * x, axis=-1, keepdims=True) + NORM_EPS)

    def bdot(a, b):
        return jnp.dot(a, b, preferred_element_type=F32)

    def conv_head(ci, h):
        r0 = ci * c

        def conv_silu(part):
            off = part * DN_WIDTH + h * d
            slab = (ci * 3 + part) * DN_HEADS + h
            x = qkv_ref[0, r0:r0 + c, off:off + d].astype(F32)
            if ci == 0:
                prev = halo_ref[0, BF16_SUBLANES - 8:, off:off + d].astype(F32) * halo_on
            else:
                prev = qkv_ref[0, r0 - 8:r0, off:off + d].astype(F32)
            xs_scr[slab, 0:8, :] = prev
            xs_scr[slab, 8:, :] = x
            w = convw_ref[:, off:off + d]
            acc = x * w[CONV_WIDTH - 1:CONV_WIDTH]
            for k in range(1, CONV_WIDTH):
                acc = acc + xs_scr[slab, pl.ds(8 - k, c), :] * w[CONV_WIDTH - 1 - k:CONV_WIDTH - k]
            return acc * _sigmoid(acc)

        return l2n(conv_silu(0)) * (d ** -0.5), l2n(conv_silu(1)), conv_silu(2)

    def chunk_steps(ci, qkv):
        r0 = ci * c
        q = [x[0] for x in qkv]
        k = [x[1] for x in qkv]
        v = [x[2] for x in qkv]
        gates = gate_ref[0, r0:r0 + c, :]
        gcs = gc_ref[0, r0:r0 + c, :]
        gct = gct_ref[0, :, r0:r0 + c]
        beta = [gates[:, h:h + 1] for h in heads]
        g_col = [gcs[:, DN_HEADS + h:DN_HEADS + h + 1] for h in heads]
        e_g = [jnp.exp(g) for g in g_col]
        g_last = [g[c - 1:c, :] for g in g_col]
        decay = [jnp.exp(jnp.where(strict, g_col[h] - gct[h:h + 1, :], NEG_BIG)) for h in heads]
        kb = [k[h] * beta[h] for h in heads]

        kk = [_mm_nt(jnp.concatenate([kb[h], q[h]], axis=0), k[h]) for h in heads]
        yield
        a = [kk[h][:c] * decay[h] for h in heads]
        qk = [(kk[h][c:] * (decay[h] + eye)).astype(BF16) for h in heads]
        n1 = [a[h] * neg_diag8_f for h in heads]
        n1b = [n.astype(BF16) for n in n1]
        x = [eye + n for n in n1]
        xb = [xx.astype(BF16) for xx in x]
        n2b = [bdot(n, n).astype(BF16) for n in n1b]
        yield
        x = [x[h] + bdot(n2b[h], xb[h]) for h in heads]
        n4b = [bdot(n, n).astype(BF16) for n in n2b]
        yield
        xb = [xx.astype(BF16) for xx in x]
        x = [x[h] + bdot(n4b[h], xb[h]) for h in heads]
        yield
        for lf in low_f:
            xb = [xx.astype(BF16) for xx in x]
            lx = [bdot((a[h] * lf).astype(BF16), xb[h]).astype(BF16) for h in heads]
            yield
            x = [x[h] - bdot(xb[h], lx[h]) for h in heads]
            yield

        uw = [_mm(x[h], jnp.concatenate([v[h] * beta[h], kb[h] * e_g[h]], axis=1)) for h in heads]
        yield
        s_old = [state_ref[h] for h in heads]
        r = [_mm(jnp.concatenate([uw[h][:, d:], q[h] * e_g[h]], axis=0), s_old[h]) for h in heads]
        yield
        v_new = [(uw[h][:, :d] - r[h][:c]).astype(BF16) for h in heads]
        o = [r[h][c:] + bdot(qk[h], v_new[h]) for h in heads]
        for h in heads:
            kd = k[h] * jnp.exp(g_last[h] - g_col[h])
            state_ref[h] = s_old[h] * jnp.exp(g_last[h]) + _mm_tn(kd, v_new[h])
        yield
        for h in heads:
            z = z_ref[0, r0:r0 + c, h * d:(h + 1) * d].astype(F32)
            rms = lax.rsqrt(jnp.mean(o[h] * o[h], axis=-1, keepdims=True) + NORM_EPS)
            o_ref[0, r0:r0 + c, h * d:(h + 1) * d] = (o[h] * rms * normw * (z * _sigmoid(z))).astype(BF16)

    qkv = [conv_head(0, h) for h in heads]
    for ci in range(DN_CHUNKS_PER_STEP):
        following = []
        more = ci + 1 < DN_CHUNKS_PER_STEP
        for _ in chunk_steps(ci, qkv):
            if more and len(following) < DN_HEADS:
                following.append(conv_head(ci + 1, len(following)))
        while more and len(following) < DN_HEADS:
            following.append(conv_head(ci + 1, len(following)))
        qkv = following


def _deltanet(main3, gates3, gc3, gct, conv_w, norm_w):
    bsz, seq, _ = main3.shape
    rows = DN_CHUNK * DN_CHUNKS_PER_STEP
    halo_blocks = rows // BF16_SUBLANES
    return pl.pallas_call(
        _dn_kernel,
        grid=(bsz, seq // rows),
        in_specs=[pl.BlockSpec((1, rows, 3 * DN_WIDTH), lambda b, t: (b, t, 0)),
                  pl.BlockSpec((1, BF16_SUBLANES, 3 * DN_WIDTH),
                               lambda b, t: (b, jnp.maximum(t * halo_blocks - 1, 0), 0)),
                  pl.BlockSpec((1, rows, DN_WIDTH), lambda b, t: (b, t, 3)),
                  pl.BlockSpec((1, rows, LANES), lambda b, t: (b, t, 0)),
                  pl.BlockSpec((1, rows, LANES), lambda b, t: (b, t, 0)),
                  pl.BlockSpec((1, DN_HEADS, rows), lambda b, t: (b, 0, t)),
                  pl.BlockSpec((CONV_WIDTH, 3 * DN_WIDTH), lambda b, t: (0, 0)),
                  pl.BlockSpec((1, HEAD_DIM), lambda b, t: (0, 0))],
        out_specs=pl.BlockSpec((1, rows, DN_WIDTH), lambda b, t: (b, t, 0)),
        out_shape=jax.ShapeDtypeStruct((bsz, seq, DN_WIDTH), BF16),
        scratch_shapes=[pltpu.VMEM((DN_HEADS, HEAD_DIM, HEAD_DIM), F32),
                        pltpu.VMEM((3 * DN_HEADS * DN_CHUNKS_PER_STEP, 8 + DN_CHUNK, HEAD_DIM), F32)],
        compiler_params=pltpu.CompilerParams(dimension_semantics=("arbitrary", "arbitrary"),
                                             vmem_limit_bytes=VMEM_LIMIT),
    )(main3, main3, main3, gates3, gc3, gct, conv_w, norm_w.reshape(1, HEAD_DIM))


def _fox_kernel(qi_ref, kj_ref, q_ref, k_ref, v_ref, fk_ref, o_ref, qa_scr, m_scr, l_scr, acc_scr):
    step = pl.program_id(1)
    i = qi_ref[step]
    j = kj_ref[step]
    d = HEAD_DIM
    tq = q_ref.shape[1]
    tk = k_ref.shape[1]

    @pl.when(j == 0)
    def _():
        m_scr[...] = jnp.full_like(m_scr, NEG_BIG)
        l_scr[...] = jnp.zeros_like(l_scr)
        acc_scr[...] = jnp.zeros_like(acc_scr)
        lane = lax.broadcasted_iota(jnp.int32, (tq, LANES), 1)
        ones = jnp.where(lane < 3, 1.0, 0.0).astype(BF16)
        for h in range(FOX_HEADS):
            q = (q_ref[0, :, h * d:(h + 1) * d].astype(F32) * (d ** -0.5 * LOG2E)).astype(BF16)
            qa_scr[h] = jnp.concatenate([q, ones], axis=1)

    def update(diagonal):
        if diagonal:
            key_pos = lax.broadcasted_iota(jnp.int32, (tk, tq), 0)
            q_pos = lax.broadcasted_iota(jnp.int32, (tk, tq), 1)
            visible = key_pos <= q_pos

        def scores(h):
            k_aug = jnp.concatenate([k_ref[0, :, h * d:(h + 1) * d],
                                     fk_ref[0, :, h * LANES:(h + 1) * LANES]], axis=1)
            return lax.dot_general(k_aug, qa_scr[h], (((1,), (1,)), ((), ())),
                                   preferred_element_type=F32)

        pending = [scores(0), scores(1)]
        for h in range(FOX_HEADS):
            s = pending.pop(0)
            if h + 2 < FOX_HEADS:
                pending.append(scores(h + 2))
            v = v_ref[0, :, h * d:(h + 1) * d]
            if diagonal:
                s = jnp.where(visible, s, NEG_BIG)
            m_old = m_scr[h]
            m_new = jnp.maximum(m_old, jnp.max(s, axis=0, keepdims=True))
            alpha = jnp.exp2(m_old - m_new)
            p = jnp.exp2(s - m_new)
            l_scr[h] = alpha * l_scr[h] + jnp.sum(p, axis=0, keepdims=True)
            acc_scr[h] = alpha * acc_scr[h] + _mm_tn(v, p)
            m_scr[h] = m_new

    @pl.when(j < i)
    def _():
        update(False)

    @pl.when(j == i)
    def _():
        update(True)
        for h in range(FOX_HEADS):
            o = acc_scr[h] / l_scr[h]
            o_ref[0, :, h * d:(h + 1) * d] = o.T.astype(BF16)


def _fox(main3, fk_aug):
    bsz, seq, _ = main3.shape
    tq = min(512, seq)
    n_blk = seq // tq
    w = FOX_WIDTH
    base = (4 * DN_WIDTH) // w
    pairs = [(i, j) for i in range(n_blk) for j in range(i + 1)]
    qi = jnp.array([p[0] for p in pairs], jnp.int32)
    kj = jnp.array([p[1] for p in pairs], jnp.int32)
    grid_spec = pltpu.PrefetchScalarGridSpec(
        num_scalar_prefetch=2,
        grid=(bsz, len(pairs)),
        in_specs=[pl.BlockSpec((1, tq, w), lambda b, t, qi, kj: (b, qi[t], base)),
                  pl.BlockSpec((1, tq, w), lambda b, t, qi, kj: (b, kj[t], base + 1)),
                  pl.BlockSpec((1, tq, w), lambda b, t, qi, kj: (b, kj[t], base + 2)),
                  pl.BlockSpec((1, tq, FOX_HEADS * LANES), lambda b, t, qi, kj: (b, kj[t], 0))],
        out_specs=pl.BlockSpec((1, tq, w), lambda b, t, qi, kj: (b, qi[t], 0)),
        scratch_shapes=[pltpu.VMEM((FOX_HEADS, tq, 2 * HEAD_DIM), BF16),
                        pltpu.VMEM((FOX_HEADS, 1, tq), F32),
                        pltpu.VMEM((FOX_HEADS, 1, tq), F32),
                        pltpu.VMEM((FOX_HEADS, HEAD_DIM, tq), F32)])
    return pl.pallas_call(
        _fox_kernel,
        grid_spec=grid_spec,
        out_shape=jax.ShapeDtypeStruct((bsz, seq, w), BF16),
        compiler_params=pltpu.CompilerParams(
            dimension_semantics=("arbitrary", "arbitrary"),
            vmem_limit_bytes=VMEM_LIMIT),
    )(qi, kj, main3, main3, main3, fk_aug)


ROUTE_ROWS = 8 + N_EXPERTS
TOKEN_TILE = 8


def _store_token_tiles(ref, packed, first_token=0):
    m = packed.shape[0]
    for s in range(TOKEN_TILE):
        ref[pl.ds(first_token * TOKEN_TILE + s, m, stride=TOKEN_TILE), :] = packed[:, s * LANES:(s + 1) * LANES]


def _load_token_tiles(ref, start, m):
    return [ref[pl.ds(start + s, m, stride=TOKEN_TILE), :] for s in range(TOKEN_TILE)]


def _outproj_kernel(odn_ref, ofox_ref, w1_ref, w2_ref, x_ref, g1_ref, sc2_ref, sh2_ref,
                    lng_ref, lnb_ref, wr_ref, br_ref, x1_ref, h2p_ref, route_ref, count_ref,
                    upper_scr, carry_scr):
    tm = x_ref.shape[0]

    @pl.when(pl.program_id(0) == 0)
    def _():
        r = lax.broadcasted_iota(jnp.int32, (tm, tm), 0)
        c = lax.broadcasted_iota(jnp.int32, (tm, tm), 1)
        upper_scr[...] = jnp.where(r < c, 1.0, 0.0).astype(BF16)
        carry_scr[...] = jnp.zeros_like(carry_scr)

    y = (jnp.dot(odn_ref[...], w1_ref[...], preferred_element_type=F32)
         + jnp.dot(ofox_ref[...], w2_ref[...], preferred_element_type=F32))
    _outproj_tail(0, tm, y, x_ref, g1_ref, sc2_ref, sh2_ref, lng_ref, lnb_ref, wr_ref, br_ref,
                  x1_ref, h2p_ref, route_ref, upper_scr, carry_scr)
    count_ref[...] = jnp.broadcast_to(carry_scr[...], count_ref.shape)


def _outproj_tail(u, tm, y, x_ref, g1_ref, sc2_ref, sh2_ref, lng_ref, lnb_ref, wr_ref, br_ref,
                  x1_ref, h2p_ref, route_ref, upper_scr, carry_scr):
    rows = slice(u * tm, (u + 1) * tm)
    r = ALPHA * x_ref[rows, :] + (1.0 + g1_ref[0]) * y
    x1 = _layer_norm(r, lng_ref[...], lnb_ref[...])
    x1_ref[rows, :] = x1
    h2 = x1 * (1.0 + sc2_ref[0]) + sh2_ref[0]
    _store_token_tiles(h2p_ref, _pack_halves(h2), u * tm)

    h_hi = h2.astype(BF16)
    h_lo = (h2 - h_hi.astype(F32)).astype(BF16)
    parts = jnp.dot(jnp.concatenate([h_hi, h_lo], axis=0), wr_ref[...], preferred_element_type=F32)
    terms = parts[:tm] + parts[tm:]
    lt = (terms[:, :LANES] + terms[:, LANES:]).T + br_ref[...]
    row = [lt[n:n + 1, :] for n in range(ROUTE_ROWS)]

    def first_max(vals):
        best = vals[0]
        for val in vals[1:]:
            best = jnp.maximum(best, val)
        idx = jnp.full(best.shape, len(vals) - 1, jnp.int32)
        for n in range(len(vals) - 2, -1, -1):
            idx = jnp.where(vals[n] == best, n, idx)
        return best, idx

    gmax, gidx = first_max(row[:N_GROUPS])
    gsum = jnp.exp(row[0] - gmax)
    for n in range(1, N_GROUPS):
        gsum = gsum + jnp.exp(row[n] - gmax)
    g_w = 1.0 / gsum
    el = []
    for e in range(EXPERTS_PER_GROUP):
        val = row[8 + e]
        for g in range(1, N_GROUPS):
            val = jnp.where(gidx == g, row[8 + EXPERTS_PER_GROUP * g + e], val)
        el.append(val)
    emax, _ = first_max(el)
    ee = [jnp.exp(val - emax) for val in el]
    esum = ee[0]
    for val in ee[1:]:
        esum = esum + val
    prob = [val / esum for val in ee]
    p1, e1 = first_max(prob)
    p2, e2 = first_max([jnp.where(e1 == e, -1.0, prob[e]) for e in range(EXPERTS_PER_GROUP)])
    denom = p1 + p2
    ex1 = gidx * EXPERTS_PER_GROUP + e1
    ex2 = gidx * EXPERTS_PER_GROUP + e2
    ridx = lax.broadcasted_iota(jnp.int32, (8, tm), 0)

    eidx = lax.broadcasted_iota(jnp.int32, (N_EXPERTS, tm), 0)
    oh1 = jnp.where(eidx == ex1, 1.0, 0.0)
    oh2 = jnp.where(eidx == ex2, 1.0, 0.0)
    both = oh1 + oh2
    before = jnp.dot(both.astype(BF16), upper_scr[:tm, :tm], preferred_element_type=F32) + carry_scr[...]
    rank1 = jnp.sum(oh1 * before, axis=0, keepdims=True)
    rank2 = jnp.sum(oh2 * before, axis=0, keepdims=True)
    carry_scr[...] = carry_scr[...] + jnp.sum(both, axis=1, keepdims=True)

    fields = [ex1.astype(F32), ex2.astype(F32), g_w * p1 / denom, g_w * p2 / denom, rank1, rank2]
    out = jnp.zeros((8, tm), F32)
    for n, val in enumerate(fields):
        out = jnp.where(ridx == n, val, out)
    route_ref[:, rows] = out


def _outproj(o_dn, o_fox, w1, w2, x2, g1, sc2, sh2, ln_g, ln_b, wr_t, br_t, seq):
    n = x2.shape[0]
    tm = min(512, seq)
    per_batch = seq // tm
    row = lambda i: (i, 0)
    fixed = lambda i: (0, 0)
    mod = pl.BlockSpec((1, 1, D_MODEL), lambda i: (i // per_batch, 0, 0))
    return pl.pallas_call(
        _outproj_kernel,
        grid=(n // tm,),
        in_specs=[pl.BlockSpec((tm, DN_WIDTH), row),
                  pl.BlockSpec((tm, FOX_WIDTH), row),
                  pl.BlockSpec((DN_WIDTH, D_MODEL), fixed),
                  pl.BlockSpec((FOX_WIDTH, D_MODEL), fixed),
                  pl.BlockSpec((tm, D_MODEL), row),
                  mod, mod, mod,
                  pl.BlockSpec((1, D_MODEL), fixed),
                  pl.BlockSpec((1, D_MODEL), fixed),
                  pl.BlockSpec((D_MODEL, 2 * LANES), fixed),
                  pl.BlockSpec((LANES, 1), fixed)],
        out_specs=[pl.BlockSpec((tm, D_MODEL), row),
                   pl.BlockSpec((tm * TOKEN_TILE, LANES), row),
                   pl.BlockSpec((8, tm), lambda i: (0, i)),
                   pl.BlockSpec((N_EXPERTS, LANES), fixed)],
        out_shape=[jax.ShapeDtypeStruct((n, D_MODEL), F32),
                   jax.ShapeDtypeStruct((n * TOKEN_TILE, LANES), U32),
                   jax.ShapeDtypeStruct((8, n), F32),
                   jax.ShapeDtypeStruct((N_EXPERTS, LANES), F32)],
        scratch_shapes=[pltpu.VMEM((tm, tm), BF16),
                        pltpu.VMEM((N_EXPERTS, 1), F32)],
        compiler_params=pltpu.CompilerParams(dimension_semantics=("arbitrary",),
                                             vmem_limit_bytes=VMEM_LIMIT),
    )(o_dn, o_fox, w1, w2, x2, g1, sc2, sh2, ln_g.reshape(1, D_MODEL), ln_b.reshape(1, D_MODEL),
      wr_t, br_t)


INVERT_STEPS = N_EXPERTS
INVERT_BATCH = 16


def _invert_kernel(dest_ref, pad_lo_ref, pad_hi_ref, tok_ref):
    phase = pl.program_id(0)
    i = pl.program_id(1)
    n_tok = dest_ref.shape[0] // TOP_K
    toks_per = n_tok // INVERT_STEPS

    @pl.when(phase == 0)
    def _():
        def clear(p, carry):
            tok_ref[p] = 0
            return carry
        lax.fori_loop(pad_lo_ref[i], pad_hi_ref[i], clear, 0)

    @pl.when(phase == 1)
    def _():
        def put(t, carry):
            first = i * toks_per + t * INVERT_BATCH
            rows = [[dest_ref[k * n_tok + first + u] for k in range(TOP_K)] for u in range(INVERT_BATCH)]
            for u in range(INVERT_BATCH):
                for k in range(TOP_K):
                    tok_ref[rows[u][k]] = (first + u) * TOKEN_TILE
            return carry
        lax.fori_loop(0, toks_per // INVERT_BATCH, put, 0)


def _invert(dest_flat, pad_lo, pad_hi, n_slots):
    assert (dest_flat.shape[0] // TOP_K) % (INVERT_STEPS * INVERT_BATCH) == 0
    smem = pl.BlockSpec(memory_space=pltpu.SMEM)
    return pl.pallas_call(
        _invert_kernel,
        grid=(2, INVERT_STEPS),
        in_specs=[smem, smem, smem],
        out_specs=smem,
        out_shape=jax.ShapeDtypeStruct((n_slots,), jnp.int32),
        compiler_params=pltpu.CompilerParams(dimension_semantics=("arbitrary", "arbitrary")),
    )(dest_flat, pad_lo, pad_hi)


def _tile_gather_start(idx_ref, base, src_hbm, dst, dst_row0, sem, count):
    for r in range(count):
        row = pl.multiple_of(idx_ref[base + r], TOKEN_TILE)
        pltpu.make_async_copy(src_hbm.at[pl.ds(row, TOKEN_TILE)],
                              dst.at[pl.ds(dst_row0 + r * TOKEN_TILE, TOKEN_TILE)],
                              sem).start(priority=r % 2)


def _tile_gather_wait(src_hbm, dst, dst_row0, sem, count):
    rows = count * TOKEN_TILE
    pltpu.make_async_copy(src_hbm.at[pl.ds(0, rows)], dst.at[pl.ds(dst_row0, rows)], sem).wait()


GATHER_DEPTH = 3


def _moe_kernel(blk_expert_ref, n_active_ref, tok_ref, h_hbm, wg_ref, wu_ref, wd_ref,
                y_ref, xbuf0, xbuf1, xbuf2, sem):
    b = pl.program_id(0)
    nb = pl.num_programs(0)
    n_active = n_active_ref[0]
    last = jnp.maximum(jnp.minimum(n_active, nb), 1) - 1
    bufs = (xbuf0, xbuf1, xbuf2)

    def fetch(blk, c):
        _tile_gather_start(tok_ref, blk * MOE_BLOCK, h_hbm, bufs[c], 0, sem.at[c], MOE_BLOCK)

    def drain(c):
        _tile_gather_wait(h_hbm, bufs[c], 0, sem.at[c], MOE_BLOCK)

    @pl.when(b == 0)
    def _():
        fetch(0, 0)
        fetch(jnp.minimum(1, last), 1)

    def block(c):
        drain(c)
        parts = [_unpack_halves(p) for p in _load_token_tiles(bufs[c], 0, MOE_BLOCK)]
        lo = jnp.concatenate([p[0] for p in parts], axis=1).astype(BF16)
        hi = jnp.concatenate([p[1] for p in parts], axis=1).astype(BF16)
        fetch(jnp.minimum(b + 2, last), (c + 2) % GATHER_DEPTH)
        gate = (jnp.dot(lo, wg_ref[0, :HALF, :], preferred_element_type=F32)
                + jnp.dot(hi, wg_ref[0, HALF:, :], preferred_element_type=F32))
        up = (jnp.dot(lo, wu_ref[0, :HALF, :], preferred_element_type=F32)
              + jnp.dot(hi, wu_ref[0, HALF:, :], preferred_element_type=F32))
        hid = (gate * _sigmoid(gate) * up).astype(BF16)
        y = jnp.dot(hid, wd_ref[0], preferred_element_type=F32)
        _store_token_tiles(y_ref, _pack_halves(y))

        @pl.when(b == last)
        def _():
            drain((c + 1) % GATHER_DEPTH)
            drain((c + 2) % GATHER_DEPTH)

    for c in range(GATHER_DEPTH):
        pl.when(jnp.logical_and(b <= last, lax.rem(b, GATHER_DEPTH) == c))(functools.partial(block, c))

    @pl.when(b > last)
    def _():
        y_ref[...] = jnp.zeros_like(y_ref)


def _moe(blk_expert, n_active, tok_buf, h2p, wg, wu, wd):
    n_blocks = blk_expert.shape[0]
    buf_rows = MOE_BLOCK * TOKEN_TILE
    grid_spec = pltpu.PrefetchScalarGridSpec(
        num_scalar_prefetch=3,
        grid=(n_blocks,),
        in_specs=[pl.BlockSpec(memory_space=pl.ANY),
                  pl.BlockSpec((1, D_MODEL, D_EXPERT), lambda b, be, na, tk: (be[b], 0, 0)),
                  pl.BlockSpec((1, D_MODEL, D_EXPERT), lambda b, be, na, tk: (be[b], 0, 0)),
                  pl.BlockSpec((1, D_EXPERT, D_MODEL), lambda b, be, na, tk: (be[b], 0, 0))],
        out_specs=pl.BlockSpec((buf_rows, LANES), lambda b, be, na, tk: (b, 0)),
        scratch_shapes=[pltpu.VMEM((buf_rows, LANES), U32)] * GATHER_DEPTH
        + [pltpu.SemaphoreType.DMA((GATHER_DEPTH,))])
    return pl.pallas_call(
        _moe_kernel,
        grid_spec=grid_spec,
        out_shape=jax.ShapeDtypeStruct((n_blocks * buf_rows, LANES), U32),
        compiler_params=pltpu.CompilerParams(dimension_semantics=("arbitrary",),
                                             vmem_limit_bytes=VMEM_LIMIT),
    )(blk_expert, n_active, tok_buf, h2p, wg, wu, wd)


def _combine_kernel(dest_ref, y_hbm, x1_ref, wt_ref, g2_ref, lng_ref, lnb_ref, o_ref,
                    ybuf0, ybuf1, ybuf2, sem):
    i = pl.program_id(0)
    nt = pl.num_programs(0)
    tm = x1_ref.shape[0]
    count = TOP_K * tm
    bufs = (ybuf0, ybuf1, ybuf2)

    def fetch(tile_idx, c):
        _tile_gather_start(dest_ref, tile_idx * count, y_hbm, bufs[c], 0, sem.at[c], count)

    def drain(c):
        _tile_gather_wait(y_hbm, bufs[c], 0, sem.at[c], count)

    @pl.when(i == 0)
    def _():
        fetch(0, 0)
        fetch(jnp.minimum(1, nt - 1), 1)

    def tile(c):
        drain(c)
        first = [_unpack_halves(p) for p in _load_token_tiles(bufs[c], 0, tm)]
        second = [_unpack_halves(p) for p in _load_token_tiles(bufs[c], tm * TOKEN_TILE, tm)]
        fetch(jnp.minimum(i + 2, nt - 1), (c + 2) % GATHER_DEPTH)
        w1 = wt_ref[:, 0:1]
        w2 = wt_ref[:, 1:2]
        lo = [w1 * a[0] + w2 * b[0] for a, b in zip(first, second)]
        hi = [w1 * a[1] + w2 * b[1] for a, b in zip(first, second)]
        y = jnp.concatenate(lo + hi, axis=1)
        r = ALPHA * x1_ref[...] + (1.0 + g2_ref[0]) * y
        o_ref[...] = _layer_norm(r, lng_ref[...], lnb_ref[...])

        @pl.when(i == nt - 1)
        def _():
            drain((c + 1) % GATHER_DEPTH)
            drain((c + 2) % GATHER_DEPTH)

    for c in range(GATHER_DEPTH):
        pl.when(lax.rem(i, GATHER_DEPTH) == c)(functools.partial(tile, c))


def _combine(dest_tiled, y_buf, x1, wt, g2, ln_g, ln_b, seq, tm):
    n = x1.shape[0]
    per_batch = seq // tm
    grid_spec = pltpu.PrefetchScalarGridSpec(
        num_scalar_prefetch=1,
        grid=(n // tm,),
        in_specs=[pl.BlockSpec(memory_space=pl.ANY),
                  pl.BlockSpec((tm, D_MODEL), lambda i, d: (i, 0)),
                  pl.BlockSpec((tm, LANES), lambda i, d: (i, 0)),
                  pl.BlockSpec((1, 1, D_MODEL), lambda i, d: (i // per_batch, 0, 0)),
                  pl.BlockSpec((1, D_MODEL), lambda i, d: (0, 0)),
                  pl.BlockSpec((1, D_MODEL), lambda i, d: (0, 0))],
        out_specs=pl.BlockSpec((tm, D_MODEL), lambda i, d: (i, 0)),
        scratch_shapes=[pltpu.VMEM((TOP_K * tm * TOKEN_TILE, LANES), U32)] * GATHER_DEPTH
        + [pltpu.SemaphoreType.DMA((GATHER_DEPTH,))])
    return pl.pallas_call(
        _combine_kernel,
        grid_spec=grid_spec,
        out_shape=jax.ShapeDtypeStruct((n, D_MODEL), F32),
        compiler_params=pltpu.CompilerParams(dimension_semantics=("arbitrary",),
                                             vmem_limit_bytes=VMEM_LIMIT),
    )(dest_tiled, y_buf, x1, wt, g2, ln_g.reshape(1, D_MODEL), ln_b.reshape(1, D_MODEL))


def _dispatch_plan(route, counts, n):
    m = n * TOP_K
    n_blocks = -(-m // MOE_BLOCK) + N_EXPERTS
    expert = route[0:2].astype(jnp.int32)
    rank = route[4:6].astype(jnp.int32)
    sizes = counts[:, 0].astype(jnp.int32)
    padded = (sizes + MOE_BLOCK - 1) // MOE_BLOCK * MOE_BLOCK
    p_end = jnp.cumsum(padded)
    p_start = p_end - padded
    ids = jnp.arange(N_EXPERTS, dtype=jnp.int32)
    start_of = jnp.sum(jnp.where(expert[..., None] == ids, p_start, 0), axis=-1)
    dest = start_of + rank
    blk_start = jnp.arange(n_blocks, dtype=jnp.int32) * MOE_BLOCK
    blk_expert = jnp.minimum(jnp.sum((p_end[None, :] <= blk_start[:, None]).astype(jnp.int32), axis=1),
                             N_EXPERTS - 1)
    n_active = (p_end[-1] // MOE_BLOCK).astype(jnp.int32).reshape(1)
    pad_lo = p_start + sizes
    pad_hi = p_end.at[N_EXPERTS - 1].set(n_blocks * MOE_BLOCK)
    return dest, blk_expert, n_active, n_blocks, pad_lo, pad_hi


def kernel(x, c, w_ada, b_ada, w_in, dn_conv_w, dn_a_log, dn_dt_bias, dn_norm_w, fox_f_bias, w_out,
           ln1_g, ln1_b, w_router_group, b_router_group, w_router_expert, b_router_expert,
           w_gate, w_up, w_down, ln2_g, ln2_b):
    bsz, seq, _ = x.shape
    n = bsz * seq
    x2 = x.reshape(n, D_MODEL)

    mod = _ada(c, w_ada[0], b_ada[0])
    sh1, sc1, g1, sh2, sc2, g2 = [m.reshape(bsz, 1, D_MODEL) for m in jnp.split(mod, 6, axis=-1)]

    wi = w_in[0]
    o_b = 4 * DN_WIDTH
    o_a = o_b + DN_HEADS
    o_fq = o_a + DN_HEADS
    o_f = o_fq + 3 * FOX_WIDTH
    w_main = jnp.concatenate([wi[:, :o_b], wi[:, o_fq:o_f]], axis=1).astype(BF16)
    w_small = jnp.concatenate([wi[:, o_b:o_fq], wi[:, o_f:],
                               jnp.zeros((D_MODEL, LANES - 2 * DN_HEADS - FOX_HEADS), F32)],
                              axis=1).astype(BF16)
    zeros8 = jnp.zeros((DN_HEADS,), F32)
    pad = jnp.zeros((LANES - 2 * DN_HEADS - FOX_HEADS,), F32)
    gate_par = jnp.stack([jnp.concatenate([zeros8, dn_dt_bias[0], fox_f_bias[0], pad]),
                          jnp.concatenate([zeros8, dn_a_log[0], zeros8, pad])])

    main, gates = _inproj(x2, sc1, sh1, w_main, w_small, gate_par, seq)
    main3 = main.reshape(bsz, seq, D_MAIN)
    gates3 = gates.reshape(bsz, seq, LANES)
    gc3, fk_aug = _cumsum(gates3)
    gct = jnp.swapaxes(gc3[:, :, DN_HEADS:2 * DN_HEADS], 1, 2)

    o_dn = _deltanet(main3, gates3, gc3, gct, dn_conv_w[0], dn_norm_w[0])
    o_fox = _fox(main3, fk_aug)

    wo = w_out[0].astype(BF16)
    wr_t = jnp.concatenate([w_router_group[0].T, jnp.zeros((8 - N_GROUPS, D_MODEL), F32),
                            jnp.transpose(w_router_expert[0], (0, 2, 1)).reshape(N_EXPERTS, D_MODEL)],
                           axis=0)
    br_t = jnp.concatenate([b_router_group[0], jnp.zeros((8 - N_GROUPS,), F32),
                            b_router_expert[0].reshape(-1)])
    br_col = jnp.pad(br_t, (0, LANES - ROUTE_ROWS)).reshape(LANES, 1)
    wr_hi = wr_t.astype(BF16)
    wr_lo = (wr_t - wr_hi.astype(F32)).astype(BF16)
    lane_pad = ((0, 0), (0, LANES - ROUTE_ROWS))
    wr_split = jnp.concatenate([jnp.pad(wr_hi.T, lane_pad), jnp.pad(wr_lo.T, lane_pad)], axis=1)
    x1, h2p, route, counts = _outproj(o_dn.reshape(n, DN_WIDTH), o_fox.reshape(n, FOX_WIDTH),
                                      wo[:DN_WIDTH], wo[DN_WIDTH:], x2, g1, sc2, sh2, ln1_g[0], ln1_b[0],
                                      wr_split, br_col, seq)

    dest, blk_expert, n_active, n_blocks, pad_lo, pad_hi = _dispatch_plan(route, counts, n)
    tok_buf = _invert(dest.reshape(-1), pad_lo, pad_hi, n_blocks * MOE_BLOCK)
    y_buf = _moe(blk_expert, n_active, tok_buf, h2p,
                 w_gate[0].astype(BF16), w_up[0].astype(BF16), w_down[0].astype(BF16))

    tm = min(256, seq)
    dest_tiled = (dest * TOKEN_TILE).reshape(TOP_K, n // tm, tm).transpose(1, 0, 2).reshape(-1)
    wt = jnp.pad(route[2:4].T, ((0, 0), (0, LANES - TOP_K)))
    out = _combine(dest_tiled, y_buf, x1, wt, g2, ln2_g[0], ln2_b[0], seq, tm)
    return out.reshape(bsz, seq, D_MODEL)
```

```python
import functools
import math

import jax
import jax.numpy as jnp
from jax import lax
from jax.experimental import pallas as pl
from jax.experimental.pallas import tpu as pltpu

F32 = jnp.float32
BF16 = jnp.bfloat16
U32 = jnp.uint32
HIGHEST = lax.Precision.HIGHEST

D_MODEL = 2048
HEAD_DIM = 128
DN_HEADS = 8
FOX_HEADS = 8
DN_WIDTH = DN_HEADS * HEAD_DIM
FOX_WIDTH = FOX_HEADS * HEAD_DIM
CONV_WIDTH = 4
N_GROUPS = 4
EXPERTS_PER_GROUP = 8
N_EXPERTS = N_GROUPS * EXPERTS_PER_GROUP
TOP_K = 2
D_EXPERT = 512
MOE_BLOCK = 256
LN_EPS = 1e-5
NORM_EPS = 1e-6
ALPHA = 2.0 ** 0.25

D_MAIN = 3 * DN_WIDTH + DN_WIDTH + 3 * FOX_WIDTH
LANES = 128
BF16_SUBLANES = 16
DN_CHUNK = 128
NEG_BIG = -1e30
LOG2E = 1.4426950408889634
HALF = D_MODEL // 2
VMEM_LIMIT = 56 * 1024 * 1024


def _sigmoid(x):
    return 1.0 / (1.0 + jnp.exp(-x))


def _softplus(x):
    return jnp.maximum(x, 0.0) + jnp.log(1.0 + jnp.exp(-jnp.abs(x)))


def _mm(a, b):
    return jnp.dot(a.astype(BF16), b.astype(BF16), preferred_element_type=F32)


def _mm_nt(a, b):
    return lax.dot_general(a.astype(BF16), b.astype(BF16), (((1,), (1,)), ((), ())),
                           preferred_element_type=F32)


def _mm_tn(a, b):
    return lax.dot_general(a.astype(BF16), b.astype(BF16), (((0,), (0,)), ((), ())),
                           preferred_element_type=F32)


def _pack_halves(x):
    c = x.shape[1] // 2
    lo = pltpu.bitcast(x[:, :c].astype(BF16).astype(F32), U32)
    hi = pltpu.bitcast(x[:, c:].astype(BF16).astype(F32), U32)
    return (lo >> 16) | (hi & jnp.uint32(0xFFFF0000))


def _unpack_halves(p):
    lo = pltpu.bitcast(p << 16, F32)
    hi = pltpu.bitcast(p & jnp.uint32(0xFFFF0000), F32)
    return lo, hi


def _layer_norm(r, g, b):
    mu = jnp.mean(r, axis=-1, keepdims=True)
    d = r - mu
    var = jnp.mean(d * d, axis=-1, keepdims=True)
    return d * lax.rsqrt(var + LN_EPS) * g + b


def _ada_kernel(c_ref, w_ref, b_ref, o_ref):
    c = c_ref[...]
    a = c * _sigmoid(c)
    o_ref[...] = jnp.dot(a, w_ref[...], preferred_element_type=F32, precision=HIGHEST) + b_ref[...]


def _ada(c, w_ada, b_ada):
    bsz = c.shape[0]
    n_out = w_ada.shape[1]
    tn = 1024
    return pl.pallas_call(
        _ada_kernel,
        grid=(n_out // tn,),
        in_specs=[pl.BlockSpec((bsz, D_MODEL), lambda j: (0, 0)),
                  pl.BlockSpec((D_MODEL, tn), lambda j: (0, j)),
                  pl.BlockSpec((1, tn), lambda j: (0, j))],
        out_specs=pl.BlockSpec((bsz, tn), lambda j: (0, j)),
        out_shape=jax.ShapeDtypeStruct((bsz, n_out), F32),
        compiler_params=pltpu.CompilerParams(dimension_semantics=("arbitrary",),
                                             vmem_limit_bytes=VMEM_LIMIT),
    )(c, w_ada, b_ada.reshape(1, n_out))


def _inproj_kernel(x_ref, sc_ref, sh_ref, w_ref, ws_ref, par_ref, o_ref, g_ref, h_scr):
    @pl.when(pl.program_id(1) == 0)
    def _():
        h = (x_ref[...] * (1.0 + sc_ref[0]) + sh_ref[0]).astype(BF16)
        h_scr[...] = h
        s = jnp.dot(h, ws_ref[...], preferred_element_type=F32)
        lane = lax.broadcasted_iota(jnp.int32, s.shape, 1)
        sb = s + par_ref[0:1, :]
        neg_a = -jnp.exp(par_ref[1:2, :])
        beta = _sigmoid(s)
        g = neg_a * _softplus(sb)
        log_f = -_softplus(-sb)
        g_ref[...] = jnp.where(lane < DN_HEADS, beta,
                               jnp.where(lane < 2 * DN_HEADS, g,
                                         jnp.where(lane < 2 * DN_HEADS + FOX_HEADS, log_f, 0.0)))

    o_ref[...] = jnp.dot(h_scr[...], w_ref[...], preferred_element_type=F32).astype(BF16)


def _inproj(x2, sc1, sh1, w_main, w_small, gate_par, seq):
    n = x2.shape[0]
    tm = min(1024, seq)
    tn = D_MAIN // 4
    per_batch = seq // tm
    return pl.pallas_call(
        _inproj_kernel,
        grid=(n // tm, D_MAIN // tn),
        in_specs=[pl.BlockSpec((tm, D_MODEL), lambda i, j: (i, 0)),
                  pl.BlockSpec((1, 1, D_MODEL), lambda i, j: (i // per_batch, 0, 0)),
                  pl.BlockSpec((1, 1, D_MODEL), lambda i, j: (i // per_batch, 0, 0)),
                  pl.BlockSpec((D_MODEL, tn), lambda i, j: (0, j)),
                  pl.BlockSpec((D_MODEL, LANES), lambda i, j: (0, 0)),
                  pl.BlockSpec((2, LANES), lambda i, j: (0, 0))],
        out_specs=[pl.BlockSpec((tm, tn), lambda i, j: (i, j)),
                   pl.BlockSpec((tm, LANES), lambda i, j: (i, 0))],
        out_shape=[jax.ShapeDtypeStruct((n, D_MAIN), BF16),
                   jax.ShapeDtypeStruct((n, LANES), F32)],
        scratch_shapes=[pltpu.VMEM((tm, D_MODEL), BF16)],
        compiler_params=pltpu.CompilerParams(dimension_semantics=("arbitrary", "arbitrary"),
                                             vmem_limit_bytes=VMEM_LIMIT),
    )(x2, sc1, sh1, w_main, w_small, gate_par)


def _cumsum_kernel(g_ref, gc_ref, fk_ref, carry, tri_scr):
    first = jnp.logical_and(pl.program_id(0) == 0, pl.program_id(1) == 0)

    @pl.when(pl.program_id(1) == 0)
    def _():
        carry[...] = jnp.zeros_like(carry)

    x = g_ref[0]
    t = x.shape[0]

    @pl.when(first)
    def _():
        row = lax.broadcasted_iota(jnp.int32, (t, t), 0)
        col = lax.broadcasted_iota(jnp.int32, (t, t), 1)
        shift = int(math.log2(DN_CHUNK))
        tri_scr[:t, :] = jnp.where(col <= row, 1.0, 0.0).astype(BF16)
        tri_scr[t:, :] = jnp.where(jnp.logical_and(col <= row, (row >> shift) == (col >> shift)),
                                   1.0, 0.0).astype(BF16)

    hi = x.astype(BF16)
    r1 = x - hi.astype(F32)
    mid = r1.astype(BF16)
    lo = (r1 - mid.astype(F32)).astype(BF16)
    parts = jnp.dot(tri_scr[...], jnp.concatenate([hi, mid, lo], axis=1), preferred_element_type=F32)
    sums = parts[:, :LANES] + parts[:, LANES:2 * LANES] + parts[:, 2 * LANES:]
    cs = sums[:t]
    f = cs + carry[...]
    gc_ref[0] = sums[t:]
    carry[...] = carry[...] + cs[t - 1:t, :]

    lane = lax.broadcasted_iota(jnp.int32, (t, LANES), 1)
    for h in range(FOX_HEADS):
        c0 = 2 * DN_HEADS + h
        val = jnp.broadcast_to(f[:, c0:c0 + 1] * (-LOG2E), (t, LANES))
        hi = val.astype(BF16)
        r1 = val - hi.astype(F32)
        mid = r1.astype(BF16)
        lo = r1 - mid.astype(F32)
        fk_ref[0, :, h * LANES:(h + 1) * LANES] = jnp.where(
            lane == 0, hi.astype(F32),
            jnp.where(lane == 1, mid.astype(F32), jnp.where(lane == 2, lo, 0.0))).astype(BF16)


def _cumsum(gates3):
    bsz, seq, _ = gates3.shape
    t = min(512, seq)
    spec = pl.BlockSpec((1, t, LANES), lambda b, i: (b, i, 0))
    return pl.pallas_call(
        _cumsum_kernel,
        grid=(bsz, seq // t),
        in_specs=[spec],
        out_specs=[spec, pl.BlockSpec((1, t, FOX_HEADS * LANES), lambda b, i: (b, i, 0))],
        out_shape=[jax.ShapeDtypeStruct(gates3.shape, F32),
                   jax.ShapeDtypeStruct((bsz, seq, FOX_HEADS * LANES), BF16)],
        scratch_shapes=[pltpu.VMEM((1, LANES), F32), pltpu.VMEM((2 * t, t), BF16)],
        compiler_params=pltpu.CompilerParams(dimension_semantics=("arbitrary", "arbitrary")),
    )(gates3)


DN_CHUNKS_PER_STEP = 4


def _dn_kernel(qkv_ref, halo_ref, z_ref, gate_ref, gc_ref, gct_ref, convw_ref, normw_ref,
               o_ref, state_ref, xs_scr):
    t = pl.program_id(1)

    @pl.when(t == 0)
    def _():
        state_ref[...] = jnp.zeros_like(state_ref)

    c = DN_CHUNK
    d = HEAD_DIM
    heads = range(DN_HEADS)
    row = lax.broadcasted_iota(jnp.int32, (c, c), 0)
    col = lax.broadcasted_iota(jnp.int32, (c, c), 1)
    strict = col < row
    eye = (row == col).astype(F32)
    neg_diag8_f = jnp.where((row >> 3) == (col >> 3), -1.0, 0.0)
    merge_levels = range(3, int(math.log2(c)))
    low_f = [jnp.where((row >> (s + 1)) == (col >> (s + 1)),
                       jnp.where((row >> s) == (col >> s), 0.0, 1.0), 0.0) for s in merge_levels]
    halo_on = (t > 0).astype(F32)
    normw = normw_ref[...]

    def l2n(x):
        return x * lax.rsqrt(jnp.sum(x * x, axis=-1, keepdims=True) + NORM_EPS)

    def bdot(a, b):
        return jnp.dot(a, b, preferred_element_type=F32)

    def conv_head(ci, h):
        r0 = ci * c

        def conv_silu(part):
            off = part * DN_WIDTH + h * d
            slab = (ci * 3 + part) * DN_HEADS + h
            x = qkv_ref[0, r0:r0 + c, off:off + d].astype(F32)
            if ci == 0:
                prev = halo_ref[0, BF16_SUBLANES - 8:, off:off + d].astype(F32) * halo_on
            else:
                prev = qkv_ref[0, r0 - 8:r0, off:off + d].astype(F32)
            xs_scr[slab, 0:8, :] = prev
            xs_scr[slab, 8:, :] = x
            w = convw_ref[:, off:off + d]
            acc = x * w[CONV_WIDTH - 1:CONV_WIDTH]
            for k in range(1, CONV_WIDTH):
                acc = acc + xs_scr[slab, pl.ds(8 - k, c), :] * w[CONV_WIDTH - 1 - k:CONV_WIDTH - k]
            return acc * _sigmoid(acc)

        return l2n(conv_silu(0)) * (d ** -0.5), l2n(conv_silu(1)), conv_silu(2)

    def chunk_steps(ci, qkv):
        r0 = ci * c
        q = [x[0] for x in qkv]
        k = [x[1] for x in qkv]
        v = [x[2] for x in qkv]
        gates = gate_ref[0, r0:r0 + c, :]
        gcs = gc_ref[0, r0:r0 + c, :]
        gct = gct_ref[0, :, r0:r0 + c]
        beta = [gates[:, h:h + 1] for h in heads]
        g_col = [gcs[:, DN_HEADS + h:DN_HEADS + h + 1] for h in heads]
        e_g = [jnp.exp(g) for g in g_col]
        g_last = [g[c - 1:c, :] for g in g_col]
        decay = [jnp.exp(jnp.where(strict, g_col[h] - gct[h:h + 1, :], NEG_BIG)) for h in heads]
        kb = [k[h] * beta[h] for h in heads]

        kk = [_mm_nt(jnp.concatenate([kb[h], q[h]], axis=0), k[h]) for h in heads]
        yield
        a = [kk[h][:c] * decay[h] for h in heads]
        qk = [(kk[h][c:] * (decay[h] + eye)).astype(BF16) for h in heads]
        n1 = [a[h] * neg_diag8_f for h in heads]
        n1b = [n.astype(BF16) for n in n1]
        x = [eye + n for n in n1]
        xb = [xx.astype(BF16) for xx in x]
        n2b = [bdot(n, n).astype(BF16) for n in n1b]
        yield
        x = [x[h] + bdot(n2b[h], xb[h]) for h in heads]
        n4b = [bdot(n, n).astype(BF16) for n in n2b]
        yield
        xb = [xx.astype(BF16) for xx in x]
        x = [x[h] + bdot(n4b[h], xb[h]) for h in heads]
        yield
        for lf in low_f:
            xb = [xx.astype(BF16) for xx in x]
            lx = [bdot((a[h] * lf).astype(BF16), xb[h]).astype(BF16) for h in heads]
            yield
            x = [x[h] - bdot(xb[h], lx[h]) for h in heads]
            yield

        uw = [_mm(x[h], jnp.concatenate([v[h] * beta[h], kb[h] * e_g[h]], axis=1)) for h in heads]
        yield
        s_old = [state_ref[h] for h in heads]
        r = [_mm(jnp.concatenate([uw[h][:, d:], q[h] * e_g[h]], axis=0), s_old[h]) for h in heads]
        yield
        v_new = [(uw[h][:, :d] - r[h][:c]).astype(BF16) for h in heads]
        o = [r[h][c:] + bdot(qk[h], v_new[h]) for h in heads]
        for h in heads:
            kd = k[h] * jnp.exp(g_last[h] - g_col[h])
            state_ref[h] = s_old[h] * jnp.exp(g_last[h]) + _mm_tn(kd, v_new[h])
        yield
        for h in heads:
            z = z_ref[0, r0:r0 + c, h * d:(h + 1) * d].astype(F32)
            rms = lax.rsqrt(jnp.mean(o[h] * o[h], axis=-1, keepdims=True) + NORM_EPS)
            o_ref[0, r0:r0 + c, h * d:(h + 1) * d] = (o[h] * rms * normw * (z * _sigmoid(z))).astype(BF16)

    qkv = [conv_head(0, h) for h in heads]
    for ci in range(DN_CHUNKS_PER_STEP):
        following = []
        more = ci + 1 < DN_CHUNKS_PER_STEP
        for _ in chunk_steps(ci, qkv):
            if more and len(following) < DN_HEADS:
                following.append(conv_head(ci + 1, len(following)))
        while more and len(following) < DN_HEADS:
            following.append(conv_head(ci + 1, len(following)))
        qkv = following


def _deltanet(main3, gates3, gc3, gct, conv_w, norm_w):
    bsz, seq, _ = main3.shape
    rows = DN_CHUNK * DN_CHUNKS_PER_STEP
    halo_blocks = rows // BF16_SUBLANES
    return pl.pallas_call(
        _dn_kernel,
        grid=(bsz, seq // rows),
        in_specs=[pl.BlockSpec((1, rows, 3 * DN_WIDTH), lambda b, t: (b, t, 0)),
                  pl.BlockSpec((1, BF16_SUBLANES, 3 * DN_WIDTH),
                               lambda b, t: (b, jnp.maximum(t * halo_blocks - 1, 0), 0)),
                  pl.BlockSpec((1, rows, DN_WIDTH), lambda b, t: (b, t, 3)),
                  pl.BlockSpec((1, rows, LANES), lambda b, t: (b, t, 0)),
                  pl.BlockSpec((1, rows, LANES), lambda b, t: (b, t, 0)),
                  pl.BlockSpec((1, DN_HEADS, rows), lambda b, t: (b, 0, t)),
                  pl.BlockSpec((CONV_WIDTH, 3 * DN_WIDTH), lambda b, t: (0, 0)),
                  pl.BlockSpec((1, HEAD_DIM), lambda b, t: (0, 0))],
        out_specs=pl.BlockSpec((1, rows, DN_WIDTH), lambda b, t: (b, t, 0)),
        out_shape=jax.ShapeDtypeStruct((bsz, seq, DN_WIDTH), BF16),
        scratch_shapes=[pltpu.VMEM((DN_HEADS, HEAD_DIM, HEAD_DIM), F32),
                        pltpu.VMEM((3 * DN_HEADS * DN_CHUNKS_PER_STEP, 8 + DN_CHUNK, HEAD_DIM), F32)],
        compiler_params=pltpu.CompilerParams(dimension_semantics=("arbitrary", "arbitrary"),
                                             vmem_limit_bytes=VMEM_LIMIT),
    )(main3, main3, main3, gates3, gc3, gct, conv_w, norm_w.reshape(1, HEAD_DIM))


def _fox_kernel(qi_ref, kj_ref, q_ref, k_ref, v_ref, fk_ref, o_ref, qa_scr, m_scr, l_scr, acc_scr):
    step = pl.program_id(1)
    i = qi_ref[step]
    j = kj_ref[step]
    d = HEAD_DIM
    tq = q_ref.shape[1]
    tk = k_ref.shape[1]

    @pl.when(j == 0)
    def _():
        m_scr[...] = jnp.full_like(m_scr, NEG_BIG)
        l_scr[...] = jnp.zeros_like(l_scr)
        acc_scr[...] = jnp.zeros_like(acc_scr)
        lane = lax.broadcasted_iota(jnp.int32, (tq, LANES), 1)
        ones = jnp.where(lane < 3, 1.0, 0.0).astype(BF16)
        for h in range(FOX_HEADS):
            q = (q_ref[0, :, h * d:(h + 1) * d].astype(F32) * (d ** -0.5 * LOG2E)).astype(BF16)
            qa_scr[h] = jnp.concatenate([q, ones], axis=1)

    def update(diagonal):
        if diagonal:
            key_pos = lax.broadcasted_iota(jnp.int32, (tk, tq), 0)
            q_pos = lax.broadcasted_iota(jnp.int32, (tk, tq), 1)
            visible = key_pos <= q_pos

        def scores(h):
            k_aug = jnp.concatenate([k_ref[0, :, h * d:(h + 1) * d],
                                     fk_ref[0, :, h * LANES:(h + 1) * LANES]], axis=1)
            return lax.dot_general(k_aug, qa_scr[h], (((1,), (1,)), ((), ())),
                                   preferred_element_type=F32)

        heads = range(FOX_HEADS)
        s = [scores(h) for h in heads]
        if diagonal:
            s = [jnp.where(visible, x, NEG_BIG) for x in s]
        m_old = [m_scr[h] for h in heads]
        m_new = [jnp.maximum(m_old[h], jnp.max(s[h], axis=0, keepdims=True)) for h in heads]
        alpha = [jnp.exp2(m_old[h] - m_new[h]) for h in heads]
        p = [jnp.exp2(s[h] - m_new[h]) for h in heads]
        for h in heads:
            l_scr[h] = alpha[h] * l_scr[h] + jnp.sum(p[h], axis=0, keepdims=True)
            acc_scr[h] = alpha[h] * acc_scr[h] + _mm_tn(v_ref[0, :, h * d:(h + 1) * d], p[h])
            m_scr[h] = m_new[h]

    @pl.when(j < i)
    def _():
        update(False)

    @pl.when(j == i)
    def _():
        update(True)
        for h in range(FOX_HEADS):
            o = acc_scr[h] / l_scr[h]
            o_ref[0, :, h * d:(h + 1) * d] = o.T.astype(BF16)


def _fox(main3, fk_aug):
    bsz, seq, _ = main3.shape
    tq = min(512, seq)
    n_blk = seq // tq
    w = FOX_WIDTH
    base = (4 * DN_WIDTH) // w
    pairs = [(i, j) for i in range(n_blk) for j in range(i + 1)]
    qi = jnp.array([p[0] for p in pairs], jnp.int32)
    kj = jnp.array([p[1] for p in pairs], jnp.int32)
    grid_spec = pltpu.PrefetchScalarGridSpec(
        num_scalar_prefetch=2,
        grid=(bsz, len(pairs)),
        in_specs=[pl.BlockSpec((1, tq, w), lambda b, t, qi, kj: (b, qi[t], base)),
                  pl.BlockSpec((1, tq, w), lambda b, t, qi, kj: (b, kj[t], base + 1)),
                  pl.BlockSpec((1, tq, w), lambda b, t, qi, kj: (b, kj[t], base + 2)),
                  pl.BlockSpec((1, tq, FOX_HEADS * LANES), lambda b, t, qi, kj: (b, kj[t], 0))],
        out_specs=pl.BlockSpec((1, tq, w), lambda b, t, qi, kj: (b, qi[t], 0)),
        scratch_shapes=[pltpu.VMEM((FOX_HEADS, tq, 2 * HEAD_DIM), BF16),
                        pltpu.VMEM((FOX_HEADS, 1, tq), F32),
                        pltpu.VMEM((FOX_HEADS, 1, tq), F32),
                        pltpu.VMEM((FOX_HEADS, HEAD_DIM, tq), F32)])
    return pl.pallas_call(
        _fox_kernel,
        grid_spec=grid_spec,
        out_shape=jax.ShapeDtypeStruct((bsz, seq, w), BF16),
        compiler_params=pltpu.CompilerParams(
            dimension_semantics=("arbitrary", "arbitrary"),
            vmem_limit_bytes=VMEM_LIMIT),
    )(qi, kj, main3, main3, main3, fk_aug)


ROUTE_ROWS = 8 + N_EXPERTS
TOKEN_TILE = 8


def _store_token_tiles(ref, packed, first_token=0):
    m = packed.shape[0]
    for s in range(TOKEN_TILE):
        ref[pl.ds(first_token * TOKEN_TILE + s, m, stride=TOKEN_TILE), :] = packed[:, s * LANES:(s + 1) * LANES]


def _load_token_tiles(ref, start, m):
    return [ref[pl.ds(start + s, m, stride=TOKEN_TILE), :] for s in range(TOKEN_TILE)]


def _outproj_kernel(odn_ref, ofox_ref, w1_ref, w2_ref, x_ref, g1_ref, sc2_ref, sh2_ref,
                    lng_ref, lnb_ref, wr_ref, br_ref, x1_ref, h2p_ref, route_ref, count_ref,
                    upper_scr, carry_scr):
    tm = x_ref.shape[0]

    @pl.when(pl.program_id(0) == 0)
    def _():
        r = lax.broadcasted_iota(jnp.int32, (tm, tm), 0)
        c = lax.broadcasted_iota(jnp.int32, (tm, tm), 1)
        upper_scr[...] = jnp.where(r < c, 1.0, 0.0).astype(BF16)
        carry_scr[...] = jnp.zeros_like(carry_scr)

    y = (jnp.dot(odn_ref[...], w1_ref[...], preferred_element_type=F32)
         + jnp.dot(ofox_ref[...], w2_ref[...], preferred_element_type=F32))
    _outproj_tail(0, tm, y, x_ref, g1_ref, sc2_ref, sh2_ref, lng_ref, lnb_ref, wr_ref, br_ref,
                  x1_ref, h2p_ref, route_ref, upper_scr, carry_scr)
    count_ref[...] = jnp.broadcast_to(carry_scr[...], count_ref.shape)


def _outproj_tail(u, tm, y, x_ref, g1_ref, sc2_ref, sh2_ref, lng_ref, lnb_ref, wr_ref, br_ref,
                  x1_ref, h2p_ref, route_ref, upper_scr, carry_scr):
    rows = slice(u * tm, (u + 1) * tm)
    r = ALPHA * x_ref[rows, :] + (1.0 + g1_ref[0]) * y
    x1 = _layer_norm(r, lng_ref[...], lnb_ref[...])
    x1_ref[rows, :] = x1
    h2 = x1 * (1.0 + sc2_ref[0]) + sh2_ref[0]
    _store_token_tiles(h2p_ref, _pack_halves(h2), u * tm)

    h_hi = h2.astype(BF16)
    h_lo = (h2 - h_hi.astype(F32)).astype(BF16)
    parts = jnp.dot(jnp.concatenate([h_hi, h_lo], axis=0), wr_ref[...], preferred_element_type=F32)
    terms = parts[:tm] + parts[tm:]
    lt = (terms[:, :LANES] + terms[:, LANES:]).T + br_ref[...]
    row = [lt[n:n + 1, :] for n in range(ROUTE_ROWS)]

    def first_max(vals):
        best = vals[0]
        for val in vals[1:]:
            best = jnp.maximum(best, val)
        idx = jnp.full(best.shape, len(vals) - 1, jnp.int32)
        for n in range(len(vals) - 2, -1, -1):
            idx = jnp.where(vals[n] == best, n, idx)
        return best, idx

    gmax, gidx = first_max(row[:N_GROUPS])
    gsum = jnp.exp(row[0] - gmax)
    for n in range(1, N_GROUPS):
        gsum = gsum + jnp.exp(row[n] - gmax)
    g_w = 1.0 / gsum
    el = []
    for e in range(EXPERTS_PER_GROUP):
        val = row[8 + e]
        for g in range(1, N_GROUPS):
            val = jnp.where(gidx == g, row[8 + EXPERTS_PER_GROUP * g + e], val)
        el.append(val)
    emax, _ = first_max(el)
    ee = [jnp.exp(val - emax) for val in el]
    esum = ee[0]
    for val in ee[1:]:
        esum = esum + val
    prob = [val / esum for val in ee]
    p1, e1 = first_max(prob)
    p2, e2 = first_max([jnp.where(e1 == e, -1.0, prob[e]) for e in range(EXPERTS_PER_GROUP)])
    denom = p1 + p2
    ex1 = gidx * EXPERTS_PER_GROUP + e1
    ex2 = gidx * EXPERTS_PER_GROUP + e2
    ridx = lax.broadcasted_iota(jnp.int32, (8, tm), 0)

    eidx = lax.broadcasted_iota(jnp.int32, (N_EXPERTS, tm), 0)
    oh1 = jnp.where(eidx == ex1, 1.0, 0.0)
    oh2 = jnp.where(eidx == ex2, 1.0, 0.0)
    both = oh1 + oh2
    before = jnp.dot(both.astype(BF16), upper_scr[:tm, :tm], preferred_element_type=F32) + carry_scr[...]
    rank1 = jnp.sum(oh1 * before, axis=0, keepdims=True)
    rank2 = jnp.sum(oh2 * before, axis=0, keepdims=True)
    carry_scr[...] = carry_scr[...] + jnp.sum(both, axis=1, keepdims=True)

    fields = [ex1.astype(F32), ex2.astype(F32), g_w * p1 / denom, g_w * p2 / denom, rank1, rank2]
    out = jnp.zeros((8, tm), F32)
    for n, val in enumerate(fields):
        out = jnp.where(ridx == n, val, out)
    route_ref[:, rows] = out


def _outproj(o_dn, o_fox, w1, w2, x2, g1, sc2, sh2, ln_g, ln_b, wr_t, br_t, seq):
    n = x2.shape[0]
    tm = min(512, seq)
    per_batch = seq // tm
    row = lambda i: (i, 0)
    fixed = lambda i: (0, 0)
    mod = pl.BlockSpec((1, 1, D_MODEL), lambda i: (i // per_batch, 0, 0))
    return pl.pallas_call(
        _outproj_kernel,
        grid=(n // tm,),
        in_specs=[pl.BlockSpec((tm, DN_WIDTH), row),
                  pl.BlockSpec((tm, FOX_WIDTH), row),
                  pl.BlockSpec((DN_WIDTH, D_MODEL), fixed),
                  pl.BlockSpec((FOX_WIDTH, D_MODEL), fixed),
                  pl.BlockSpec((tm, D_MODEL), row),
                  mod, mod, mod,
                  pl.BlockSpec((1, D_MODEL), fixed),
                  pl.BlockSpec((1, D_MODEL), fixed),
                  pl.BlockSpec((D_MODEL, 2 * LANES), fixed),
                  pl.BlockSpec((LANES, 1), fixed)],
        out_specs=[pl.BlockSpec((tm, D_MODEL), row),
                   pl.BlockSpec((tm * TOKEN_TILE, LANES), row),
                   pl.BlockSpec((8, tm), lambda i: (0, i)),
                   pl.BlockSpec((N_EXPERTS, LANES), fixed)],
        out_shape=[jax.ShapeDtypeStruct((n, D_MODEL), F32),
                   jax.ShapeDtypeStruct((n * TOKEN_TILE, LANES), U32),
                   jax.ShapeDtypeStruct((8, n), F32),
                   jax.ShapeDtypeStruct((N_EXPERTS, LANES), F32)],
        scratch_shapes=[pltpu.VMEM((tm, tm), BF16),
                        pltpu.VMEM((N_EXPERTS, 1), F32)],
        compiler_params=pltpu.CompilerParams(dimension_semantics=("arbitrary",),
                                             vmem_limit_bytes=VMEM_LIMIT),
    )(o_dn, o_fox, w1, w2, x2, g1, sc2, sh2, ln_g.reshape(1, D_MODEL), ln_b.reshape(1, D_MODEL),
      wr_t, br_t)


INVERT_STEPS = N_EXPERTS
INVERT_BATCH = 16


def _invert_kernel(dest_ref, pad_lo_ref, pad_hi_ref, tok_ref):
    phase = pl.program_id(0)
    i = pl.program_id(1)
    n_tok = dest_ref.shape[0] // TOP_K
    toks_per = n_tok // INVERT_STEPS

    @pl.when(phase == 0)
    def _():
        def clear(p, carry):
            tok_ref[p] = 0
            return carry
        lax.fori_loop(pad_lo_ref[i], pad_hi_ref[i], clear, 0)

    @pl.when(phase == 1)
    def _():
        def put(t, carry):
            first = i * toks_per + t * INVERT_BATCH
            rows = [[dest_ref[k * n_tok + first + u] for k in range(TOP_K)] for u in range(INVERT_BATCH)]
            for u in range(INVERT_BATCH):
                for k in range(TOP_K):
                    tok_ref[rows[u][k]] = (first + u) * TOKEN_TILE
            return carry
        lax.fori_loop(0, toks_per // INVERT_BATCH, put, 0)


def _invert(dest_flat, pad_lo, pad_hi, n_slots):
    assert (dest_flat.shape[0] // TOP_K) % (INVERT_STEPS * INVERT_BATCH) == 0
    smem = pl.BlockSpec(memory_space=pltpu.SMEM)
    return pl.pallas_call(
        _invert_kernel,
        grid=(2, INVERT_STEPS),
        in_specs=[smem, smem, smem],
        out_specs=smem,
        out_shape=jax.ShapeDtypeStruct((n_slots,), jnp.int32),
        compiler_params=pltpu.CompilerParams(dimension_semantics=("arbitrary", "arbitrary")),
    )(dest_flat, pad_lo, pad_hi)


def _tile_gather_start(idx_ref, base, src_hbm, dst, dst_row0, sem, count):
    for r in range(count):
        row = pl.multiple_of(idx_ref[base + r], TOKEN_TILE)
        pltpu.make_async_copy(src_hbm.at[pl.ds(row, TOKEN_TILE)],
                              dst.at[pl.ds(dst_row0 + r * TOKEN_TILE, TOKEN_TILE)],
                              sem).start(priority=r % 2)


def _tile_gather_wait(src_hbm, dst, dst_row0, sem, count):
    rows = count * TOKEN_TILE
    pltpu.make_async_copy(src_hbm.at[pl.ds(0, rows)], dst.at[pl.ds(dst_row0, rows)], sem).wait()


GATHER_DEPTH = 3


def _moe_kernel(blk_expert_ref, n_active_ref, tok_ref, h_hbm, wg_ref, wu_ref, wd_ref,
                y_ref, xbuf0, xbuf1, xbuf2, sem):
    b = pl.program_id(0)
    nb = pl.num_programs(0)
    n_active = n_active_ref[0]
    last = jnp.maximum(jnp.minimum(n_active, nb), 1) - 1
    bufs = (xbuf0, xbuf1, xbuf2)

    def fetch(blk, c):
        _tile_gather_start(tok_ref, blk * MOE_BLOCK, h_hbm, bufs[c], 0, sem.at[c], MOE_BLOCK)

    def drain(c):
        _tile_gather_wait(h_hbm, bufs[c], 0, sem.at[c], MOE_BLOCK)

    @pl.when(b == 0)
    def _():
        fetch(0, 0)
        fetch(jnp.minimum(1, last), 1)

    def block(c):
        drain(c)
        parts = [_unpack_halves(p) for p in _load_token_tiles(bufs[c], 0, MOE_BLOCK)]
        lo = jnp.concatenate([p[0] for p in parts], axis=1).astype(BF16)
        hi = jnp.concatenate([p[1] for p in parts], axis=1).astype(BF16)
        fetch(jnp.minimum(b + 2, last), (c + 2) % GATHER_DEPTH)
        gate = (jnp.dot(lo, wg_ref[0, :HALF, :], preferred_element_type=F32)
                + jnp.dot(hi, wg_ref[0, HALF:, :], preferred_element_type=F32))
        up = (jnp.dot(lo, wu_ref[0, :HALF, :], preferred_element_type=F32)
              + jnp.dot(hi, wu_ref[0, HALF:, :], preferred_element_type=F32))
        hid = (gate * _sigmoid(gate) * up).astype(BF16)
        y = jnp.dot(hid, wd_ref[0], preferred_element_type=F32)
        _store_token_tiles(y_ref, _pack_halves(y))

        @pl.when(b == last)
        def _():
            drain((c + 1) % GATHER_DEPTH)
            drain((c + 2) % GATHER_DEPTH)

    for c in range(GATHER_DEPTH):
        pl.when(jnp.logical_and(b <= last, lax.rem(b, GATHER_DEPTH) == c))(functools.partial(block, c))

    @pl.when(b > last)
    def _():
        y_ref[...] = jnp.zeros_like(y_ref)


def _moe(blk_expert, n_active, tok_buf, h2p, wg, wu, wd):
    n_blocks = blk_expert.shape[0]
    buf_rows = MOE_BLOCK * TOKEN_TILE
    grid_spec = pltpu.PrefetchScalarGridSpec(
        num_scalar_prefetch=3,
        grid=(n_blocks,),
        in_specs=[pl.BlockSpec(memory_space=pl.ANY),
                  pl.BlockSpec((1, D_MODEL, D_EXPERT), lambda b, be, na, tk: (be[b], 0, 0)),
                  pl.BlockSpec((1, D_MODEL, D_EXPERT), lambda b, be, na, tk: (be[b], 0, 0)),
                  pl.BlockSpec((1, D_EXPERT, D_MODEL), lambda b, be, na, tk: (be[b], 0, 0))],
        out_specs=pl.BlockSpec((buf_rows, LANES), lambda b, be, na, tk: (b, 0)),
        scratch_shapes=[pltpu.VMEM((buf_rows, LANES), U32)] * GATHER_DEPTH
        + [pltpu.SemaphoreType.DMA((GATHER_DEPTH,))])
    return pl.pallas_call(
        _moe_kernel,
        grid_spec=grid_spec,
        out_shape=jax.ShapeDtypeStruct((n_blocks * buf_rows, LANES), U32),
        compiler_params=pltpu.CompilerParams(dimension_semantics=("arbitrary",),
                                             vmem_limit_bytes=VMEM_LIMIT),
    )(blk_expert, n_active, tok_buf, h2p, wg, wu, wd)


def _combine_kernel(dest_ref, y_hbm, x1_ref, wt_ref, g2_ref, lng_ref, lnb_ref, o_ref,
                    ybuf0, ybuf1, ybuf2, sem):
    i = pl.program_id(0)
    nt = pl.num_programs(0)
    tm = x1_ref.shape[0]
    count = TOP_K * tm
    bufs = (ybuf0, ybuf1, ybuf2)

    def fetch(tile_idx, c):
        _tile_gather_start(dest_ref, tile_idx * count, y_hbm, bufs[c], 0, sem.at[c], count)

    def drain(c):
        _tile_gather_wait(y_hbm, bufs[c], 0, sem.at[c], count)

    @pl.when(i == 0)
    def _():
        fetch(0, 0)
        fetch(jnp.minimum(1, nt - 1), 1)

    def tile(c):
        drain(c)
        first = [_unpack_halves(p) for p in _load_token_tiles(bufs[c], 0, tm)]
        second = [_unpack_halves(p) for p in _load_token_tiles(bufs[c], tm * TOKEN_TILE, tm)]
        fetch(jnp.minimum(i + 2, nt - 1), (c + 2) % GATHER_DEPTH)
        w1 = wt_ref[:, 0:1]
        w2 = wt_ref[:, 1:2]
        lo = [w1 * a[0] + w2 * b[0] for a, b in zip(first, second)]
        hi = [w1 * a[1] + w2 * b[1] for a, b in zip(first, second)]
        y = jnp.concatenate(lo + hi, axis=1)
        r = ALPHA * x1_ref[...] + (1.0 + g2_ref[0]) * y
        o_ref[...] = _layer_norm(r, lng_ref[...], lnb_ref[...])

        @pl.when(i == nt - 1)
        def _():
            drain((c + 1) % GATHER_DEPTH)
            drain((c + 2) % GATHER_DEPTH)

    for c in range(GATHER_DEPTH):
        pl.when(lax.rem(i, GATHER_DEPTH) == c)(functools.partial(tile, c))


def _combine(dest_tiled, y_buf, x1, wt, g2, ln_g, ln_b, seq, tm):
    n = x1.shape[0]
    per_batch = seq // tm
    grid_spec = pltpu.PrefetchScalarGridSpec(
        num_scalar_prefetch=1,
        grid=(n // tm,),
        in_specs=[pl.BlockSpec(memory_space=pl.ANY),
                  pl.BlockSpec((tm, D_MODEL), lambda i, d: (i, 0)),
                  pl.BlockSpec((tm, LANES), lambda i, d: (i, 0)),
                  pl.BlockSpec((1, 1, D_MODEL), lambda i, d: (i // per_batch, 0, 0)),
                  pl.BlockSpec((1, D_MODEL), lambda i, d: (0, 0)),
                  pl.BlockSpec((1, D_MODEL), lambda i, d: (0, 0))],
        out_specs=pl.BlockSpec((tm, D_MODEL), lambda i, d: (i, 0)),
        scratch_shapes=[pltpu.VMEM((TOP_K * tm * TOKEN_TILE, LANES), U32)] * GATHER_DEPTH
        + [pltpu.SemaphoreType.DMA((GATHER_DEPTH,))])
    return pl.pallas_call(
        _combine_kernel,
        grid_spec=grid_spec,
        out_shape=jax.ShapeDtypeStruct((n, D_MODEL), F32),
        compiler_params=pltpu.CompilerParams(dimension_semantics=("arbitrary",),
                                             vmem_limit_bytes=VMEM_LIMIT),
    )(dest_tiled, y_buf, x1, wt, g2, ln_g.reshape(1, D_MODEL), ln_b.reshape(1, D_MODEL))


def _dispatch_plan(route, counts, n):
    m = n * TOP_K
    n_blocks = -(-m // MOE_BLOCK) + N_EXPERTS
    expert = route[0:2].astype(jnp.int32)
    rank = route[4:6].astype(jnp.int32)
    sizes = counts[:, 0].astype(jnp.int32)
    padded = (sizes + MOE_BLOCK - 1) // MOE_BLOCK * MOE_BLOCK
    p_end = jnp.cumsum(padded)
    p_start = p_end - padded
    ids = jnp.arange(N_EXPERTS, dtype=jnp.int32)
    start_of = jnp.sum(jnp.where(expert[..., None] == ids, p_start, 0), axis=-1)
    dest = start_of + rank
    blk_start = jnp.arange(n_blocks, dtype=jnp.int32) * MOE_BLOCK
    blk_expert = jnp.minimum(jnp.sum((p_end[None, :] <= blk_start[:, None]).astype(jnp.int32), axis=1),
                             N_EXPERTS - 1)
    n_active = (p_end[-1] // MOE_BLOCK).astype(jnp.int32).reshape(1)
    pad_lo = p_start + sizes
    pad_hi = p_end.at[N_EXPERTS - 1].set(n_blocks * MOE_BLOCK)
    return dest, blk_expert, n_active, n_blocks, pad_lo, pad_hi


def kernel(x, c, w_ada, b_ada, w_in, dn_conv_w, dn_a_log, dn_dt_bias, dn_norm_w, fox_f_bias, w_out,
           ln1_g, ln1_b, w_router_group, b_router_group, w_router_expert, b_router_expert,
           w_gate, w_up, w_down, ln2_g, ln2_b):
    bsz, seq, _ = x.shape
    n = bsz * seq
    x2 = x.reshape(n, D_MODEL)

    mod = _ada(c, w_ada[0], b_ada[0])
    sh1, sc1, g1, sh2, sc2, g2 = [m.reshape(bsz, 1, D_MODEL) for m in jnp.split(mod, 6, axis=-1)]

    wi = w_in[0]
    o_b = 4 * DN_WIDTH
    o_a = o_b + DN_HEADS
    o_fq = o_a + DN_HEADS
    o_f = o_fq + 3 * FOX_WIDTH
    w_main = jnp.concatenate([wi[:, :o_b], wi[:, o_fq:o_f]], axis=1).astype(BF16)
    w_small = jnp.concatenate([wi[:, o_b:o_fq], wi[:, o_f:],
                               jnp.zeros((D_MODEL, LANES - 2 * DN_HEADS - FOX_HEADS), F32)],
                              axis=1).astype(BF16)
    zeros8 = jnp.zeros((DN_HEADS,), F32)
    pad = jnp.zeros((LANES - 2 * DN_HEADS - FOX_HEADS,), F32)
    gate_par = jnp.stack([jnp.concatenate([zeros8, dn_dt_bias[0], fox_f_bias[0], pad]),
                          jnp.concatenate([zeros8, dn_a_log[0], zeros8, pad])])

    main, gates = _inproj(x2, sc1, sh1, w_main, w_small, gate_par, seq)
    main3 = main.reshape(bsz, seq, D_MAIN)
    gates3 = gates.reshape(bsz, seq, LANES)
    gc3, fk_aug = _cumsum(gates3)
    gct = jnp.swapaxes(gc3[:, :, DN_HEADS:2 * DN_HEADS], 1, 2)

    o_dn = _deltanet(main3, gates3, gc3, gct, dn_conv_w[0], dn_norm_w[0])
    o_fox = _fox(main3, fk_aug)

    wo = w_out[0].astype(BF16)
    wr_t = jnp.concatenate([w_router_group[0].T, jnp.zeros((8 - N_GROUPS, D_MODEL), F32),
                            jnp.transpose(w_router_expert[0], (0, 2, 1)).reshape(N_EXPERTS, D_MODEL)],
                           axis=0)
    br_t = jnp.concatenate([b_router_group[0], jnp.zeros((8 - N_GROUPS,), F32),
                            b_router_expert[0].reshape(-1)])
    br_col = jnp.pad(br_t, (0, LANES - ROUTE_ROWS)).reshape(LANES, 1)
    wr_hi = wr_t.astype(BF16)
    wr_lo = (wr_t - wr_hi.astype(F32)).astype(BF16)
    lane_pad = ((0, 0), (0, LANES - ROUTE_ROWS))
    wr_split = jnp.concatenate([jnp.pad(wr_hi.T, lane_pad), jnp.pad(wr_lo.T, lane_pad)], axis=1)
    x1, h2p, route, counts = _outproj(o_dn.reshape(n, DN_WIDTH), o_fox.reshape(n, FOX_WIDTH),
                                      wo[:DN_WIDTH], wo[DN_WIDTH:], x2, g1, sc2, sh2, ln1_g[0], ln1_b[0],
                                      wr_split, br_col, seq)

    dest, blk_expert, n_active, n_blocks, pad_lo, pad_hi = _dispatch_plan(route, counts, n)
    tok_buf = _invert(dest.reshape(-1), pad_lo, pad_hi, n_blocks * MOE_BLOCK)
    y_buf = _moe(blk_expert, n_active, tok_buf, h2p,
                 w_gate[0].astype(BF16), w_up[0].astype(BF16), w_down[0].astype(BF16))

    tm = min(256, seq)
    dest_tiled = (dest * TOKEN_TILE).reshape(TOP_K, n // tm, tm).transpose(1, 0, 2).reshape(-1)
    wt = jnp.pad(route[2:4].T, ((0, 0), (0, LANES - TOP_K)))
    out = _combine(dest_tiled, y_buf, x1, wt, g2, ln2_g[0], ln2_b[0], seq, tm)
    return out.reshape(bsz, seq, D_MODEL)
```

```python
import functools
import math

import jax
import jax.numpy as jnp
from jax import lax
from jax.experimental import pallas as pl
from jax.experimental.pallas import tpu as pltpu

F32 = jnp.float32
BF16 = jnp.bfloat16
U32 = jnp.uint32
HIGHEST = lax.Precision.HIGHEST

D_MODEL = 2048
HEAD_DIM = 128
DN_HEADS = 8
FOX_HEADS = 8
DN_WIDTH = DN_HEADS * HEAD_DIM
FOX_WIDTH = FOX_HEADS * HEAD_DIM
CONV_WIDTH = 4
N_GROUPS = 4
EXPERTS_PER_GROUP = 8
N_EXPERTS = N_GROUPS * EXPERTS_PER_GROUP
TOP_K = 2
D_EXPERT = 512
MOE_BLOCK = 256
LN_EPS = 1e-5
NORM_EPS = 1e-6
ALPHA = 2.0 ** 0.25

D_MAIN = 3 * DN_WIDTH + DN_WIDTH + 3 * FOX_WIDTH
LANES = 128
BF16_SUBLANES = 16
DN_CHUNK = 128
NEG_BIG = -1e30
LOG2E = 1.4426950408889634
HALF = D_MODEL // 2
VMEM_LIMIT = 56 * 1024 * 1024


def _sigmoid(x):
    return 1.0 / (1.0 + jnp.exp(-x))


def _softplus(x):
    return jnp.maximum(x, 0.0) + jnp.log(1.0 + jnp.exp(-jnp.abs(x)))


def _mm(a, b):
    return jnp.dot(a.astype(BF16), b.astype(BF16), preferred_element_type=F32)


def _mm_nt(a, b):
    return lax.dot_general(a.astype(BF16), b.astype(BF16), (((1,), (1,)), ((), ())),
                           preferred_element_type=F32)


def _mm_tn(a, b):
    return lax.dot_general(a.astype(BF16), b.astype(BF16), (((0,), (0,)), ((), ())),
                           preferred_element_type=F32)


def _pack_halves(x):
    c = x.shape[1] // 2
    lo = pltpu.bitcast(x[:, :c].astype(BF16).astype(F32), U32)
    hi = pltpu.bitcast(x[:, c:].astype(BF16).astype(F32), U32)
    return (lo >> 16) | (hi & jnp.uint32(0xFFFF0000))


def _unpack_halves(p):
    lo = pltpu.bitcast(p << 16, F32)
    hi = pltpu.bitcast(p & jnp.uint32(0xFFFF0000), F32)
    return lo, hi


def _layer_norm(r, g, b):
    mu = jnp.mean(r, axis=-1, keepdims=True)
    d = r - mu
    var = jnp.mean(d * d, axis=-1, keepdims=True)
    return d * lax.rsqrt(var + LN_EPS) * g + b


def _ada_kernel(c_ref, w_ref, b_ref, o_ref):
    c = c_ref[...]
    a = c * _sigmoid(c)
    o_ref[...] = jnp.dot(a, w_ref[...], preferred_element_type=F32, precision=HIGHEST) + b_ref[...]


def _ada(c, w_ada, b_ada):
    bsz = c.shape[0]
    n_out = w_ada.shape[1]
    tn = 1024
    return pl.pallas_call(
        _ada_kernel,
        grid=(n_out // tn,),
        in_specs=[pl.BlockSpec((bsz, D_MODEL), lambda j: (0, 0)),
                  pl.BlockSpec((D_MODEL, tn), lambda j: (0, j)),
                  pl.BlockSpec((1, tn), lambda j: (0, j))],
        out_specs=pl.BlockSpec((bsz, tn), lambda j: (0, j)),
        out_shape=jax.ShapeDtypeStruct((bsz, n_out), F32),
        compiler_params=pltpu.CompilerParams(dimension_semantics=("arbitrary",),
                                             vmem_limit_bytes=VMEM_LIMIT),
    )(c, w_ada, b_ada.reshape(1, n_out))


def _inproj_kernel(x_ref, sc_ref, sh_ref, w_ref, ws_ref, par_ref, o_ref, g_ref, h_scr):
    @pl.when(pl.program_id(1) == 0)
    def _():
        h = (x_ref[...] * (1.0 + sc_ref[0]) + sh_ref[0]).astype(BF16)
        h_scr[...] = h
        s = jnp.dot(h, ws_ref[...], preferred_element_type=F32)
        lane = lax.broadcasted_iota(jnp.int32, s.shape, 1)
        sb = s + par_ref[0:1, :]
        neg_a = -jnp.exp(par_ref[1:2, :])
        beta = _sigmoid(s)
        g = neg_a * _softplus(sb)
        log_f = -_softplus(-sb)
        g_ref[...] = jnp.where(lane < DN_HEADS, beta,
                               jnp.where(lane < 2 * DN_HEADS, g,
                                         jnp.where(lane < 2 * DN_HEADS + FOX_HEADS, log_f, 0.0)))

    o_ref[...] = jnp.dot(h_scr[...], w_ref[...], preferred_element_type=F32).astype(BF16)


def _inproj(x2, sc1, sh1, w_main, w_small, gate_par, seq):
    n = x2.shape[0]
    tm = min(1024, seq)
    tn = D_MAIN // 4
    per_batch = seq // tm
    return pl.pallas_call(
        _inproj_kernel,
        grid=(n // tm, D_MAIN // tn),
        in_specs=[pl.BlockSpec((tm, D_MODEL), lambda i, j: (i, 0)),
                  pl.BlockSpec((1, 1, D_MODEL), lambda i, j: (i // per_batch, 0, 0)),
                  pl.BlockSpec((1, 1, D_MODEL), lambda i, j: (i // per_batch, 0, 0)),
                  pl.BlockSpec((D_MODEL, tn), lambda i, j: (0, j)),
                  pl.BlockSpec((D_MODEL, LANES), lambda i, j: (0, 0)),
                  pl.BlockSpec((2, LANES), lambda i, j: (0, 0))],
        out_specs=[pl.BlockSpec((tm, tn), lambda i, j: (i, j)),
                   pl.BlockSpec((tm, LANES), lambda i, j: (i, 0))],
        out_shape=[jax.ShapeDtypeStruct((n, D_MAIN), BF16),
                   jax.ShapeDtypeStruct((n, LANES), F32)],
        scratch_shapes=[pltpu.VMEM((tm, D_MODEL), BF16)],
        compiler_params=pltpu.CompilerParams(dimension_semantics=("arbitrary", "arbitrary"),
                                             vmem_limit_bytes=VMEM_LIMIT),
    )(x2, sc1, sh1, w_main, w_small, gate_par)


def _cumsum_kernel(g_ref, gc_ref, fk_ref, carry, tri_scr):
    first = jnp.logical_and(pl.program_id(0) == 0, pl.program_id(1) == 0)

    @pl.when(pl.program_id(1) == 0)
    def _():
        carry[...] = jnp.zeros_like(carry)

    x = g_ref[0]
    t = x.shape[0]

    @pl.when(first)
    def _():
        row = lax.broadcasted_iota(jnp.int32, (t, t), 0)
        col = lax.broadcasted_iota(jnp.int32, (t, t), 1)
        shift = int(math.log2(DN_CHUNK))
        tri_scr[:t, :] = jnp.where(col <= row, 1.0, 0.0).astype(BF16)
        tri_scr[t:, :] = jnp.where(jnp.logical_and(col <= row, (row >> shift) == (col >> shift)),
                                   1.0, 0.0).astype(BF16)

    hi = x.astype(BF16)
    r1 = x - hi.astype(F32)
    mid = r1.astype(BF16)
    lo = (r1 - mid.astype(F32)).astype(BF16)
    parts = jnp.dot(tri_scr[...], jnp.concatenate([hi, mid, lo], axis=1), preferred_element_type=F32)
    sums = parts[:, :LANES] + parts[:, LANES:2 * LANES] + parts[:, 2 * LANES:]
    cs = sums[:t]
    f = cs + carry[...]
    gc_ref[0] = sums[t:]
    carry[...] = carry[...] + cs[t - 1:t, :]

    lane = lax.broadcasted_iota(jnp.int32, (t, LANES), 1)
    for h in range(FOX_HEADS):
        c0 = 2 * DN_HEADS + h
        val = jnp.broadcast_to(f[:, c0:c0 + 1] * (-LOG2E), (t, LANES))
        hi = val.astype(BF16)
        r1 = val - hi.astype(F32)
        mid = r1.astype(BF16)
        lo = r1 - mid.astype(F32)
        fk_ref[0, :, h * LANES:(h + 1) * LANES] = jnp.where(
            lane == 0, hi.astype(F32),
            jnp.where(lane == 1, mid.astype(F32), jnp.where(lane == 2, lo, 0.0))).astype(BF16)


def _cumsum(gates3):
    bsz, seq, _ = gates3.shape
    t = min(512, seq)
    spec = pl.BlockSpec((1, t, LANES), lambda b, i: (b, i, 0))
    return pl.pallas_call(
        _cumsum_kernel,
        grid=(bsz, seq // t),
        in_specs=[spec],
        out_specs=[spec, pl.BlockSpec((1, t, FOX_HEADS * LANES), lambda b, i: (b, i, 0))],
        out_shape=[jax.ShapeDtypeStruct(gates3.shape, F32),
                   jax.ShapeDtypeStruct((bsz, seq, FOX_HEADS * LANES), BF16)],
        scratch_shapes=[pltpu.VMEM((1, LANES), F32), pltpu.VMEM((2 * t, t), BF16)],
        compiler_params=pltpu.CompilerParams(dimension_semantics=("arbitrary", "arbitrary")),
    )(gates3)


DN_CHUNKS_PER_STEP = 4


def _dn_kernel(qkv_ref, halo_ref, z_ref, gate_ref, gc_ref, gct_ref, convw_ref, normw_ref,
               o_ref, state_ref, xs_scr):
    t = pl.program_id(1)

    @pl.when(t == 0)
    def _():
        state_ref[...] = jnp.zeros_like(state_ref)

    c = DN_CHUNK
    d = HEAD_DIM
    heads = range(DN_HEADS)
    row = lax.broadcasted_iota(jnp.int32, (c, c), 0)
    col = lax.broadcasted_iota(jnp.int32, (c, c), 1)
    strict = col < row
    eye = (row == col).astype(F32)
    neg_diag8_f = jnp.where((row >> 3) == (col >> 3), -1.0, 0.0)
    merge_levels = range(3, int(math.log2(c)))
    low_f = [jnp.where((row >> (s + 1)) == (col >> (s + 1)),
                       jnp.where((row >> s) == (col >> s), 0.0, 1.0), 0.0) for s in merge_levels]
    halo_on = (t > 0).astype(F32)
    normw = normw_ref[...]

    def l2n(x):
        return x * lax.rsqrt(jnp.sum(x * x, axis=-1, keepdims=True) + NORM_EPS)

    def bdot(a, b):
        return jnp.dot(a, b, preferred_element_type=F32)

    def conv_head(ci, h):
        r0 = ci * c

        def conv_silu(part):
            off = part * DN_WIDTH + h * d
            slab = (ci * 3 + part) * DN_HEADS + h
            x = qkv_ref[0, r0:r0 + c, off:off + d].astype(F32)
            if ci == 0:
                prev = halo_ref[0, BF16_SUBLANES - 8:, off:off + d].astype(F32) * halo_on
            else:
                prev = qkv_ref[0, r0 - 8:r0, off:off + d].astype(F32)
            xs_scr[slab, 0:8, :] = prev
            xs_scr[slab, 8:, :] = x
            w = convw_ref[:, off:off + d]
            acc = x * w[CONV_WIDTH - 1:CONV_WIDTH]
            for k in range(1, CONV_WIDTH):
                acc = acc + xs_scr[slab, pl.ds(8 - k, c), :] * w[CONV_WIDTH - 1 - k:CONV_WIDTH - k]
            return acc * _sigmoid(acc)

        return l2n(conv_silu(0)) * (d ** -0.5), l2n(conv_silu(1)), conv_silu(2)

    def chunk_steps(ci, qkv):
        r0 = ci * c
        q = [x[0] for x in qkv]
        k = [x[1] for x in qkv]
        v = [x[2] for x in qkv]
        gates = gate_ref[0, r0:r0 + c, :]
        gcs = gc_ref[0, r0:r0 + c, :]
        gct = gct_ref[0, :, r0:r0 + c]
        beta = [gates[:, h:h + 1] for h in heads]
        g_col = [gcs[:, DN_HEADS + h:DN_HEADS + h + 1] for h in heads]
        e_g = [jnp.exp(g) for g in g_col]
        g_last = [g[c - 1:c, :] for g in g_col]
        decay = [jnp.exp(jnp.where(strict, g_col[h] - gct[h:h + 1, :], NEG_BIG)) for h in heads]
        kb = [k[h] * beta[h] for h in heads]

        kk = [_mm_nt(jnp.concatenate([kb[h], q[h]], axis=0), k[h]) for h in heads]
        yield
        a = [kk[h][:c] * decay[h] for h in heads]
        qk = [(kk[h][c:] * (decay[h] + eye)).astype(BF16) for h in heads]
        n1 = [a[h] * neg_diag8_f for h in heads]
        n1b = [n.astype(BF16) for n in n1]
        x = [eye + n for n in n1]
        xb = [xx.astype(BF16) for xx in x]
        n2b = [bdot(n, n).astype(BF16) for n in n1b]
        yield
        x = [x[h] + bdot(n2b[h], xb[h]) for h in heads]
        n4b = [bdot(n, n).astype(BF16) for n in n2b]
        yield
        xb = [xx.astype(BF16) for xx in x]
        x = [x[h] + bdot(n4b[h], xb[h]) for h in heads]
        yield
        for lf in low_f:
            xb = [xx.astype(BF16) for xx in x]
            lx = [bdot((a[h] * lf).astype(BF16), xb[h]).astype(BF16) for h in heads]
            yield
            x = [x[h] - bdot(xb[h], lx[h]) for h in heads]
            yield

        uw = [_mm(x[h], jnp.concatenate([v[h] * beta[h], kb[h] * e_g[h]], axis=1)) for h in heads]
        yield
        s_old = [state_ref[h] for h in heads]
        r = [_mm(jnp.concatenate([uw[h][:, d:], q[h] * e_g[h]], axis=0), s_old[h]) for h in heads]
        yield
        v_new = [(uw[h][:, :d] - r[h][:c]).astype(BF16) for h in heads]
        o = [r[h][c:] + bdot(qk[h], v_new[h]) for h in heads]
        for h in heads:
            kd = k[h] * jnp.exp(g_last[h] - g_col[h])
            state_ref[h] = s_old[h] * jnp.exp(g_last[h]) + _mm_tn(kd, v_new[h])
        yield
        for h in heads:
            z = z_ref[0, r0:r0 + c, h * d:(h + 1) * d].astype(F32)
            rms = lax.rsqrt(jnp.mean(o[h] * o[h], axis=-1, keepdims=True) + NORM_EPS)
            o_ref[0, r0:r0 + c, h * d:(h + 1) * d] = (o[h] * rms * normw * (z * _sigmoid(z))).astype(BF16)

    qkv = [conv_head(0, h) for h in heads]
    for ci in range(DN_CHUNKS_PER_STEP):
        following = []
        more = ci + 1 < DN_CHUNKS_PER_STEP
        for _ in chunk_steps(ci, qkv):
            if more and len(following) < DN_HEADS:
                following.append(conv_head(ci + 1, len(following)))
        while more and len(following) < DN_HEADS:
            following.append(conv_head(ci + 1, len(following)))
        qkv = following


def _deltanet(main3, gates3, gc3, gct, conv_w, norm_w):
    bsz, seq, _ = main3.shape
    rows = DN_CHUNK * DN_CHUNKS_PER_STEP
    halo_blocks = rows // BF16_SUBLANES
    return pl.pallas_call(
        _dn_kernel,
        grid=(bsz, seq // rows),
        in_specs=[pl.BlockSpec((1, rows, 3 * DN_WIDTH), lambda b, t: (b, t, 0)),
                  pl.BlockSpec((1, BF16_SUBLANES, 3 * DN_WIDTH),
                               lambda b, t: (b, jnp.maximum(t * halo_blocks - 1, 0), 0)),
                  pl.BlockSpec((1, rows, DN_WIDTH), lambda b, t: (b, t, 3)),
                  pl.BlockSpec((1, rows, LANES), lambda b, t: (b, t, 0)),
                  pl.BlockSpec((1, rows, LANES), lambda b, t: (b, t, 0)),
                  pl.BlockSpec((1, DN_HEADS, rows), lambda b, t: (b, 0, t)),
                  pl.BlockSpec((CONV_WIDTH, 3 * DN_WIDTH), lambda b, t: (0, 0)),
                  pl.BlockSpec((1, HEAD_DIM), lambda b, t: (0, 0))],
        out_specs=pl.BlockSpec((1, rows, DN_WIDTH), lambda b, t: (b, t, 0)),
        out_shape=jax.ShapeDtypeStruct((bsz, seq, DN_WIDTH), BF16),
        scratch_shapes=[pltpu.VMEM((DN_HEADS, HEAD_DIM, HEAD_DIM), F32),
                        pltpu.VMEM((3 * DN_HEADS * DN_CHUNKS_PER_STEP, 8 + DN_CHUNK, HEAD_DIM), F32)],
        compiler_params=pltpu.CompilerParams(dimension_semantics=("arbitrary", "arbitrary"),
                                             vmem_limit_bytes=VMEM_LIMIT),
    )(main3, main3, main3, gates3, gc3, gct, conv_w, norm_w.reshape(1, HEAD_DIM))


def _fox_kernel(qi_ref, kj_ref, q_ref, k_ref, v_ref, fk_ref, o_ref, qa_scr, m_scr, l_scr, acc_scr):
    step = pl.program_id(1)
    i = qi_ref[step]
    j = kj_ref[step]
    d = HEAD_DIM
    tq = q_ref.shape[1]
    tk = k_ref.shape[1]

    @pl.when(j == 0)
    def _():
        m_scr[...] = jnp.full_like(m_scr, NEG_BIG)
        l_scr[...] = jnp.zeros_like(l_scr)
        acc_scr[...] = jnp.zeros_like(acc_scr)
        lane = lax.broadcasted_iota(jnp.int32, (tq, LANES), 1)
        ones = jnp.where(lane < 3, 1.0, 0.0).astype(BF16)
        for h in range(FOX_HEADS):
            q = (q_ref[0, :, h * d:(h + 1) * d].astype(F32) * (d ** -0.5 * LOG2E)).astype(BF16)
            qa_scr[h] = jnp.concatenate([q, ones], axis=1)

    def update(diagonal):
        if diagonal:
            key_pos = lax.broadcasted_iota(jnp.int32, (tk, tq), 0)
            q_pos = lax.broadcasted_iota(jnp.int32, (tk, tq), 1)
            visible = key_pos <= q_pos

        def scores(h):
            k_aug = jnp.concatenate([k_ref[0, :, h * d:(h + 1) * d],
                                     fk_ref[0, :, h * LANES:(h + 1) * LANES]], axis=1)
            return lax.dot_general(k_aug, qa_scr[h], (((1,), (1,)), ((), ())),
                                   preferred_element_type=F32)

        heads = range(FOX_HEADS)
        s = [scores(h) for h in heads]
        if diagonal:
            s = [jnp.where(visible, x, NEG_BIG) for x in s]
        m_old = [m_scr[h] for h in heads]
        m_new = [jnp.maximum(m_old[h], jnp.max(s[h], axis=0, keepdims=True)) for h in heads]
        alpha = [jnp.exp2(m_old[h] - m_new[h]) for h in heads]
        p = [jnp.exp2(s[h] - m_new[h]) for h in heads]
        for h in heads:
            l_scr[h] = alpha[h] * l_scr[h] + jnp.sum(p[h], axis=0, keepdims=True)
            acc_scr[h] = alpha[h] * acc_scr[h] + _mm_tn(v_ref[0, :, h * d:(h + 1) * d], p[h])
            m_scr[h] = m_new[h]

    @pl.when(j < i)
    def _():
        update(False)

    @pl.when(j == i)
    def _():
        update(True)
        for h in range(FOX_HEADS):
            o = acc_scr[h] / l_scr[h]
            o_ref[0, :, h * d:(h + 1) * d] = o.T.astype(BF16)


def _fox(main3, fk_aug):
    bsz, seq, _ = main3.shape
    tq = min(512, seq)
    n_blk = seq // tq
    w = FOX_WIDTH
    base = (4 * DN_WIDTH) // w
    pairs = [(i, j) for i in range(n_blk) for j in range(i + 1)]
    qi = jnp.array([p[0] for p in pairs], jnp.int32)
    kj = jnp.array([p[1] for p in pairs], jnp.int32)
    grid_spec = pltpu.PrefetchScalarGridSpec(
        num_scalar_prefetch=2,
        grid=(bsz, len(pairs)),
        in_specs=[pl.BlockSpec((1, tq, w), lambda b, t, qi, kj: (b, qi[t], base)),
                  pl.BlockSpec((1, tq, w), lambda b, t, qi, kj: (b, kj[t], base + 1)),
                  pl.BlockSpec((1, tq, w), lambda b, t, qi, kj: (b, kj[t], base + 2)),
                  pl.BlockSpec((1, tq, FOX_HEADS * LANES), lambda b, t, qi, kj: (b, kj[t], 0))],
        out_specs=pl.BlockSpec((1, tq, w), lambda b, t, qi, kj: (b, qi[t], 0)),
        scratch_shapes=[pltpu.VMEM((FOX_HEADS, tq, 2 * HEAD_DIM), BF16),
                        pltpu.VMEM((FOX_HEADS, 1, tq), F32),
                        pltpu.VMEM((FOX_HEADS, 1, tq), F32),
                        pltpu.VMEM((FOX_HEADS, HEAD_DIM, tq), F32)])
    return pl.pallas_call(
        _fox_kernel,
        grid_spec=grid_spec,
        out_shape=jax.ShapeDtypeStruct((bsz, seq, w), BF16),
        compiler_params=pltpu.CompilerParams(
            dimension_semantics=("arbitrary", "arbitrary"),
            vmem_limit_bytes=VMEM_LIMIT),
    )(qi, kj, main3, main3, main3, fk_aug)


ROUTE_ROWS = 8 + N_EXPERTS
TOKEN_TILE = 8


def _store_token_tiles(ref, packed, first_token=0):
    m = packed.shape[0]
    for s in range(TOKEN_TILE):
        ref[pl.ds(first_token * TOKEN_TILE + s, m, stride=TOKEN_TILE), :] = packed[:, s * LANES:(s + 1) * LANES]


def _load_token_tiles(ref, start, m):
    return [ref[pl.ds(start + s, m, stride=TOKEN_TILE), :] for s in range(TOKEN_TILE)]


def _outproj_kernel(odn_ref, ofox_ref, w1_ref, w2_ref, x_ref, g1_ref, sc2_ref, sh2_ref,
                    lng_ref, lnb_ref, wr_ref, br_ref, x1_ref, h2p_ref, route_ref, count_ref,
                    upper_scr, carry_scr):
    tm = x_ref.shape[0]

    @pl.when(pl.program_id(0) == 0)
    def _():
        r = lax.broadcasted_iota(jnp.int32, (tm, tm), 0)
        c = lax.broadcasted_iota(jnp.int32, (tm, tm), 1)
        upper_scr[...] = jnp.where(r < c, 1.0, 0.0).astype(BF16)
        carry_scr[...] = jnp.zeros_like(carry_scr)

    y = (jnp.dot(odn_ref[...], w1_ref[...], preferred_element_type=F32)
         + jnp.dot(ofox_ref[...], w2_ref[...], preferred_element_type=F32))
    _outproj_tail(0, tm, y, x_ref, g1_ref, sc2_ref, sh2_ref, lng_ref, lnb_ref, wr_ref, br_ref,
                  x1_ref, h2p_ref, route_ref, upper_scr, carry_scr)
    count_ref[...] = jnp.broadcast_to(carry_scr[...], count_ref.shape)


def _outproj_tail(u, tm, y, x_ref, g1_ref, sc2_ref, sh2_ref, lng_ref, lnb_ref, wr_ref, br_ref,
                  x1_ref, h2p_ref, route_ref, upper_scr, carry_scr):
    rows = slice(u * tm, (u + 1) * tm)
    r = ALPHA * x_ref[rows, :] + (1.0 + g1_ref[0]) * y
    x1 = _layer_norm(r, lng_ref[...], lnb_ref[...])
    x1_ref[rows, :] = x1
    h2 = x1 * (1.0 + sc2_ref[0]) + sh2_ref[0]
    _store_token_tiles(h2p_ref, _pack_halves(h2), u * tm)

    h_hi = h2.astype(BF16)
    h_lo = (h2 - h_hi.astype(F32)).astype(BF16)
    parts = jnp.dot(jnp.concatenate([h_hi, h_lo], axis=0), wr_ref[...], preferred_element_type=F32)
    terms = parts[:tm] + parts[tm:]
    lt = (terms[:, :LANES] + terms[:, LANES:]).T + br_ref[...]
    row = [lt[n:n + 1, :] for n in range(ROUTE_ROWS)]

    def first_max(vals):
        best = vals[0]
        for val in vals[1:]:
            best = jnp.maximum(best, val)
        idx = jnp.full(best.shape, len(vals) - 1, jnp.int32)
        for n in range(len(vals) - 2, -1, -1):
            idx = jnp.where(vals[n] == best, n, idx)
        return best, idx

    gmax, gidx = first_max(row[:N_GROUPS])
    gsum = jnp.exp(row[0] - gmax)
    for n in range(1, N_GROUPS):
        gsum = gsum + jnp.exp(row[n] - gmax)
    g_w = 1.0 / gsum
    el = []
    for e in range(EXPERTS_PER_GROUP):
        val = row[8 + e]
        for g in range(1, N_GROUPS):
            val = jnp.where(gidx == g, row[8 + EXPERTS_PER_GROUP * g + e], val)
        el.append(val)
    emax, _ = first_max(el)
    ee = [jnp.exp(val - emax) for val in el]
    esum = ee[0]
    for val in ee[1:]:
        esum = esum + val
    prob = [val / esum for val in ee]
    p1, e1 = first_max(prob)
    p2, e2 = first_max([jnp.where(e1 == e, -1.0, prob[e]) for e in range(EXPERTS_PER_GROUP)])
    denom = p1 + p2
    ex1 = gidx * EXPERTS_PER_GROUP + e1
    ex2 = gidx * EXPERTS_PER_GROUP + e2
    ridx = lax.broadcasted_iota(jnp.int32, (8, tm), 0)

    eidx = lax.broadcasted_iota(jnp.int32, (N_EXPERTS, tm), 0)
    oh1 = jnp.where(eidx == ex1, 1.0, 0.0)
    oh2 = jnp.where(eidx == ex2, 1.0, 0.0)
    both = oh1 + oh2
    before = jnp.dot(both.astype(BF16), upper_scr[:tm, :tm], preferred_element_type=F32) + carry_scr[...]
    rank1 = jnp.sum(oh1 * before, axis=0, keepdims=True)
    rank2 = jnp.sum(oh2 * before, axis=0, keepdims=True)
    carry_scr[...] = carry_scr[...] + jnp.sum(both, axis=1, keepdims=True)

    fields = [ex1.astype(F32), ex2.astype(F32), g_w * p1 / denom, g_w * p2 / denom, rank1, rank2]
    out = jnp.zeros((8, tm), F32)
    for n, val in enumerate(fields):
        out = jnp.where(ridx == n, val, out)
    route_ref[:, rows] = out


def _outproj(o_dn, o_fox, w1, w2, x2, g1, sc2, sh2, ln_g, ln_b, wr_t, br_t, seq):
    n = x2.shape[0]
    tm = min(512, seq)
    per_batch = seq // tm
    row = lambda i: (i, 0)
    fixed = lambda i: (0, 0)
    mod = pl.BlockSpec((1, 1, D_MODEL), lambda i: (i // per_batch, 0, 0))
    return pl.pallas_call(
        _outproj_kernel,
        grid=(n // tm,),
        in_specs=[pl.BlockSpec((tm, DN_WIDTH), row),
                  pl.BlockSpec((tm, FOX_WIDTH), row),
                  pl.BlockSpec((DN_WIDTH, D_MODEL), fixed),
                  pl.BlockSpec((FOX_WIDTH, D_MODEL), fixed),
                  pl.BlockSpec((tm, D_MODEL), row),
                  mod, mod, mod,
                  pl.BlockSpec((1, D_MODEL), fixed),
                  pl.BlockSpec((1, D_MODEL), fixed),
                  pl.BlockSpec((D_MODEL, 2 * LANES), fixed),
                  pl.BlockSpec((LANES, 1), fixed)],
        out_specs=[pl.BlockSpec((tm, D_MODEL), row),
                   pl.BlockSpec((tm * TOKEN_TILE, LANES), row),
                   pl.BlockSpec((8, tm), lambda i: (0, i)),
                   pl.BlockSpec((N_EXPERTS, LANES), fixed)],
        out_shape=[jax.ShapeDtypeStruct((n, D_MODEL), F32),
                   jax.ShapeDtypeStruct((n * TOKEN_TILE, LANES), U32),
                   jax.ShapeDtypeStruct((8, n), F32),
                   jax.ShapeDtypeStruct((N_EXPERTS, LANES), F32)],
        scratch_shapes=[pltpu.VMEM((tm, tm), BF16),
                        pltpu.VMEM((N_EXPERTS, 1), F32)],
        compiler_params=pltpu.CompilerParams(dimension_semantics=("arbitrary",),
                                             vmem_limit_bytes=VMEM_LIMIT),
    )(o_dn, o_fox, w1, w2, x2, g1, sc2, sh2, ln_g.reshape(1, D_MODEL), ln_b.reshape(1, D_MODEL),
      wr_t, br_t)


INVERT_STEPS = N_EXPERTS
INVERT_BATCH = 16


def _invert_kernel(dest_ref, pad_lo_ref, pad_hi_ref, tok_ref):
    phase = pl.program_id(0)
    i = pl.program_id(1)
    n_tok = dest_ref.shape[0] // TOP_K
    toks_per = n_tok // INVERT_STEPS

    @pl.when(phase == 0)
    def _():
        def clear(p, carry):
            tok_ref[p] = 0
            return carry
        lax.fori_loop(pad_lo_ref[i], pad_hi_ref[i], clear, 0)

    @pl.when(phase == 1)
    def _():
        def put(t, carry):
            first = i * toks_per + t * INVERT_BATCH
            rows = [[dest_ref[k * n_tok + first + u] for k in range(TOP_K)] for u in range(INVERT_BATCH)]
            for u in range(INVERT_BATCH):
                for k in range(TOP_K):
                    tok_ref[rows[u][k]] = (first + u) * TOKEN_TILE
            return carry
        lax.fori_loop(0, toks_per // INVERT_BATCH, put, 0)


def _invert(dest_flat, pad_lo, pad_hi, n_slots):
    assert (dest_flat.shape[0] // TOP_K) % (INVERT_STEPS * INVERT_BATCH) == 0
    smem = pl.BlockSpec(memory_space=pltpu.SMEM)
    return pl.pallas_call(
        _invert_kernel,
        grid=(2, INVERT_STEPS),
        in_specs=[smem, smem, smem],
        out_specs=smem,
        out_shape=jax.ShapeDtypeStruct((n_slots,), jnp.int32),
        compiler_params=pltpu.CompilerParams(dimension_semantics=("arbitrary", "arbitrary")),
    )(dest_flat, pad_lo, pad_hi)


def _tile_gather_start(idx_ref, base, src_hbm, dst, dst_row0, sem, count):
    for r in range(count):
        row = pl.multiple_of(idx_ref[base + r], TOKEN_TILE)
        pltpu.make_async_copy(src_hbm.at[pl.ds(row, TOKEN_TILE)],
                              dst.at[pl.ds(dst_row0 + r * TOKEN_TILE, TOKEN_TILE)],
                              sem).start(priority=r % 2)


def _tile_gather_wait(src_hbm, dst, dst_row0, sem, count):
    rows = count * TOKEN_TILE
    pltpu.make_async_copy(src_hbm.at[pl.ds(0, rows)], dst.at[pl.ds(dst_row0, rows)], sem).wait()


GATHER_DEPTH = 3


def _moe_kernel(blk_expert_ref, n_active_ref, tok_ref, h_hbm, wg_ref, wu_ref, wd_ref,
                y_ref, xbuf0, xbuf1, xbuf2, sem, wg_bf, wu_bf, wd_bf):
    b = pl.program_id(0)
    nb = pl.num_programs(0)
    n_active = n_active_ref[0]
    last = jnp.maximum(jnp.minimum(n_active, nb), 1) - 1
    bufs = (xbuf0, xbuf1, xbuf2)

    new_expert = jnp.logical_or(b == 0, blk_expert_ref[b] != blk_expert_ref[jnp.maximum(b - 1, 0)])

    @pl.when(jnp.logical_and(b <= last, new_expert))
    def _():
        wg_bf[...] = wg_ref[0].astype(BF16)
        wu_bf[...] = wu_ref[0].astype(BF16)
        wd_bf[...] = wd_ref[0].astype(BF16)

    def fetch(blk, c):
        _tile_gather_start(tok_ref, blk * MOE_BLOCK, h_hbm, bufs[c], 0, sem.at[c], MOE_BLOCK)

    def drain(c):
        _tile_gather_wait(h_hbm, bufs[c], 0, sem.at[c], MOE_BLOCK)

    @pl.when(b == 0)
    def _():
        fetch(0, 0)
        fetch(jnp.minimum(1, last), 1)

    def block(c):
        drain(c)
        parts = [_unpack_halves(p) for p in _load_token_tiles(bufs[c], 0, MOE_BLOCK)]
        lo = jnp.concatenate([p[0] for p in parts], axis=1).astype(BF16)
        hi = jnp.concatenate([p[1] for p in parts], axis=1).astype(BF16)
        fetch(jnp.minimum(b + 2, last), (c + 2) % GATHER_DEPTH)
        gate = (jnp.dot(lo, wg_bf[:HALF, :], preferred_element_type=F32)
                + jnp.dot(hi, wg_bf[HALF:, :], preferred_element_type=F32))
        up = (jnp.dot(lo, wu_bf[:HALF, :], preferred_element_type=F32)
              + jnp.dot(hi, wu_bf[HALF:, :], preferred_element_type=F32))
        hid = (gate * _sigmoid(gate) * up).astype(BF16)
        y = jnp.dot(hid, wd_bf[...], preferred_element_type=F32)
        _store_token_tiles(y_ref, _pack_halves(y))

        @pl.when(b == last)
        def _():
            drain((c + 1) % GATHER_DEPTH)
            drain((c + 2) % GATHER_DEPTH)

    for c in range(GATHER_DEPTH):
        pl.when(jnp.logical_and(b <= last, lax.rem(b, GATHER_DEPTH) == c))(functools.partial(block, c))

    @pl.when(b > last)
    def _():
        y_ref[...] = jnp.zeros_like(y_ref)


def _moe(blk_expert, n_active, tok_buf, h2p, wg, wu, wd):
    n_blocks = blk_expert.shape[0]
    buf_rows = MOE_BLOCK * TOKEN_TILE
    grid_spec = pltpu.PrefetchScalarGridSpec(
        num_scalar_prefetch=3,
        grid=(n_blocks,),
        in_specs=[pl.BlockSpec(memory_space=pl.ANY),
                  pl.BlockSpec((1, D_MODEL, D_EXPERT), lambda b, be, na, tk: (be[b], 0, 0)),
                  pl.BlockSpec((1, D_MODEL, D_EXPERT), lambda b, be, na, tk: (be[b], 0, 0)),
                  pl.BlockSpec((1, D_EXPERT, D_MODEL), lambda b, be, na, tk: (be[b], 0, 0))],
        out_specs=pl.BlockSpec((buf_rows, LANES), lambda b, be, na, tk: (b, 0)),
        scratch_shapes=[pltpu.VMEM((buf_rows, LANES), U32)] * GATHER_DEPTH
        + [pltpu.SemaphoreType.DMA((GATHER_DEPTH,)),
           pltpu.VMEM((D_MODEL, D_EXPERT), BF16),
           pltpu.VMEM((D_MODEL, D_EXPERT), BF16),
           pltpu.VMEM((D_EXPERT, D_MODEL), BF16)])
    return pl.pallas_call(
        _moe_kernel,
        grid_spec=grid_spec,
        out_shape=jax.ShapeDtypeStruct((n_blocks * buf_rows, LANES), U32),
        compiler_params=pltpu.CompilerParams(dimension_semantics=("arbitrary",),
                                             vmem_limit_bytes=VMEM_LIMIT),
    )(blk_expert, n_active, tok_buf, h2p, wg, wu, wd)


def _combine_kernel(dest_ref, y_hbm, x1_ref, wt_ref, g2_ref, lng_ref, lnb_ref, o_ref,
                    ybuf0, ybuf1, ybuf2, sem):
    i = pl.program_id(0)
    nt = pl.num_programs(0)
    tm = x1_ref.shape[0]
    count = TOP_K * tm
    bufs = (ybuf0, ybuf1, ybuf2)

    def fetch(tile_idx, c):
        _tile_gather_start(dest_ref, tile_idx * count, y_hbm, bufs[c], 0, sem.at[c], count)

    def drain(c):
        _tile_gather_wait(y_hbm, bufs[c], 0, sem.at[c], count)

    @pl.when(i == 0)
    def _():
        fetch(0, 0)
        fetch(jnp.minimum(1, nt - 1), 1)

    def tile(c):
        drain(c)
        first = [_unpack_halves(p) for p in _load_token_tiles(bufs[c], 0, tm)]
        second = [_unpack_halves(p) for p in _load_token_tiles(bufs[c], tm * TOKEN_TILE, tm)]
        fetch(jnp.minimum(i + 2, nt - 1), (c + 2) % GATHER_DEPTH)
        w1 = wt_ref[:, 0:1]
        w2 = wt_ref[:, 1:2]
        lo = [w1 * a[0] + w2 * b[0] for a, b in zip(first, second)]
        hi = [w1 * a[1] + w2 * b[1] for a, b in zip(first, second)]
        y = jnp.concatenate(lo + hi, axis=1)
        r = ALPHA * x1_ref[...] + (1.0 + g2_ref[0]) * y
        o_ref[...] = _layer_norm(r, lng_ref[...], lnb_ref[...])

        @pl.when(i == nt - 1)
        def _():
            drain((c + 1) % GATHER_DEPTH)
            drain((c + 2) % GATHER_DEPTH)

    for c in range(GATHER_DEPTH):
        pl.when(lax.rem(i, GATHER_DEPTH) == c)(functools.partial(tile, c))


def _combine(dest_tiled, y_buf, x1, wt, g2, ln_g, ln_b, seq, tm):
    n = x1.shape[0]
    per_batch = seq // tm
    grid_spec = pltpu.PrefetchScalarGridSpec(
        num_scalar_prefetch=1,
        grid=(n // tm,),
        in_specs=[pl.BlockSpec(memory_space=pl.ANY),
                  pl.BlockSpec((tm, D_MODEL), lambda i, d: (i, 0)),
                  pl.BlockSpec((tm, LANES), lambda i, d: (i, 0)),
                  pl.BlockSpec((1, 1, D_MODEL), lambda i, d: (i // per_batch, 0, 0)),
                  pl.BlockSpec((1, D_MODEL), lambda i, d: (0, 0)),
                  pl.BlockSpec((1, D_MODEL), lambda i, d: (0, 0))],
        out_specs=pl.BlockSpec((tm, D_MODEL), lambda i, d: (i, 0)),
        scratch_shapes=[pltpu.VMEM((TOP_K * tm * TOKEN_TILE, LANES), U32)] * GATHER_DEPTH
        + [pltpu.SemaphoreType.DMA((GATHER_DEPTH,))])
    return pl.pallas_call(
        _combine_kernel,
        grid_spec=grid_spec,
        out_shape=jax.ShapeDtypeStruct((n, D_MODEL), F32),
        compiler_params=pltpu.CompilerParams(dimension_semantics=("arbitrary",),
                                             vmem_limit_bytes=VMEM_LIMIT),
    )(dest_tiled, y_buf, x1, wt, g2, ln_g.reshape(1, D_MODEL), ln_b.reshape(1, D_MODEL))


def _dispatch_plan(route, counts, n):
    m = n * TOP_K
    n_blocks = -(-m // MOE_BLOCK) + N_EXPERTS
    expert = route[0:2].astype(jnp.int32)
    rank = route[4:6].astype(jnp.int32)
    sizes = counts[:, 0].astype(jnp.int32)
    padded = (sizes + MOE_BLOCK - 1) // MOE_BLOCK * MOE_BLOCK
    p_end = jnp.cumsum(padded)
    p_start = p_end - padded
    ids = jnp.arange(N_EXPERTS, dtype=jnp.int32)
    start_of = jnp.sum(jnp.where(expert[..., None] == ids, p_start, 0), axis=-1)
    dest = start_of + rank
    blk_start = jnp.arange(n_blocks, dtype=jnp.int32) * MOE_BLOCK
    blk_expert = jnp.minimum(jnp.sum((p_end[None, :] <= blk_start[:, None]).astype(jnp.int32), axis=1),
                             N_EXPERTS - 1)
    n_active = (p_end[-1] // MOE_BLOCK).astype(jnp.int32).reshape(1)
    pad_lo = p_start + sizes
    pad_hi = p_end.at[N_EXPERTS - 1].set(n_blocks * MOE_BLOCK)
    return dest, blk_expert, n_active, n_blocks, pad_lo, pad_hi


def kernel(x, c, w_ada, b_ada, w_in, dn_conv_w, dn_a_log, dn_dt_bias, dn_norm_w, fox_f_bias, w_out,
           ln1_g, ln1_b, w_router_group, b_router_group, w_router_expert, b_router_expert,
           w_gate, w_up, w_down, ln2_g, ln2_b):
    bsz, seq, _ = x.shape
    n = bsz * seq
    x2 = x.reshape(n, D_MODEL)

    mod = _ada(c, w_ada[0], b_ada[0])
    sh1, sc1, g1, sh2, sc2, g2 = [m.reshape(bsz, 1, D_MODEL) for m in jnp.split(mod, 6, axis=-1)]

    wi = w_in[0]
    o_b = 4 * DN_WIDTH
    o_a = o_b + DN_HEADS
    o_fq = o_a + DN_HEADS
    o_f = o_fq + 3 * FOX_WIDTH
    w_main = jnp.concatenate([wi[:, :o_b], wi[:, o_fq:o_f]], axis=1).astype(BF16)
    w_small = jnp.concatenate([wi[:, o_b:o_fq], wi[:, o_f:],
                               jnp.zeros((D_MODEL, LANES - 2 * DN_HEADS - FOX_HEADS), F32)],
                              axis=1).astype(BF16)
    zeros8 = jnp.zeros((DN_HEADS,), F32)
    pad = jnp.zeros((LANES - 2 * DN_HEADS - FOX_HEADS,), F32)
    gate_par = jnp.stack([jnp.concatenate([zeros8, dn_dt_bias[0], fox_f_bias[0], pad]),
                          jnp.concatenate([zeros8, dn_a_log[0], zeros8, pad])])

    main, gates = _inproj(x2, sc1, sh1, w_main, w_small, gate_par, seq)
    main3 = main.reshape(bsz, seq, D_MAIN)
    gates3 = gates.reshape(bsz, seq, LANES)
    gc3, fk_aug = _cumsum(gates3)
    gct = jnp.swapaxes(gc3[:, :, DN_HEADS:2 * DN_HEADS], 1, 2)

    o_dn = _deltanet(main3, gates3, gc3, gct, dn_conv_w[0], dn_norm_w[0])
    o_fox = _fox(main3, fk_aug)

    wo = w_out[0].astype(BF16)
    wr_t = jnp.concatenate([w_router_group[0].T, jnp.zeros((8 - N_GROUPS, D_MODEL), F32),
                            jnp.transpose(w_router_expert[0], (0, 2, 1)).reshape(N_EXPERTS, D_MODEL)],
                           axis=0)
    br_t = jnp.concatenate([b_router_group[0], jnp.zeros((8 - N_GROUPS,), F32),
                            b_router_expert[0].reshape(-1)])
    br_col = jnp.pad(br_t, (0, LANES - ROUTE_ROWS)).reshape(LANES, 1)
    wr_hi = wr_t.astype(BF16)
    wr_lo = (wr_t - wr_hi.astype(F32)).astype(BF16)
    lane_pad = ((0, 0), (0, LANES - ROUTE_ROWS))
    wr_split = jnp.concatenate([jnp.pad(wr_hi.T, lane_pad), jnp.pad(wr_lo.T, lane_pad)], axis=1)
    x1, h2p, route, counts = _outproj(o_dn.reshape(n, DN_WIDTH), o_fox.reshape(n, FOX_WIDTH),
                                      wo[:DN_WIDTH], wo[DN_WIDTH:], x2, g1, sc2, sh2, ln1_g[0], ln1_b[0],
                                      wr_split, br_col, seq)

    dest, blk_expert, n_active, n_blocks, pad_lo, pad_hi = _dispatch_plan(route, counts, n)
    tok_buf = _invert(dest.reshape(-1), pad_lo, pad_hi, n_blocks * MOE_BLOCK)
    y_buf = _moe(blk_expert, n_active, tok_buf, h2p, w_gate[0], w_up[0], w_down[0])

    tm = min(256, seq)
    dest_tiled = (dest * TOKEN_TILE).reshape(TOP_K, n // tm, tm).transpose(1, 0, 2).reshape(-1)
    wt = jnp.pad(route[2:4].T, ((0, 0), (0, LANES - TOP_K)))
    out = _combine(dest_tiled, y_buf, x1, wt, g2, ln2_g[0], ln2_b[0], seq, tm)
    return out.reshape(bsz, seq, D_MODEL)
```
